```python
import jax
import jax.numpy as jnp
from jax import lax
import numpy as np


D_MODEL = 1024
BATCH = 1
SEQ = 16384
DEPTH = 1
DEC_BATCH = 32
DEC_SEQ = 32
PAST_LEN = 2048

CHUNK = 64
BAND_CHUNKS = 8
BAND_PAST = BAND_CHUNKS * CHUNK
BAND = BAND_PAST + CHUNK
N_HEADS_A = 8
HEAD_DIM_A = 64
D_A = N_HEADS_A * HEAD_DIM_A
REL_CLIP = 128
N_HEADS_B = 4
DK_HEAD_B = 128
DV_HEAD_B = 256
DK_B = N_HEADS_B * DK_HEAD_B
DV_B = N_HEADS_B * DV_HEAD_B
GATE_RANK = 16
GATE_TEMP = 16.0
D_FF = ((8 * D_MODEL // 3 + 255) // 256) * 256
EPS = 1e-6
SPLIT_WIDTHS = (D_A, D_A, D_A, DK_B, DK_B, DV_B, DV_B, GATE_RANK, D_MODEL, D_MODEL)
D_IN = sum(SPLIT_WIDTHS)

kernel_name = 'streaming_hybrid_bandattn_gla_step'


def _rmsnorm(x, g):
    xf = x.astype(jnp.float32)
    xf = xf * lax.rsqrt(jnp.mean(xf * xf, axis=-1, keepdims=True) + EPS)
    return (xf * g.astype(jnp.float32)).astype(x.dtype)


def _rel_bias(table, dist):
    return table[:, jnp.clip(dist, -REL_CLIP, REL_CLIP) + REL_CLIP]


def _band_attention_prompt(q, k, v, rel_table):
    B, L, H, dh = q.shape
    nc = L // CHUNK
    qc = q.reshape(B, nc, CHUNK, H, dh)
    pad = jnp.zeros((B, BAND_CHUNKS, CHUNK, H, dh), k.dtype)
    kc = jnp.concatenate([pad, k.reshape(B, nc, CHUNK, H, dh)], axis=1)
    vc = jnp.concatenate([pad, v.reshape(B, nc, CHUNK, H, dh)], axis=1)
    k_band = jnp.concatenate([kc[:, o:o + nc] for o in range(BAND_CHUNKS + 1)], axis=2)
    v_band = jnp.concatenate([vc[:, o:o + nc] for o in range(BAND_CHUNKS + 1)], axis=2)
    s = jnp.einsum('bcqhd,bckhd->bchqk', qc, k_band).astype(jnp.float32) * (HEAD_DIM_A ** -0.5)
    qi = jnp.arange(CHUNK)[:, None]
    kk = jnp.arange(BAND)[None, :]
    bias = _rel_bias(rel_table, qi + BAND_PAST - kk).astype(jnp.float32)
    valid = kk[None] >= (BAND_PAST - jnp.arange(nc) * CHUNK)[:, None, None]
    s = jnp.where(valid[None, :, None], s + bias[None, None], -jnp.inf)
    p = jax.nn.softmax(s, axis=-1).astype(v.dtype)
    o = jnp.einsum('bchqk,bckhd->bcqhd', p, v_band)
    return o.reshape(B, L, H * dh)


def _band_attention_sample(q, k_new, v_new, k_cache, v_cache, rel_table):
    Bd, S, H, dh = q.shape
    W = k_cache.shape[1]
    k_all = jnp.concatenate([k_cache, k_new], axis=1)
    v_all = jnp.concatenate([v_cache, v_new], axis=1)
    s = jnp.einsum('bqhd,bkhd->bhqk', q, k_all).astype(jnp.float32) * (HEAD_DIM_A ** -0.5)
    dist = jnp.arange(S)[:, None] + W - jnp.arange(W + S)[None, :]
    s = s + _rel_bias(rel_table, dist).astype(jnp.float32)[None]
    p = jax.nn.softmax(s, axis=-1).astype(v_all.dtype)
    o = jnp.einsum('bhqk,bkhd->bqhd', p, v_all)
    return o.reshape(Bd, S, H * dh)


def _gla(q, k, v, log_a, state0, chunk):
    B, L, H, dk = q.shape
    dv = v.shape[-1]
    nc = L // chunk

    def to_chunks(t):
        return jnp.moveaxis(t.astype(jnp.float32).reshape(B, nc, chunk, H, t.shape[-1]), 1, 0)

    causal = jnp.tril(jnp.ones((chunk, chunk), bool))[None, :, :, None, None]

    def step(S, inp):
        qc, kc, vc, lac = inp
        b = jnp.cumsum(lac, axis=1)
        decay = jnp.exp(jnp.where(causal, b[:, :, None] - b[:, None, :], -jnp.inf))
        attn = jnp.einsum('bthd,bshd,btshd->bhts', qc, kc, decay)
        o = (jnp.einsum('bhts,bshv->bthv', attn, vc)
             + jnp.einsum('bthd,bhdv->bthv', qc * jnp.exp(b), S))
        b_last = b[:, -1]
        S = (jnp.exp(b_last)[..., None] * S
             + jnp.einsum('bshd,bshv->bhdv', kc * jnp.exp(b_last[:, None] - b), vc))
        return S, o

    S, o = lax.scan(step, state0.astype(jnp.float32),
                    (to_chunks(q), to_chunks(k), to_chunks(v), to_chunks(log_a)))
    o = jnp.moveaxis(o, 0, 1).reshape(B, L, H, dv)
    return o, S


def _layer(x, gla_state0, cache_k, cache_v, norm_mix_pre, norm_mix_post, norm_ffn_pre, norm_ffn_post,
           w_in, w_decay_up, b_decay, rel_bias, gla_norm, w_proj_a, w_proj_b, w_out,
           w_ffn_gate, w_ffn_up, w_ffn_down):
    B, L, _ = x.shape
    h = _rmsnorm(x, norm_mix_pre)
    proj = h @ w_in
    split_points = np.cumsum(SPLIT_WIDTHS)[:-1].tolist()
    qa, ka, va, qb, kb, vb, rb, dlr, ga, gb = jnp.split(proj, split_points, axis=-1)
    qa = qa.reshape(B, L, N_HEADS_A, HEAD_DIM_A)
    ka = ka.reshape(B, L, N_HEADS_A, HEAD_DIM_A)
    va = va.reshape(B, L, N_HEADS_A, HEAD_DIM_A)
    if cache_k is None:
        oa = _band_attention_prompt(qa, ka, va, rel_bias)
        keep = min(BAND_PAST, L)
        k_rows, v_rows = ka[:, L - keep:], va[:, L - keep:]
    else:
        oa = _band_attention_sample(qa, ka, va, cache_k, cache_v, rel_bias)
        k_rows, v_rows = ka, va
    log_a = jax.nn.log_sigmoid((dlr @ w_decay_up + b_decay).astype(jnp.float32)) / GATE_TEMP
    chunk = CHUNK if L % CHUNK == 0 else L
    ob, S = _gla((qb * (DK_HEAD_B ** -0.5)).reshape(B, L, N_HEADS_B, DK_HEAD_B),
                 kb.reshape(B, L, N_HEADS_B, DK_HEAD_B),
                 vb.reshape(B, L, N_HEADS_B, DV_HEAD_B),
                 log_a.reshape(B, L, N_HEADS_B, DK_HEAD_B),
                 gla_state0, chunk)
    ob = _rmsnorm(ob, gla_norm).astype(x.dtype).reshape(B, L, DV_B) * jax.nn.silu(rb)
    mix = jax.nn.sigmoid(ga) * (oa @ w_proj_a) + jax.nn.sigmoid(gb) * (ob @ w_proj_b)
    x = x + _rmsnorm(mix @ w_out, norm_mix_post)
    h = _rmsnorm(x, norm_ffn_pre)
    f = (jax.nn.silu(h @ w_ffn_gate) * (h @ w_ffn_up)) @ w_ffn_down
    x = x + _rmsnorm(f, norm_ffn_post)
    return x, k_rows, v_rows, S


def setup_inputs(seed: int = 0) -> dict:
    key = jax.random.key(seed)
    ks = jax.random.split(key, 24)
    f32 = jnp.float32
    past_win = min(BAND_PAST, PAST_LEN)

    def nrm(k, shape, scale):
        return scale * jax.random.normal(k, shape, f32)

    return {
        'x_prompt': nrm(ks[0], (BATCH, SEQ, D_MODEL), 1.0),
        'x_sample': nrm(ks[1], (DEC_BATCH, DEC_SEQ, D_MODEL), 1.0),
        'cache_attn_k': nrm(ks[2], (DEPTH, DEC_BATCH, past_win, N_HEADS_A, HEAD_DIM_A), 1.0),
        'cache_attn_v': nrm(ks[3], (DEPTH, DEC_BATCH, past_win, N_HEADS_A, HEAD_DIM_A), 1.0),
        'state_gla': nrm(ks[4], (DEPTH, DEC_BATCH, N_HEADS_B, DK_HEAD_B, DV_HEAD_B), 0.5),
        'norm_mix_pre': 1.0 + nrm(ks[5], (DEPTH, D_MODEL), 0.05),
        'norm_mix_post': 1.0 + nrm(ks[6], (DEPTH, D_MODEL), 0.05),
        'norm_ffn_pre': 1.0 + nrm(ks[7], (DEPTH, D_MODEL), 0.05),
        'norm_ffn_post': 1.0 + nrm(ks[8], (DEPTH, D_MODEL), 0.05),
        'w_in': nrm(ks[9], (DEPTH, D_MODEL, D_IN), D_MODEL ** -0.5),
        'w_decay_up': nrm(ks[10], (DEPTH, GATE_RANK, DK_B), GATE_RANK ** -0.5),
        'b_decay': nrm(ks[11], (DEPTH, DK_B), 0.1),
        'rel_bias': nrm(ks[12], (DEPTH, N_HEADS_A, 2 * REL_CLIP + 1), 0.1),
        'gla_norm': 1.0 + nrm(ks[13], (DEPTH, DV_HEAD_B), 0.05),
        'w_proj_a': nrm(ks[14], (DEPTH, D_A, D_MODEL), D_A ** -0.5),
        'w_proj_b': nrm(ks[15], (DEPTH, DV_B, D_MODEL), DV_B ** -0.5),
        'w_out': nrm(ks[16], (DEPTH, D_MODEL, D_MODEL), D_MODEL ** -0.5),
        'w_ffn_gate': nrm(ks[17], (DEPTH, D_MODEL, D_FF), D_MODEL ** -0.5),
        'w_ffn_up': nrm(ks[18], (DEPTH, D_MODEL, D_FF), D_MODEL ** -0.5),
        'w_ffn_down': nrm(ks[19], (DEPTH, D_FF, D_MODEL), D_FF ** -0.5),
    }


def reference(x_prompt, x_sample, cache_attn_k, cache_attn_v, state_gla,
              norm_mix_pre, norm_mix_post, norm_ffn_pre, norm_ffn_post,
              w_in, w_decay_up, b_decay, rel_bias, gla_norm, w_proj_a, w_proj_b, w_out,
              w_ffn_gate, w_ffn_up, w_ffn_down):
    yp, ys = x_prompt, x_sample
    kp_l, vp_l, sp_l, ks_l, vs_l, ss_l = [], [], [], [], [], []
    for l in range(DEPTH):
        w = (norm_mix_pre[l], norm_mix_post[l], norm_ffn_pre[l], norm_ffn_post[l],
             w_in[l], w_decay_up[l], b_decay[l], rel_bias[l], gla_norm[l],
             w_proj_a[l], w_proj_b[l], w_out[l], w_ffn_gate[l], w_ffn_up[l], w_ffn_down[l])
        s0 = jnp.zeros((yp.shape[0], N_HEADS_B, DK_HEAD_B, DV_HEAD_B), jnp.float32)
        yp, kp, vp, sp = _layer(yp, s0, None, None, *w)
        ys, kn, vn, sn = _layer(ys, state_gla[l], cache_attn_k[l], cache_attn_v[l], *w)
        kp_l.append(kp)
        vp_l.append(vp)
        sp_l.append(sp.astype(x_prompt.dtype))
        ks_l.append(kn)
        vs_l.append(vn)
        ss_l.append(sn.astype(state_gla.dtype))
    return (yp, ys, jnp.stack(kp_l), jnp.stack(vp_l), jnp.stack(sp_l),
            jnp.stack(ks_l), jnp.stack(vs_l), jnp.stack(ss_l))
```

```python
import functools

import numpy as np
import jax
import jax.numpy as jnp
from jax import lax
from jax.experimental import pallas as pl
from jax.experimental.pallas import tpu as pltpu

F32 = jnp.float32
BF16 = jnp.bfloat16

D_MODEL = 1024
CHUNK = 64
BAND_CHUNKS = 8
BAND_PAST = BAND_CHUNKS * CHUNK
N_HEADS_A = 8
HEAD_DIM_A = 64
D_A = N_HEADS_A * HEAD_DIM_A
REL_CLIP = 128
N_HEADS_B = 4
DK_HEAD_B = 128
DV_HEAD_B = 256
DK_B = N_HEADS_B * DK_HEAD_B
DV_B = N_HEADS_B * DV_HEAD_B
GATE_RANK = 16
GATE_TEMP = 16.0
D_FF = 2816
EPS = 1e-6

LANES = 128
VMEM_LIMIT_BYTES = 56 * 1024 * 1024
NEG_BIG = -1e30

DLR_PAD = LANES
_PIECES = (("qa", D_A), ("ka", D_A), ("va", D_A), ("qb", DK_B), ("kb", DK_B), ("vb", DV_B),
           ("rb", DV_B), ("dlr", DLR_PAD), ("ga", D_MODEL), ("gb", D_MODEL))
_OFF = {}
_o = 0
for _n, _w in _PIECES:
    _OFF[_n] = (_o, _o + _w)
    _o += _w
D_IN_PACKED = _o

TM = 512
ATT_QB = 256
ATT_WIN = ATT_QB + BAND_PAST
GLA_TB = 512
FF_CHUNK = 256
N_FF_CHUNKS = D_FF // FF_CHUNK


def _const_spec(shape):
    nd = len(shape)
    return pl.BlockSpec(shape, lambda i: (0,) * nd, pipeline_mode=pl.Buffered(1))


def _params():
    return pltpu.CompilerParams(dimension_semantics=("arbitrary",), vmem_limit_bytes=VMEM_LIMIT_BYTES)


def _rms(x):
    return x * lax.rsqrt(jnp.mean(x * x, axis=-1, keepdims=True) + EPS)


def _dot(a, b):
    return jnp.dot(a, b, preferred_element_type=F32)


def _dot_nt(a, b):
    return lax.dot_general(a, b, (((1,), (1,)), ((), ())), preferred_element_type=F32)


def _dot_tn(a, b):
    return lax.dot_general(a, b, (((0,), (0,)), ((), ())), preferred_element_type=F32)


def _in_proj_kernel(x_ref, g_ref, w_ref, wup_ref, bdec_ref,
                    qa_ref, ka_ref, va_ref, qb_ref, kb_ref, vb_ref, rb_ref, la_ref, ga_ref, gb_ref,
                    kf_ref, vf_ref, *, kv_rows_every_step):
    h = (_rms(x_ref[...]) * g_ref[...]).astype(BF16)

    def proj(name):
        lo, hi = _OFF[name]
        return _dot(h, w_ref[:, lo:hi])

    qa_ref[...] = (proj("qa") * (HEAD_DIM_A ** -0.5)).astype(BF16)
    ka = proj("ka")
    va = proj("va")
    ka_ref[...] = ka.astype(BF16)
    va_ref[...] = va.astype(BF16)
    if kv_rows_every_step:
        kf_ref[...] = ka
        vf_ref[...] = va
    else:
        @pl.when(pl.program_id(0) == pl.num_programs(0) - 1)
        def _():
            kf_ref[...] = ka
            vf_ref[...] = va
    qb_ref[...] = (proj("qb") * (DK_HEAD_B ** -0.5)).astype(BF16)
    kb_ref[...] = proj("kb").astype(BF16)
    vb_ref[...] = proj("vb").astype(BF16)
    rb_ref[...] = proj("rb").astype(BF16)
    ga_ref[...] = proj("ga").astype(BF16)
    gb_ref[...] = proj("gb").astype(BF16)
    z = _dot(proj("dlr").astype(BF16), wup_ref[...]) + bdec_ref[...]
    la_ref[...] = (jnp.minimum(z, 0.0) - jnp.log1p(jnp.exp(-jnp.abs(z)))) * (1.0 / GATE_TEMP)


def _in_proj(x, g, w_packed, wup_pad, bdec, *, kv_rows_every_step):
    t = x.shape[0]
    n = t // TM
    row = lambda w: pl.BlockSpec((TM, w), lambda i: (i, 0))
    if kv_rows_every_step:
        kv_rows, kv_spec = t, row(D_A)
    else:
        kv_rows, kv_spec = TM, pl.BlockSpec((TM, D_A), lambda i: (0, 0))
    widths = (D_A, D_A, D_A, DK_B, DK_B, DV_B, DV_B, DK_B, D_MODEL, D_MODEL)
    dtypes = (BF16,) * 7 + (F32, BF16, BF16)
    out_shape = tuple(jax.ShapeDtypeStruct((t, w), d) for w, d in zip(widths, dtypes))
    out_shape += (jax.ShapeDtypeStruct((kv_rows, D_A), F32),) * 2
    out_specs = tuple(row(w) for w in widths) + (kv_spec, kv_spec)
    return pl.pallas_call(
        functools.partial(_in_proj_kernel, kv_rows_every_step=kv_rows_every_step),
        grid=(n,),
        in_specs=[row(D_MODEL), _const_spec((1, D_MODEL)), _const_spec((D_MODEL, D_IN_PACKED)),
                  _const_spec((DLR_PAD, DK_B)), _const_spec((1, DK_B))],
        out_specs=out_specs,
        out_shape=out_shape,
        compiler_params=_params(),
        name="in_proj",
    )(x, g, w_packed, wup_pad, bdec)


def _softmax_pv(s, v):
    m = jnp.max(s, axis=-1, keepdims=True)
    e = jnp.exp(s - m)
    l = jnp.sum(e, axis=-1, keepdims=True)
    return _dot(e.astype(BF16), v) / l


def _head_pair_attention(q_pair, k_pair, v_pair, bias0, bias1, extra):
    nq = q_pair.shape[0]
    lane = lax.broadcasted_iota(jnp.int32, (1, LANES), 1)
    first = lane < HEAD_DIM_A
    zero = jnp.zeros_like(q_pair)
    qs = jnp.concatenate([jnp.where(first, q_pair, zero), jnp.where(first, zero, q_pair)], axis=0)
    s = _dot_nt(qs, k_pair) + jnp.concatenate([bias0, bias1], axis=0)
    if extra is not None:
        s = s + extra
    pv = _softmax_pv(s, v_pair)
    return jnp.where(first, pv[:nq], pv[nq:])


def _attn_prompt_kernel(q_ref, k0_ref, k1_ref, k2_ref, v0_ref, v1_ref, v2_ref, bias_ref, o_ref):
    i = pl.program_id(0)
    kwin = jnp.concatenate([k0_ref[...], k1_ref[...], k2_ref[...]], axis=0)
    vwin = jnp.concatenate([v0_ref[...], v1_ref[...], v2_ref[...]], axis=0)
    kk = lax.broadcasted_iota(jnp.int32, (1, ATT_WIN), 1)
    before_start = jnp.where(kk >= BAND_PAST - ATT_QB * i, 0.0, NEG_BIG).astype(F32)
    for p in range(N_HEADS_A // 2):
        sl = slice(p * LANES, (p + 1) * LANES)
        o = _head_pair_attention(q_ref[:, sl], kwin[:, sl], vwin[:, sl],
                                 bias_ref[2 * p], bias_ref[2 * p + 1], before_start)
        o_ref[:, sl] = o.astype(BF16)


def _attn_prompt(qa, ka, va, bias_win):
    t = qa.shape[0]
    n = t // ATT_QB
    blk = lambda back: pl.BlockSpec((ATT_QB, D_A), lambda i: (jnp.maximum(i - back, 0), 0))
    return pl.pallas_call(
        _attn_prompt_kernel,
        grid=(n,),
        in_specs=[blk(0), blk(2), blk(1), blk(0), blk(2), blk(1), blk(0),
                  _const_spec((N_HEADS_A, ATT_QB, ATT_WIN))],
        out_specs=blk(0),
        out_shape=jax.ShapeDtypeStruct((t, D_A), BF16),
        compiler_params=_params(),
        name="attn_prompt",
    )(qa, ka, ka, ka, va, va, va, bias_win)


def _attn_sample_kernel(q_ref, kc_ref, vc_ref, kn_ref, vn_ref, bias_ref, o_ref, *, pad):
    zpad = jnp.zeros((pad, D_A), BF16)
    kall = jnp.concatenate([kc_ref[...].astype(BF16), kn_ref[...].astype(BF16), zpad], axis=0)
    vall = jnp.concatenate([vc_ref[...].astype(BF16), vn_ref[...].astype(BF16), zpad], axis=0)
    for p in range(N_HEADS_A // 2):
        sl = slice(p * LANES, (p + 1) * LANES)
        o = _head_pair_attention(q_ref[:, sl], kall[:, sl], vall[:, sl],
                                 bias_ref[2 * p], bias_ref[2 * p + 1], None)
        o_ref[:, sl] = o.astype(BF16)


def _attn_sample(qa, kn, vn, cache_k, cache_v, bias_s, *, seq, pad):
    t = qa.shape[0]
    nb, w, _ = cache_k.shape
    row = pl.BlockSpec((seq, D_A), lambda i: (i, 0))
    cache = pl.BlockSpec((None, w, D_A), lambda i: (i, 0, 0))
    return pl.pallas_call(
        functools.partial(_attn_sample_kernel, pad=pad),
        grid=(nb,),
        in_specs=[row, cache, cache, row, row, _const_spec(bias_s.shape)],
        out_specs=row,
        out_shape=jax.ShapeDtypeStruct((t, D_A), BF16),
        compiler_params=_params(),
        name="attn_sample",
    )(qa, cache_k, cache_v, kn, vn, bias_s)


def _gla_levels(c):
    hs, out = c // 2, []
    while hs >= 1:
        out.append(hs)
        hs //= 2
    return out


def _gla_consts(c):
    t = np.arange(c)
    rows = [t[:, None] >= t[None, :]]
    masks = []
    for hs in _gla_levels(c):
        blk = t // (2 * hs)
        split = blk * 2 * hs + hs - 1
        rows.append(split[:, None] >= t[None, :])
        second = (t % (2 * hs)) >= hs
        masks.append((blk[:, None] == blk[None, :]) & second[:, None] & ~second[None, :])
    masks.append(np.eye(c, dtype=bool))
    return (np.concatenate(rows, 0).astype(np.float32), np.stack(masks).astype(np.float32))


def _gla_kernel(q_ref, k_ref, v_ref, la_ref, r_ref, gn_ref, sel_ref, mask_ref, s0_ref,
                o_ref, sout_ref, st_ref, *, c, n_chunks, carry_state):
    n_lvl = len(_gla_levels(c))

    def load_state():
        for h in range(N_HEADS_B):
            st_ref[h] = s0_ref[h].T

    if carry_state:
        pl.when(pl.program_id(0) == 0)(load_state)
    else:
        load_state()

    def chunk(ci, carry):
        rows = pl.ds(pl.multiple_of(ci * c, c), c)
        la = la_ref[rows, :]
        hi = la.astype(BF16)
        r1 = la - hi.astype(F32)
        mid = r1.astype(BF16)
        lo = (r1 - mid.astype(F32)).astype(BF16)
        sums = _dot(sel_ref[...], jnp.concatenate([hi, mid, lo], axis=1))
        sums = sums[:, :DK_B] + sums[:, DK_B:2 * DK_B] + sums[:, 2 * DK_B:]
        for h in range(N_HEADS_B):
            ks = slice(h * DK_HEAD_B, (h + 1) * DK_HEAD_B)
            vs = slice(h * DV_HEAD_B, (h + 1) * DV_HEAD_B)
            b = sums[:c, ks]
            qh = q_ref[rows, ks].astype(F32)
            kh = k_ref[rows, ks].astype(F32)
            vh = v_ref[rows, vs]
            a = mask_ref[n_lvl] * _dot_nt(qh.astype(BF16), kh.astype(BF16))
            for l in range(n_lvl):
                ref_b = sums[(l + 1) * c:(l + 2) * c, ks]
                ql = (qh * jnp.exp(jnp.minimum(b - ref_b, 0.0))).astype(BF16)
                kl = (kh * jnp.exp(jnp.minimum(ref_b - b, 0.0))).astype(BF16)
                a = a + mask_ref[l] * _dot_nt(ql, kl)
            o = _dot(a.astype(BF16), vh)
            st = st_ref[h]
            o = o + _dot_nt((qh * jnp.exp(b)).astype(BF16), st.astype(BF16))
            b_last = b[c - 1:c, :]
            kd = (kh * jnp.exp(b_last - b)).astype(BF16)
            st_ref[h] = st * jnp.exp(b_last) + _dot_tn(vh, kd)
            rg = r_ref[rows, vs].astype(F32)
            o = _rms(o) * gn_ref[...] * (rg * jax.nn.sigmoid(rg))
            o_ref[rows, vs] = o.astype(BF16)
        return carry

    lax.fori_loop(0, n_chunks, chunk, 0)

    def store_state():
        for h in range(N_HEADS_B):
            sout_ref[h] = st_ref[h].T

    if carry_state:
        pl.when(pl.program_id(0) == pl.num_programs(0) - 1)(store_state)
    else:
        store_state()


def _gla(qb, kb, vb, la, rb, gnorm, s0, *, c, tb, carry_state):
    t = qb.shape[0]
    n = t // tb
    sel, masks = _gla_consts(c)
    sel = jnp.asarray(sel, BF16)
    masks = jnp.asarray(masks, F32)
    row = lambda w: pl.BlockSpec((tb, w), lambda i: (i, 0))
    st_shape = (N_HEADS_B, DK_HEAD_B, DV_HEAD_B)
    if carry_state:
        s_spec = pl.BlockSpec(st_shape, lambda i: (0, 0, 0))
        s_out_shape = st_shape
    else:
        s_spec = pl.BlockSpec((None,) + st_shape, lambda i: (i, 0, 0, 0))
        s_out_shape = (n,) + st_shape
    return pl.pallas_call(
        functools.partial(_gla_kernel, c=c, n_chunks=tb // c, carry_state=carry_state),
        grid=(n,),
        in_specs=[row(DK_B), row(DK_B), row(DV_B), row(DK_B), row(DV_B), _const_spec((1, DV_HEAD_B)),
                  _const_spec(sel.shape), _const_spec(masks.shape), s_spec],
        out_specs=(row(DV_B), s_spec),
        out_shape=(jax.ShapeDtypeStruct((t, DV_B), BF16), jax.ShapeDtypeStruct(s_out_shape, F32)),
        scratch_shapes=[pltpu.VMEM((N_HEADS_B, DV_HEAD_B, DK_HEAD_B), F32)],
        compiler_params=_params(),
        name="gla",
    )(qb, kb, vb, la, rb, gnorm, sel, masks, s0)


def _merge_ffn_kernel(x_ref, oa_ref, ob_ref, ga_ref, gb_ref, gpost_ref, gpre_ref, gfpost_ref,
                      wpa_ref, wpb_ref, wout_ref, wg_ref, wu_ref, wd_ref, y_ref, x1_ref, h_ref, acc_ref):
    pa = _dot(oa_ref[...], wpa_ref[...])
    pb = _dot(ob_ref[...], wpb_ref[...])
    mix = jax.nn.sigmoid(ga_ref[...].astype(F32)) * pa + jax.nn.sigmoid(gb_ref[...].astype(F32)) * pb
    x1 = x_ref[...] + _rms(_dot(mix.astype(BF16), wout_ref[...])) * gpost_ref[...]
    x1_ref[...] = x1
    h_ref[...] = (_rms(x1) * gpre_ref[...]).astype(BF16)
    acc_ref[...] = jnp.zeros_like(acc_ref)

    def ff(j, carry):
        h = h_ref[...]
        g = _dot(h, wg_ref[j])
        u = _dot(h, wu_ref[j])
        acc_ref[...] += _dot((g * jax.nn.sigmoid(g) * u).astype(BF16), wd_ref[j])
        return carry

    lax.fori_loop(0, N_FF_CHUNKS, ff, 0)
    y_ref[...] = x1_ref[...] + _rms(acc_ref[...]) * gfpost_ref[...]


def _merge_ffn(x, oa, ob, ga, gb, gpost, gpre, gfpost, wpa, wpb, wout, wg, wu, wd):
    t = x.shape[0]
    n = t // TM
    row = lambda w: pl.BlockSpec((TM, w), lambda i: (i, 0))
    vec = _const_spec((1, D_MODEL))
    return pl.pallas_call(
        _merge_ffn_kernel,
        grid=(n,),
        in_specs=[row(D_MODEL), row(D_A), row(DV_B), row(D_MODEL), row(D_MODEL), vec, vec, vec,
                  _const_spec(wpa.shape), _const_spec(wpb.shape), _const_spec(wout.shape),
                  _const_spec(wg.shape), _const_spec(wu.shape), _const_spec(wd.shape)],
        out_specs=row(D_MODEL),
        out_shape=jax.ShapeDtypeStruct((t, D_MODEL), F32),
        scratch_shapes=[pltpu.VMEM((TM, D_MODEL), F32), pltpu.VMEM((TM, D_MODEL), BF16),
                        pltpu.VMEM((TM, D_MODEL), F32)],
        compiler_params=_params(),
        name="merge_ffn",
    )(x, oa, ob, ga, gb, gpost, gpre, gfpost, wpa, wpb, wout, wg, wu, wd)


def _rel_bias_window(table, n_q, n_k, past, band):
    t = np.arange(n_q)[:, None]
    k = np.arange(n_k)[None, :]
    idx = np.clip(t + past - k, -REL_CLIP, REL_CLIP) + REL_CLIP
    bias = table[:, idx].astype(F32)
    if band:
        jq, jk = t // CHUNK, k // CHUNK
        bias = jnp.where(jnp.asarray((jk >= jq) & (jk <= jq + BAND_CHUNKS)), bias, NEG_BIG)
    return bias


def kernel(x_prompt, x_sample, cache_attn_k, cache_attn_v, state_gla, norm_mix_pre, norm_mix_post, norm_ffn_pre, norm_ffn_post, w_in, w_decay_up, b_decay, rel_bias, gla_norm, w_proj_a, w_proj_b, w_out, w_ffn_gate, w_ffn_up, w_ffn_down):
    depth = w_in.shape[0]
    assert depth == 1, "single-layer step"
    batch, seq, _ = x_prompt.shape
    dec_batch, dec_seq, _ = x_sample.shape
    assert batch == 1 and seq % TM == 0 and (dec_batch * dec_seq) % TM == 0
    past = cache_attn_k.shape[2]

    lo_dlr, _ = _OFF["dlr"]
    split = 3 * D_A + 2 * DK_B + 2 * DV_B
    w = w_in[0]
    w_packed = jnp.concatenate(
        [w[:, :split], w[:, split:split + GATE_RANK], jnp.zeros((D_MODEL, DLR_PAD - GATE_RANK), F32),
         w[:, split + GATE_RANK:]], axis=1).astype(BF16)
    assert lo_dlr == split
    wup_pad = jnp.concatenate([w_decay_up[0], jnp.zeros((DLR_PAD - GATE_RANK, DK_B), F32)], axis=0).astype(BF16)
    bdec = b_decay[0][None, :]
    vec = lambda a: a[0][None, :]
    wpa, wpb, wout = w_proj_a[0].astype(BF16), w_proj_b[0].astype(BF16), w_out[0].astype(BF16)
    wg = w_ffn_gate[0].reshape(D_MODEL, N_FF_CHUNKS, FF_CHUNK).transpose(1, 0, 2).astype(BF16)
    wu = w_ffn_up[0].reshape(D_MODEL, N_FF_CHUNKS, FF_CHUNK).transpose(1, 0, 2).astype(BF16)
    wd = w_ffn_down[0].reshape(N_FF_CHUNKS, FF_CHUNK, D_MODEL).astype(BF16)
    gnorm = gla_norm[0][None, :]

    def layer_tail(x, oa, ob, ga, gb):
        return _merge_ffn(x, oa, ob, ga, gb, vec(norm_mix_post), vec(norm_ffn_pre), vec(norm_ffn_post),
                          wpa, wpb, wout, wg, wu, wd)

    xp = x_prompt[0]
    qa, ka, va, qb, kb, vb, rb, la, ga, gb, kf, vf = _in_proj(
        xp, vec(norm_mix_pre), w_packed, wup_pad, bdec, kv_rows_every_step=False)
    bias_win = _rel_bias_window(rel_bias[0], ATT_QB, ATT_WIN, BAND_PAST, band=True)
    oa = _attn_prompt(qa, ka, va, bias_win)
    s0 = jnp.zeros((N_HEADS_B, DK_HEAD_B, DV_HEAD_B), F32)
    ob, sp = _gla(qb, kb, vb, la, rb, gnorm, s0, c=CHUNK, tb=GLA_TB, carry_state=True)
    yp = layer_tail(xp, oa, ob, ga, gb)
    keep = min(BAND_PAST, seq)
    assert keep == TM

    xs = x_sample.reshape(dec_batch * dec_seq, D_MODEL)
    qa, ka, va, qb, kb, vb, rb, la, ga, gb, kfs, vfs = _in_proj(
        xs, vec(norm_mix_pre), w_packed, wup_pad, bdec, kv_rows_every_step=True)
    n_keys = past + dec_seq
    pad = (-n_keys) % LANES
    bias_s = _rel_bias_window(rel_bias[0], dec_seq, n_keys, past, band=False)
    bias_s = jnp.concatenate([bias_s, jnp.full((N_HEADS_A, dec_seq, pad), NEG_BIG, F32)], axis=-1)
    oa = _attn_sample(qa, kfs, vfs, cache_attn_k[0].reshape(dec_batch, past, D_A),
                      cache_attn_v[0].reshape(dec_batch, past, D_A), bias_s, seq=dec_seq, pad=pad)
    gla_chunk = CHUNK if dec_seq % CHUNK == 0 else dec_seq
    ob, ss = _gla(qb, kb, vb, la, rb, gnorm, state_gla[0], c=gla_chunk, tb=dec_seq, carry_state=False)
    ys = layer_tail(xs, oa, ob, ga, gb)

    hd = (N_HEADS_A, HEAD_DIM_A)
    return (yp[None], ys.reshape(dec_batch, dec_seq, D_MODEL),
            kf.reshape((1, 1, keep) + hd), vf.reshape((1, 1, keep) + hd), sp[None, None],
            kfs.reshape((1, dec_batch, dec_seq) + hd), vfs.reshape((1, dec_batch, dec_seq) + hd), ss[None])
```

```python
import functools

import numpy as np
import jax
import jax.numpy as jnp
from jax import lax
from jax.experimental import pallas as pl
from jax.experimental.pallas import tpu as pltpu

F32 = jnp.float32
BF16 = jnp.bfloat16

D_MODEL = 1024
CHUNK = 64
BAND_CHUNKS = 8
BAND_PAST = BAND_CHUNKS * CHUNK
N_HEADS_A = 8
HEAD_DIM_A = 64
D_A = N_HEADS_A * HEAD_DIM_A
REL_CLIP = 128
N_HEADS_B = 4
DK_HEAD_B = 128
DV_HEAD_B = 256
DK_B = N_HEADS_B * DK_HEAD_B
DV_B = N_HEADS_B * DV_HEAD_B
GATE_RANK = 16
GATE_TEMP = 16.0
D_FF = 2816
EPS = 1e-6

LANES = 128
VMEM_LIMIT_BYTES = 56 * 1024 * 1024
NEG_BIG = -1e30

DLR_PAD = LANES
_PIECES = (("qa", D_A), ("ka", D_A), ("va", D_A), ("qb", DK_B), ("kb", DK_B), ("vb", DV_B),
           ("rb", DV_B), ("dlr", DLR_PAD), ("ga", D_MODEL), ("gb", D_MODEL))
_OFF = {}
_o = 0
for _n, _w in _PIECES:
    _OFF[_n] = (_o, _o + _w)
    _o += _w
D_IN_PACKED = _o

TM = 512
ATT_QB = 256
ATT_WIN = ATT_QB + BAND_PAST
BIAS_ORIGIN = ATT_QB
BIAS_F_LEN = 1024
GLA_TB = 512
FF_CHUNK = 256
N_FF_CHUNKS = D_FF // FF_CHUNK


def _const_spec(shape):
    nd = len(shape)
    return pl.BlockSpec(shape, lambda i: (0,) * nd, pipeline_mode=pl.Buffered(1))


def _params():
    return pltpu.CompilerParams(dimension_semantics=("arbitrary",), vmem_limit_bytes=VMEM_LIMIT_BYTES)


def _rms(x):
    return x * lax.rsqrt(jnp.mean(x * x, axis=-1, keepdims=True) + EPS)


def _dot(a, b):
    return jnp.dot(a, b, preferred_element_type=F32)


def _dot_nt(a, b):
    return lax.dot_general(a, b, (((1,), (1,)), ((), ())), preferred_element_type=F32)


def _dot_tn(a, b):
    return lax.dot_general(a, b, (((0,), (0,)), ((), ())), preferred_element_type=F32)


def _in_proj_kernel(x_ref, g_ref, w_ref, wup_ref, bdec_ref,
                    qa_ref, ka_ref, va_ref, qb_ref, kb_ref, vb_ref, rb_ref, la_ref, ga_ref, gb_ref,
                    kf_ref, vf_ref, *, kv_rows_every_step):
    h = (_rms(x_ref[...]) * g_ref[...]).astype(BF16)

    def proj(name):
        lo, hi = _OFF[name]
        return _dot(h, w_ref[:, lo:hi])

    qa_ref[...] = (proj("qa") * (HEAD_DIM_A ** -0.5)).astype(BF16)
    ka = proj("ka")
    va = proj("va")
    ka_ref[...] = ka.astype(BF16)
    va_ref[...] = va.astype(BF16)
    if kv_rows_every_step:
        kf_ref[...] = ka
        vf_ref[...] = va
    else:
        @pl.when(pl.program_id(0) == pl.num_programs(0) - 1)
        def _():
            kf_ref[...] = ka
            vf_ref[...] = va
    qb_ref[...] = (proj("qb") * (DK_HEAD_B ** -0.5)).astype(BF16)
    kb_ref[...] = proj("kb").astype(BF16)
    vb_ref[...] = proj("vb").astype(BF16)
    rb_ref[...] = proj("rb").astype(BF16)
    ga_ref[...] = proj("ga").astype(BF16)
    gb_ref[...] = proj("gb").astype(BF16)
    z = _dot(proj("dlr").astype(BF16), wup_ref[...]) + bdec_ref[...]
    la_ref[...] = (jnp.minimum(z, 0.0) - jnp.log1p(jnp.exp(-jnp.abs(z)))) * (1.0 / GATE_TEMP)


def _in_proj(x, g, w_packed, wup_pad, bdec, *, kv_rows_every_step):
    t = x.shape[0]
    n = t // TM
    row = lambda w: pl.BlockSpec((TM, w), lambda i: (i, 0))
    if kv_rows_every_step:
        kv_rows, kv_spec = t, row(D_A)
    else:
        kv_rows, kv_spec = TM, pl.BlockSpec((TM, D_A), lambda i: (0, 0))
    widths = (D_A, D_A, D_A, DK_B, DK_B, DV_B, DV_B, DK_B, D_MODEL, D_MODEL)
    dtypes = (BF16,) * 7 + (F32, BF16, BF16)
    out_shape = tuple(jax.ShapeDtypeStruct((t, w), d) for w, d in zip(widths, dtypes))
    out_shape += (jax.ShapeDtypeStruct((kv_rows, D_A), F32),) * 2
    out_specs = tuple(row(w) for w in widths) + (kv_spec, kv_spec)
    return pl.pallas_call(
        functools.partial(_in_proj_kernel, kv_rows_every_step=kv_rows_every_step),
        grid=(n,),
        in_specs=[row(D_MODEL), _const_spec((1, D_MODEL)), _const_spec((D_MODEL, D_IN_PACKED)),
                  _const_spec((DLR_PAD, DK_B)), _const_spec((1, DK_B))],
        out_specs=out_specs,
        out_shape=out_shape,
        compiler_params=_params(),
        name="in_proj",
    )(x, g, w_packed, wup_pad, bdec)


def _softmax_pv(s, v):
    m = jnp.max(s, axis=-1, keepdims=True)
    e = jnp.exp(s - m)
    l = jnp.sum(e, axis=-1, keepdims=True)
    return _dot(e.astype(BF16), v) / l


def _head_pair_attention(q_pair, k_pair, v_pair, bias0, bias1, extra):
    nq = q_pair.shape[0]
    lane = lax.broadcasted_iota(jnp.int32, (1, LANES), 1)
    first = lane < HEAD_DIM_A
    zero = jnp.zeros_like(q_pair)
    qs = jnp.concatenate([jnp.where(first, q_pair, zero), jnp.where(first, zero, q_pair)], axis=0)
    s = _dot_nt(qs, k_pair) + jnp.concatenate([bias0, bias1], axis=0)
    if extra is not None:
        s = s + extra
    pv = _softmax_pv(s, v_pair)
    return jnp.where(first, pv[:nq], pv[nq:])


def _expand_rel_bias(f_ref, bias_ref, n_q, n_k, n_valid, band):
    t = lax.broadcasted_iota(jnp.int32, (n_q, n_k), 0)
    k = lax.broadcasted_iota(jnp.int32, (n_q, n_k), 1)
    keep = k < n_valid
    if band:
        jq, jk = t // CHUNK, k // CHUNK
        keep = keep & (jk >= jq) & (jk <= jq + BAND_CHUNKS)
    for h in range(N_HEADS_A):
        rows = jnp.broadcast_to(f_ref[h:h + 1, :], (n_q, BIAS_F_LEN))
        rolled = pltpu.roll(rows, BIAS_F_LEN - BIAS_ORIGIN, 1, stride=1, stride_axis=0)
        bias_ref[h] = jnp.where(keep, rolled[:, :n_k], NEG_BIG)


def _attn_prompt_kernel(q_ref, k0_ref, k1_ref, k2_ref, v0_ref, v1_ref, v2_ref, f_ref, o_ref, bias_ref):
    i = pl.program_id(0)

    @pl.when(i == 0)
    def _():
        _expand_rel_bias(f_ref, bias_ref, ATT_QB, ATT_WIN, ATT_WIN, band=True)

    kwin = jnp.concatenate([k0_ref[...], k1_ref[...], k2_ref[...]], axis=0)
    vwin = jnp.concatenate([v0_ref[...], v1_ref[...], v2_ref[...]], axis=0)
    kk = lax.broadcasted_iota(jnp.int32, (1, ATT_WIN), 1)
    before_start = jnp.where(kk >= BAND_PAST - ATT_QB * i, 0.0, NEG_BIG).astype(F32)
    for p in range(N_HEADS_A // 2):
        sl = slice(p * LANES, (p + 1) * LANES)
        o = _head_pair_attention(q_ref[:, sl], kwin[:, sl], vwin[:, sl],
                                 bias_ref[2 * p], bias_ref[2 * p + 1], before_start)
        o_ref[:, sl] = o.astype(BF16)


def _attn_prompt(qa, ka, va, f_tab):
    t = qa.shape[0]
    n = t // ATT_QB
    blk = lambda back: pl.BlockSpec((ATT_QB, D_A), lambda i: (jnp.maximum(i - back, 0), 0))
    return pl.pallas_call(
        _attn_prompt_kernel,
        grid=(n,),
        in_specs=[blk(0), blk(2), blk(1), blk(0), blk(2), blk(1), blk(0),
                  _const_spec((N_HEADS_A, BIAS_F_LEN))],
        out_specs=blk(0),
        out_shape=jax.ShapeDtypeStruct((t, D_A), BF16),
        scratch_shapes=[pltpu.VMEM((N_HEADS_A, ATT_QB, ATT_WIN), F32)],
        compiler_params=_params(),
        name="attn_prompt",
    )(qa, ka, ka, ka, va, va, va, f_tab)


def _attn_sample_kernel(q_ref, kc_ref, vc_ref, kn_ref, vn_ref, f_ref, o_ref, bias_ref, *, pad):
    n_q, n_k = bias_ref.shape[1:]

    @pl.when(pl.program_id(0) == 0)
    def _():
        _expand_rel_bias(f_ref, bias_ref, n_q, n_k, n_k - pad, band=False)

    zpad = jnp.zeros((pad, D_A), BF16)
    kall = jnp.concatenate([kc_ref[...].astype(BF16), kn_ref[...].astype(BF16), zpad], axis=0)
    vall = jnp.concatenate([vc_ref[...].astype(BF16), vn_ref[...].astype(BF16), zpad], axis=0)
    for p in range(N_HEADS_A // 2):
        sl = slice(p * LANES, (p + 1) * LANES)
        o = _head_pair_attention(q_ref[:, sl], kall[:, sl], vall[:, sl],
                                 bias_ref[2 * p], bias_ref[2 * p + 1], None)
        o_ref[:, sl] = o.astype(BF16)


def _attn_sample(qa, kn, vn, cache_k, cache_v, f_tab, *, seq, pad):
    t = qa.shape[0]
    nb, w, _ = cache_k.shape
    row = pl.BlockSpec((seq, D_A), lambda i: (i, 0))
    cache = pl.BlockSpec((None, w, D_A), lambda i: (i, 0, 0))
    return pl.pallas_call(
        functools.partial(_attn_sample_kernel, pad=pad),
        grid=(nb,),
        in_specs=[row, cache, cache, row, row, _const_spec((N_HEADS_A, BIAS_F_LEN))],
        out_specs=row,
        out_shape=jax.ShapeDtypeStruct((t, D_A), BF16),
        scratch_shapes=[pltpu.VMEM((N_HEADS_A, seq, w + seq + pad), F32)],
        compiler_params=_params(),
        name="attn_sample",
    )(qa, cache_k, cache_v, kn, vn, f_tab)


def _gla_levels(c):
    hs, out = c // 2, []
    while hs >= 1:
        out.append(hs)
        hs //= 2
    return out


def _gla_consts(c):
    t = np.arange(c)
    rows = [t[:, None] >= t[None, :]]
    masks = []
    for hs in _gla_levels(c):
        blk = t // (2 * hs)
        split = blk * 2 * hs + hs - 1
        rows.append(split[:, None] >= t[None, :])
        second = (t % (2 * hs)) >= hs
        masks.append((blk[:, None] == blk[None, :]) & second[:, None] & ~second[None, :])
    masks.append(np.eye(c, dtype=bool))
    return (np.concatenate(rows, 0).astype(np.float32), np.stack(masks).astype(np.float32))


def _gla_kernel(q_ref, k_ref, v_ref, la_ref, r_ref, gn_ref, sel_ref, mask_ref, s0_ref,
                o_ref, sout_ref, st_ref, *, c, n_chunks, carry_state):
    n_lvl = len(_gla_levels(c))

    def load_state():
        for h in range(N_HEADS_B):
            st_ref[h] = s0_ref[h].T

    if carry_state:
        pl.when(pl.program_id(0) == 0)(load_state)
    else:
        load_state()

    def chunk(ci, carry):
        rows = pl.ds(pl.multiple_of(ci * c, c), c)
        la = la_ref[rows, :]
        hi = la.astype(BF16)
        r1 = la - hi.astype(F32)
        mid = r1.astype(BF16)
        lo = (r1 - mid.astype(F32)).astype(BF16)
        sums = _dot(sel_ref[...], jnp.concatenate([hi, mid, lo], axis=1))
        sums = sums[:, :DK_B] + sums[:, DK_B:2 * DK_B] + sums[:, 2 * DK_B:]
        for h in range(N_HEADS_B):
            ks = slice(h * DK_HEAD_B, (h + 1) * DK_HEAD_B)
            vs = slice(h * DV_HEAD_B, (h + 1) * DV_HEAD_B)
            b = sums[:c, ks]
            qh = q_ref[rows, ks].astype(F32)
            kh = k_ref[rows, ks].astype(F32)
            vh = v_ref[rows, vs]
            a = mask_ref[n_lvl] * _dot_nt(qh.astype(BF16), kh.astype(BF16))
            for l in range(n_lvl):
                ref_b = sums[(l + 1) * c:(l + 2) * c, ks]
                ql = (qh * jnp.exp(jnp.minimum(b - ref_b, 0.0))).astype(BF16)
                kl = (kh * jnp.exp(jnp.minimum(ref_b - b, 0.0))).astype(BF16)
                a = a + mask_ref[l] * _dot_nt(ql, kl)
            o = _dot(a.astype(BF16), vh)
            st = st_ref[h]
            o = o + _dot_nt((qh * jnp.exp(b)).astype(BF16), st.astype(BF16))
            b_last = b[c - 1:c, :]
            kd = (kh * jnp.exp(b_last - b)).astype(BF16)
            st_ref[h] = st * jnp.exp(b_last) + _dot_tn(vh, kd)
            rg = r_ref[rows, vs].astype(F32)
            o = _rms(o) * gn_ref[...] * (rg * jax.nn.sigmoid(rg))
            o_ref[rows, vs] = o.astype(BF16)
        return carry

    lax.fori_loop(0, n_chunks, chunk, 0)

    def store_state():
        for h in range(N_HEADS_B):
            sout_ref[h] = st_ref[h].T

    if carry_state:
        pl.when(pl.program_id(0) == pl.num_programs(0) - 1)(store_state)
    else:
        store_state()


def _gla(qb, kb, vb, la, rb, gnorm, s0, *, c, tb, carry_state):
    t = qb.shape[0]
    n = t // tb
    sel, masks = _gla_consts(c)
    sel = jnp.asarray(sel, BF16)
    masks = jnp.asarray(masks, F32)
    row = lambda w: pl.BlockSpec((tb, w), lambda i: (i, 0))
    st_shape = (N_HEADS_B, DK_HEAD_B, DV_HEAD_B)
    if carry_state:
        s_spec = pl.BlockSpec(st_shape, lambda i: (0, 0, 0))
        s_out_shape = st_shape
    else:
        s_spec = pl.BlockSpec((None,) + st_shape, lambda i: (i, 0, 0, 0))
        s_out_shape = (n,) + st_shape
    return pl.pallas_call(
        functools.partial(_gla_kernel, c=c, n_chunks=tb // c, carry_state=carry_state),
        grid=(n,),
        in_specs=[row(DK_B), row(DK_B), row(DV_B), row(DK_B), row(DV_B), _const_spec((1, DV_HEAD_B)),
                  _const_spec(sel.shape), _const_spec(masks.shape), s_spec],
        out_specs=(row(DV_B), s_spec),
        out_shape=(jax.ShapeDtypeStruct((t, DV_B), BF16), jax.ShapeDtypeStruct(s_out_shape, F32)),
        scratch_shapes=[pltpu.VMEM((N_HEADS_B, DV_HEAD_B, DK_HEAD_B), F32)],
        compiler_params=_params(),
        name="gla",
    )(qb, kb, vb, la, rb, gnorm, sel, masks, s0)


def _merge_ffn_kernel(x_ref, oa_ref, ob_ref, ga_ref, gb_ref, gpost_ref, gpre_ref, gfpost_ref,
                      wpa_ref, wpb_ref, wout_ref, wg_ref, wu_ref, wd_ref, y_ref, x1_ref, h_ref, acc_ref):
    pa = _dot(oa_ref[...], wpa_ref[...])
    pb = _dot(ob_ref[...], wpb_ref[...])
    mix = jax.nn.sigmoid(ga_ref[...].astype(F32)) * pa + jax.nn.sigmoid(gb_ref[...].astype(F32)) * pb
    x1 = x_ref[...] + _rms(_dot(mix.astype(BF16), wout_ref[...])) * gpost_ref[...]
    x1_ref[...] = x1
    h_ref[...] = (_rms(x1) * gpre_ref[...]).astype(BF16)
    acc_ref[...] = jnp.zeros_like(acc_ref)

    for j in range(N_FF_CHUNKS):
        cols = slice(j * FF_CHUNK, (j + 1) * FF_CHUNK)
        h = h_ref[...]
        g = _dot(h, wg_ref[:, cols])
        u = _dot(h, wu_ref[:, cols])
        acc_ref[...] += _dot((g * jax.nn.sigmoid(g) * u).astype(BF16), wd_ref[cols, :])
    y_ref[...] = x1_ref[...] + _rms(acc_ref[...]) * gfpost_ref[...]


def _merge_ffn(x, oa, ob, ga, gb, gpost, gpre, gfpost, wpa, wpb, wout, wg, wu, wd):
    t = x.shape[0]
    n = t // TM
    row = lambda w: pl.BlockSpec((TM, w), lambda i: (i, 0))
    vec = _const_spec((1, D_MODEL))
    return pl.pallas_call(
        _merge_ffn_kernel,
        grid=(n,),
        in_specs=[row(D_MODEL), row(D_A), row(DV_B), row(D_MODEL), row(D_MODEL), vec, vec, vec,
                  _const_spec(wpa.shape), _const_spec(wpb.shape), _const_spec(wout.shape),
                  _const_spec(wg.shape), _const_spec(wu.shape), _const_spec(wd.shape)],
        out_specs=row(D_MODEL),
        out_shape=jax.ShapeDtypeStruct((t, D_MODEL), F32),
        scratch_shapes=[pltpu.VMEM((TM, D_MODEL), F32), pltpu.VMEM((TM, D_MODEL), BF16),
                        pltpu.VMEM((TM, D_MODEL), F32)],
        compiler_params=_params(),
        name="merge_ffn",
    )(x, oa, ob, ga, gb, gpost, gpre, gfpost, wpa, wpb, wout, wg, wu, wd)


def _rel_bias_row(table):
    n_hi = BAND_PAST + BIAS_ORIGIN - REL_CLIP
    n_lo = BIAS_F_LEN - n_hi - (2 * REL_CLIP + 1)
    h = table.shape[0]
    return jnp.concatenate([jnp.broadcast_to(table[:, -1:], (h, n_hi)), table[:, ::-1],
                            jnp.broadcast_to(table[:, :1], (h, n_lo))], axis=1)


def kernel(x_prompt, x_sample, cache_attn_k, cache_attn_v, state_gla, norm_mix_pre, norm_mix_post, norm_ffn_pre, norm_ffn_post, w_in, w_decay_up, b_decay, rel_bias, gla_norm, w_proj_a, w_proj_b, w_out, w_ffn_gate, w_ffn_up, w_ffn_down):
    depth = w_in.shape[0]
    assert depth == 1, "single-layer step"
    batch, seq, _ = x_prompt.shape
    dec_batch, dec_seq, _ = x_sample.shape
    assert batch == 1 and seq % TM == 0 and (dec_batch * dec_seq) % TM == 0
    past = cache_attn_k.shape[2]

    lo_dlr, _ = _OFF["dlr"]
    split = 3 * D_A + 2 * DK_B + 2 * DV_B
    w = w_in[0]
    w_packed = jnp.concatenate(
        [w[:, :split], w[:, split:split + GATE_RANK], jnp.zeros((D_MODEL, DLR_PAD - GATE_RANK), F32),
         w[:, split + GATE_RANK:]], axis=1).astype(BF16)
    assert lo_dlr == split
    wup_pad = jnp.concatenate([w_decay_up[0], jnp.zeros((DLR_PAD - GATE_RANK, DK_B), F32)], axis=0).astype(BF16)
    bdec = b_decay[0][None, :]
    vec = lambda a: a[0][None, :]
    wpa, wpb, wout = w_proj_a[0].astype(BF16), w_proj_b[0].astype(BF16), w_out[0].astype(BF16)
    wg, wu, wd = w_ffn_gate[0].astype(BF16), w_ffn_up[0].astype(BF16), w_ffn_down[0].astype(BF16)
    gnorm = gla_norm[0][None, :]

    def layer_tail(x, oa, ob, ga, gb):
        return _merge_ffn(x, oa, ob, ga, gb, vec(norm_mix_post), vec(norm_ffn_pre), vec(norm_ffn_post),
                          wpa, wpb, wout, wg, wu, wd)

    xp = x_prompt[0]
    qa, ka, va, qb, kb, vb, rb, la, ga, gb, kf, vf = _in_proj(
        xp, vec(norm_mix_pre), w_packed, wup_pad, bdec, kv_rows_every_step=False)
    f_tab = _rel_bias_row(rel_bias[0])
    oa = _attn_prompt(qa, ka, va, f_tab)
    s0 = jnp.zeros((N_HEADS_B, DK_HEAD_B, DV_HEAD_B), F32)
    ob, sp = _gla(qb, kb, vb, la, rb, gnorm, s0, c=CHUNK, tb=GLA_TB, carry_state=True)
    yp = layer_tail(xp, oa, ob, ga, gb)
    keep = min(BAND_PAST, seq)
    assert keep == TM

    xs = x_sample.reshape(dec_batch * dec_seq, D_MODEL)
    qa, ka, va, qb, kb, vb, rb, la, ga, gb, kfs, vfs = _in_proj(
        xs, vec(norm_mix_pre), w_packed, wup_pad, bdec, kv_rows_every_step=True)
    n_keys = past + dec_seq
    pad = (-n_keys) % LANES
    assert past == BAND_PAST
    oa = _attn_sample(qa, kfs, vfs, cache_attn_k[0].reshape(dec_batch, past, D_A),
                      cache_attn_v[0].reshape(dec_batch, past, D_A), f_tab, seq=dec_seq, pad=pad)
    gla_chunk = CHUNK if dec_seq % CHUNK == 0 else dec_seq
    ob, ss = _gla(qb, kb, vb, la, rb, gnorm, state_gla[0], c=gla_chunk, tb=dec_seq, carry_state=False)
    ys = layer_tail(xs, oa, ob, ga, gb)

    hd = (N_HEADS_A, HEAD_DIM_A)
    return (yp[None], ys.reshape(dec_batch, dec_seq, D_MODEL),
            kf.reshape((1, 1, keep) + hd), vf.reshape((1, 1, keep) + hd), sp[None, None],
            kfs.reshape((1, dec_batch, dec_seq) + hd), vfs.reshape((1, dec_batch, dec_seq) + hd), ss[None])
```

```python
import functools

import numpy as np
import jax
import jax.numpy as jnp
from jax import lax
from jax.experimental import pallas as pl
from jax.experimental.pallas import tpu as pltpu

F32 = jnp.float32
BF16 = jnp.bfloat16

D_MODEL = 1024
CHUNK = 64
BAND_CHUNKS = 8
BAND_PAST = BAND_CHUNKS * CHUNK
N_HEADS_A = 8
HEAD_DIM_A = 64
D_A = N_HEADS_A * HEAD_DIM_A
REL_CLIP = 128
N_HEADS_B = 4
DK_HEAD_B = 128
DV_HEAD_B = 256
DK_B = N_HEADS_B * DK_HEAD_B
DV_B = N_HEADS_B * DV_HEAD_B
GATE_RANK = 16
GATE_TEMP = 16.0
D_FF = 2816
EPS = 1e-6

LANES = 128
SUBLANES = 8
LOG2E = 1.4426950408889634
VMEM_LIMIT_BYTES = 56 * 1024 * 1024
NEG_BIG = -1e30

_PIECES = (("qa", D_A), ("ka", D_A), ("va", D_A), ("qb", DK_B), ("kb", DK_B), ("vb", DV_B),
           ("rb", DV_B), ("dlr", GATE_RANK), ("ga", D_MODEL), ("gb", D_MODEL))
_OFF = {}
_o = 0
for _n, _w in _PIECES:
    _OFF[_n] = (_o, _o + _w)
    _o += _w
D_IN = _o
DLR_PAD = LANES
assert _OFF["dlr"][0] % LANES == 0

TM = 512
ATT_QB = 256
ATT_WIN = ATT_QB + BAND_PAST
BIAS_ORIGIN = ATT_QB
BIAS_F_LEN = 1024
GLA_TB = 512
GLA_UNROLL = 4
FF_CHUNK = 256
REALIGN_ROWS = 128
N_FF_CHUNKS = D_FF // FF_CHUNK


def _const_spec(shape):
    nd = len(shape)
    return pl.BlockSpec(shape, lambda i: (0,) * nd, pipeline_mode=pl.Buffered(1))


def _params():
    return pltpu.CompilerParams(dimension_semantics=("arbitrary",), vmem_limit_bytes=VMEM_LIMIT_BYTES)


def _rms(x):
    return x * lax.rsqrt(jnp.mean(x * x, axis=-1, keepdims=True) + EPS)


def _dot(a, b):
    return jnp.dot(a, b, preferred_element_type=F32)


def _dot_nt(a, b):
    return lax.dot_general(a, b, (((1,), (1,)), ((), ())), preferred_element_type=F32)


def _dot_tn(a, b):
    return lax.dot_general(a, b, (((0,), (0,)), ((), ())), preferred_element_type=F32)


def _in_proj_kernel(x_ref, g_ref, w_ref, wup_ref, bdec_ref,
                    qa_ref, ka_ref, va_ref, qb_ref, kb_ref, vb_ref, rs_ref, la_ref, ga_ref, gb_ref,
                    kf_ref, vf_ref, wgate_ref, *, kv_rows_every_step):
    dlr_lo = _OFF["dlr"][0]

    @pl.when(pl.program_id(0) == 0)
    def _():
        for r in range(0, D_MODEL, REALIGN_ROWS):
            tail = w_ref[r:r + REALIGN_ROWS, dlr_lo:]
            for j, name in enumerate(("ga", "gb")):
                lo, hi = _OFF[name]
                wgate_ref[j, r:r + REALIGN_ROWS, :] = tail[:, lo - dlr_lo:hi - dlr_lo]

    h = (_rms(x_ref[...]) * g_ref[...]).astype(BF16)

    def proj(name):
        lo, hi = _OFF[name]
        return _dot(h, w_ref[:, lo:hi])

    qa_ref[...] = (proj("qa") * (HEAD_DIM_A ** -0.5 * LOG2E)).astype(BF16)
    ka = proj("ka")
    va = proj("va")
    ka_ref[...] = ka.astype(BF16)
    va_ref[...] = va.astype(BF16)
    if kv_rows_every_step:
        kf_ref[...] = ka
        vf_ref[...] = va
    else:
        @pl.when(pl.program_id(0) == pl.num_programs(0) - 1)
        def _():
            kf_ref[...] = ka
            vf_ref[...] = va
    qb_ref[...] = (proj("qb") * (DK_HEAD_B ** -0.5)).astype(BF16)
    kb_ref[...] = proj("kb").astype(BF16)
    vb_ref[...] = proj("vb").astype(BF16)
    r = proj("rb")
    rs_ref[...] = (r * jax.nn.sigmoid(r)).astype(BF16)
    ga_ref[...] = _dot(h, wgate_ref[0]).astype(BF16)
    gb_ref[...] = _dot(h, wgate_ref[1]).astype(BF16)
    dlr = _dot(h, w_ref[:, dlr_lo:dlr_lo + DLR_PAD])
    z = _dot(dlr.astype(BF16), wup_ref[...]) + bdec_ref[...]
    la_ref[...] = (jnp.minimum(z, 0.0) - jnp.log1p(jnp.exp(-jnp.abs(z)))) * (LOG2E / GATE_TEMP)


def _in_proj(x, g, w_bf, wup_pad, bdec, *, kv_rows_every_step):
    t = x.shape[0]
    n = t // TM
    row = lambda w: pl.BlockSpec((TM, w), lambda i: (i, 0))
    if kv_rows_every_step:
        kv_rows, kv_spec = t, row(D_A)
    else:
        kv_rows, kv_spec = TM, pl.BlockSpec((TM, D_A), lambda i: (0, 0))
    widths = (D_A, D_A, D_A, DK_B, DK_B, DV_B, DV_B, DK_B, D_MODEL, D_MODEL)
    dtypes = (BF16,) * 7 + (F32, BF16, BF16)
    out_shape = tuple(jax.ShapeDtypeStruct((t, w), d) for w, d in zip(widths, dtypes))
    out_shape += (jax.ShapeDtypeStruct((kv_rows, D_A), F32),) * 2
    out_specs = tuple(row(w) for w in widths) + (kv_spec, kv_spec)
    return pl.pallas_call(
        functools.partial(_in_proj_kernel, kv_rows_every_step=kv_rows_every_step),
        grid=(n,),
        in_specs=[row(D_MODEL), _const_spec((1, D_MODEL)), _const_spec((D_MODEL, D_IN)),
                  _const_spec((DLR_PAD, DK_B)), _const_spec((1, DK_B))],
        out_specs=out_specs,
        out_shape=out_shape,
        scratch_shapes=[pltpu.VMEM((2, D_MODEL, D_MODEL), BF16)],
        compiler_params=_params(),
        name="in_proj",
    )(x, g, w_bf, wup_pad, bdec)


def _softmax_pv(s, v):
    m = jnp.max(s, axis=-1, keepdims=True)
    e = jnp.exp2(s - m)
    l = jnp.sum(e, axis=-1, keepdims=True)
    return _dot(e.astype(BF16), v) / l


def _head_pair_attention(q_pair, k_pair, v_pair, bias0, bias1):
    nq = q_pair.shape[0]
    lane = lax.broadcasted_iota(jnp.int32, (1, LANES), 1)
    first = lane < HEAD_DIM_A
    zero = jnp.zeros_like(q_pair)
    qs = jnp.concatenate([jnp.where(first, q_pair, zero), jnp.where(first, zero, q_pair)], axis=0)
    s = _dot_nt(qs, k_pair) + jnp.concatenate([bias0, bias1], axis=0)
    pv = _softmax_pv(s, v_pair)
    return jnp.where(first, pv[:nq], pv[nq:])


def _expand_rel_bias(f_ref, bias_ref, n_q, n_k, first_valid, n_valid, band):
    t = lax.broadcasted_iota(jnp.int32, (n_q, n_k), 0)
    k = lax.broadcasted_iota(jnp.int32, (n_q, n_k), 1)
    keep = (k < n_valid) & (k >= first_valid)
    if band:
        jq, jk = t // CHUNK, k // CHUNK
        keep = keep & (jk >= jq) & (jk <= jq + BAND_CHUNKS)
    for h in range(N_HEADS_A):
        rows = jnp.broadcast_to(f_ref[h:h + 1, :], (n_q, BIAS_F_LEN))
        rolled = pltpu.roll(rows, BIAS_F_LEN - BIAS_ORIGIN, 1, stride=1, stride_axis=0)
        bias_ref[h] = jnp.where(keep, rolled[:, :n_k] * LOG2E, NEG_BIG)


def _attn_prompt_kernel(q_ref, k0_ref, k1_ref, k2_ref, v0_ref, v1_ref, v2_ref, f_ref, o_ref, bias_ref):
    i = pl.program_id(0)

    @pl.when(i * ATT_QB <= BAND_PAST)
    def _():
        _expand_rel_bias(f_ref, bias_ref, ATT_QB, ATT_WIN, BAND_PAST - ATT_QB * i, ATT_WIN, band=True)

    kwin = jnp.concatenate([k0_ref[...], k1_ref[...], k2_ref[...]], axis=0)
    vwin = jnp.concatenate([v0_ref[...], v1_ref[...], v2_ref[...]], axis=0)
    for p in range(N_HEADS_A // 2):
        sl = slice(p * LANES, (p + 1) * LANES)
        o = _head_pair_attention(q_ref[:, sl], kwin[:, sl], vwin[:, sl],
                                 bias_ref[2 * p], bias_ref[2 * p + 1])
        o_ref[:, sl] = o.astype(BF16)


def _attn_prompt(qa, ka, va, f_tab):
    t = qa.shape[0]
    n = t // ATT_QB
    blk = lambda back: pl.BlockSpec((ATT_QB, D_A), lambda i: (jnp.maximum(i - back, 0), 0))
    return pl.pallas_call(
        _attn_prompt_kernel,
        grid=(n,),
        in_specs=[blk(0), blk(2), blk(1), blk(0), blk(2), blk(1), blk(0),
                  _const_spec((N_HEADS_A, BIAS_F_LEN))],
        out_specs=blk(0),
        out_shape=jax.ShapeDtypeStruct((t, D_A), BF16),
        scratch_shapes=[pltpu.VMEM((N_HEADS_A, ATT_QB, ATT_WIN), F32)],
        compiler_params=_params(),
        name="attn_prompt",
    )(qa, ka, ka, ka, va, va, va, f_tab)


def _attn_sample_kernel(q_ref, kc_ref, vc_ref, kn_ref, vn_ref, f_ref, o_ref, bias_ref, *, pad):
    n_q, n_k = bias_ref.shape[1:]

    @pl.when(pl.program_id(0) == 0)
    def _():
        _expand_rel_bias(f_ref, bias_ref, n_q, n_k, 0, n_k - pad, band=False)

    zpad = jnp.zeros((pad, D_A), BF16)
    kall = jnp.concatenate([kc_ref[...].astype(BF16), kn_ref[...].astype(BF16), zpad], axis=0)
    vall = jnp.concatenate([vc_ref[...].astype(BF16), vn_ref[...].astype(BF16), zpad], axis=0)
    for p in range(N_HEADS_A // 2):
        sl = slice(p * LANES, (p + 1) * LANES)
        o = _head_pair_attention(q_ref[:, sl], kall[:, sl], vall[:, sl],
                                 bias_ref[2 * p], bias_ref[2 * p + 1])
        o_ref[:, sl] = o.astype(BF16)


def _attn_sample(qa, kn, vn, cache_k, cache_v, f_tab, *, seq, pad):
    t = qa.shape[0]
    nb, w, _ = cache_k.shape
    row = pl.BlockSpec((seq, D_A), lambda i: (i, 0))
    cache = pl.BlockSpec((None, w, D_A), lambda i: (i, 0, 0))
    return pl.pallas_call(
        functools.partial(_attn_sample_kernel, pad=pad),
        grid=(nb,),
        in_specs=[row, cache, cache, row, row, _const_spec((N_HEADS_A, BIAS_F_LEN))],
        out_specs=row,
        out_shape=jax.ShapeDtypeStruct((t, D_A), BF16),
        scratch_shapes=[pltpu.VMEM((N_HEADS_A, seq, w + seq + pad), F32)],
        compiler_params=_params(),
        name="attn_sample",
    )(qa, cache_k, cache_v, kn, vn, f_tab)


def _gla_levels(c):
    out, hs = [], 1
    while hs < c:
        out.append(hs)
        hs *= 2
    return out


def _gla_masks(c):
    t = np.arange(c)
    masks = []
    for hs in _gla_levels(c):
        blk = t // (2 * hs)
        second = (t % (2 * hs)) >= hs
        masks.append((blk[:, None] == blk[None, :]) & second[:, None] & ~second[None, :])
    masks.append(np.eye(c, dtype=bool))
    return np.stack(masks).astype(np.float32)


def _split_row_bcast(x, hs, row):
    c, w = x.shape
    blk = 2 * hs
    if blk >= SUBLANES:
        xr = x.reshape(c // blk, blk, w)
        return jnp.broadcast_to(xr[:, hs - 1:hs, :], (c // blk, blk, w)).reshape(c, w)
    tiles = (c // SUBLANES, SUBLANES, w)
    x3, pos = x.reshape(tiles), (row & (blk - 1)).reshape(tiles)
    if hs == 1:
        out = jnp.where(pos == 1, pltpu.roll(x3, 1, 1), x3)
    else:
        assert hs == 2
        nxt = jnp.where((pos & 1) == 1, x3, pltpu.roll(x3, SUBLANES - 1, 1))
        out = jnp.where(pos >= 2, pltpu.roll(nxt, 2, 1), nxt)
    return out.reshape(c, w)


def _gla_kernel(q_ref, k_ref, v_ref, la_ref, rs_ref, gn_ref, mask_ref, s0_ref,
                o_ref, sout_ref, st_ref, *, c, n_chunks, carry_state, unroll):
    levels = _gla_levels(c)
    n_lvl = len(levels)

    def load_state():
        for h in range(N_HEADS_B):
            st_ref[h] = s0_ref[h].T

    if carry_state:
        pl.when(pl.program_id(0) == 0)(load_state)
    else:
        load_state()

    row = lax.broadcasted_iota(jnp.int32, (c, DK_B), 0)

    def chunk(ci, carry):
        rows = pl.ds(pl.multiple_of(ci * c, c), c)
        qf = q_ref[rows, :]
        kf = k_ref[rows, :]
        p = la_ref[rows, :]
        zs = []
        for hs in levels:
            tot = _split_row_bcast(p, hs, row)
            second = (row & hs) != 0
            e = jnp.exp2(jnp.where(second, p, tot - p)).astype(BF16)
            zs.append(jnp.where(second, qf, kf) * e)
            p = jnp.where(second, p + tot, p)
        b = p
        b_last = b[c - 1:c, :]
        qd = qf * jnp.exp2(b).astype(BF16)
        kd = kf * jnp.exp2(b_last - b).astype(BF16)
        d_last = jnp.exp2(b_last)
        for h in range(N_HEADS_B):
            hk = slice(h * DK_HEAD_B, (h + 1) * DK_HEAD_B)
            hv = slice(h * DV_HEAD_B, (h + 1) * DV_HEAD_B)
            vh = v_ref[rows, hv]
            a = mask_ref[n_lvl] * _dot_nt(qf[:, hk], kf[:, hk])
            for l in range(n_lvl):
                a = a + mask_ref[l] * _dot_nt(zs[l][:, hk], zs[l][:, hk])
            o = _dot(a.astype(BF16), vh)
            st = st_ref[h]
            o = o + _dot_nt(qd[:, hk], st.astype(BF16))
            st_ref[h] = st * d_last[:, hk] + _dot_tn(vh, kd[:, hk])
            o = _rms(o) * gn_ref[...] * rs_ref[rows, hv].astype(F32)
            o_ref[rows, hv] = o.astype(BF16)
        return carry

    lax.fori_loop(0, n_chunks, chunk, 0, unroll=unroll)

    def store_state():
        for h in range(N_HEADS_B):
            sout_ref[h] = st_ref[h].T

    if carry_state:
        pl.when(pl.program_id(0) == pl.num_programs(0) - 1)(store_state)
    else:
        store_state()


def _gla(qb, kb, vb, la, rb, gnorm, s0, *, c, tb, carry_state):
    t = qb.shape[0]
    n = t // tb
    masks = jnp.asarray(_gla_masks(c), F32)
    row = lambda w: pl.BlockSpec((tb, w), lambda i: (i, 0))
    st_shape = (N_HEADS_B, DK_HEAD_B, DV_HEAD_B)
    if carry_state:
        s_spec = pl.BlockSpec(st_shape, lambda i: (0, 0, 0))
        s_out_shape = st_shape
    else:
        s_spec = pl.BlockSpec((None,) + st_shape, lambda i: (i, 0, 0, 0))
        s_out_shape = (n,) + st_shape
    n_chunks = tb // c
    return pl.pallas_call(
        functools.partial(_gla_kernel, c=c, n_chunks=n_chunks, carry_state=carry_state,
                          unroll=min(GLA_UNROLL, n_chunks)),
        grid=(n,),
        in_specs=[row(DK_B), row(DK_B), row(DV_B), row(DK_B), row(DV_B), _const_spec((1, DV_HEAD_B)),
                  _const_spec(masks.shape), s_spec],
        out_specs=(row(DV_B), s_spec),
        out_shape=(jax.ShapeDtypeStruct((t, DV_B), BF16), jax.ShapeDtypeStruct(s_out_shape, F32)),
        scratch_shapes=[pltpu.VMEM((N_HEADS_B, DV_HEAD_B, DK_HEAD_B), F32)],
        compiler_params=_params(),
        name="gla",
    )(qb, kb, vb, la, rb, gnorm, masks, s0)


def _merge_ffn_kernel(x_ref, oa_ref, ob_ref, ga_ref, gb_ref, gpost_ref, gpre_ref, gfpost_ref,
                      wpa_ref, wpb_ref, wout_ref, wg_ref, wu_ref, wd_ref, y_ref, x1_ref, h_ref, acc_ref):
    pa = _dot(oa_ref[...], wpa_ref[...])
    pb = _dot(ob_ref[...], wpb_ref[...])
    mix = jax.nn.sigmoid(ga_ref[...].astype(F32)) * pa + jax.nn.sigmoid(gb_ref[...].astype(F32)) * pb
    x1 = x_ref[...] + _rms(_dot(mix.astype(BF16), wout_ref[...])) * gpost_ref[...]
    x1_ref[...] = x1
    h_ref[...] = (_rms(x1) * gpre_ref[...]).astype(BF16)
    acc_ref[...] = jnp.zeros_like(acc_ref)

    for j in range(N_FF_CHUNKS):
        cols = slice(j * FF_CHUNK, (j + 1) * FF_CHUNK)
        h = h_ref[...]
        g = _dot(h, wg_ref[:, cols])
        u = _dot(h, wu_ref[:, cols])
        acc_ref[...] += _dot((g * jax.nn.sigmoid(g) * u).astype(BF16), wd_ref[cols, :])
    y_ref[...] = x1_ref[...] + _rms(acc_ref[...]) * gfpost_ref[...]


def _merge_ffn(x, oa, ob, ga, gb, gpost, gpre, gfpost, wpa, wpb, wout, wg, wu, wd):
    t = x.shape[0]
    n = t // TM
    row = lambda w: pl.BlockSpec((TM, w), lambda i: (i, 0))
    vec = _const_spec((1, D_MODEL))
    return pl.pallas_call(
        _merge_ffn_kernel,
        grid=(n,),
        in_specs=[row(D_MODEL), row(D_A), row(DV_B), row(D_MODEL), row(D_MODEL), vec, vec, vec,
                  _const_spec(wpa.shape), _const_spec(wpb.shape), _const_spec(wout.shape),
                  _const_spec(wg.shape), _const_spec(wu.shape), _const_spec(wd.shape)],
        out_specs=row(D_MODEL),
        out_shape=jax.ShapeDtypeStruct((t, D_MODEL), F32),
        scratch_shapes=[pltpu.VMEM((TM, D_MODEL), F32), pltpu.VMEM((TM, D_MODEL), BF16),
                        pltpu.VMEM((TM, D_MODEL), F32)],
        compiler_params=_params(),
        name="merge_ffn",
    )(x, oa, ob, ga, gb, gpost, gpre, gfpost, wpa, wpb, wout, wg, wu, wd)


def _rel_bias_row(table):
    n_hi = BAND_PAST + BIAS_ORIGIN - REL_CLIP
    n_lo = BIAS_F_LEN - n_hi - (2 * REL_CLIP + 1)
    h = table.shape[0]
    return jnp.concatenate([jnp.broadcast_to(table[:, -1:], (h, n_hi)), table[:, ::-1],
                            jnp.broadcast_to(table[:, :1], (h, n_lo))], axis=1)


def kernel(x_prompt, x_sample, cache_attn_k, cache_attn_v, state_gla, norm_mix_pre, norm_mix_post, norm_ffn_pre, norm_ffn_post, w_in, w_decay_up, b_decay, rel_bias, gla_norm, w_proj_a, w_proj_b, w_out, w_ffn_gate, w_ffn_up, w_ffn_down):
    depth = w_in.shape[0]
    assert depth == 1, "single-layer step"
    batch, seq, _ = x_prompt.shape
    dec_batch, dec_seq, _ = x_sample.shape
    assert batch == 1 and seq % TM == 0 and (dec_batch * dec_seq) % TM == 0
    past = cache_attn_k.shape[2]

    w_bf = w_in[0].astype(BF16)
    wup_pad = jnp.concatenate([w_decay_up[0], jnp.zeros((DLR_PAD - GATE_RANK, DK_B), F32)], axis=0).astype(BF16)
    bdec = b_decay[0][None, :]
    vec = lambda a: a[0][None, :]
    wpa, wpb, wout = w_proj_a[0].astype(BF16), w_proj_b[0].astype(BF16), w_out[0].astype(BF16)
    wg, wu, wd = w_ffn_gate[0].astype(BF16), w_ffn_up[0].astype(BF16), w_ffn_down[0].astype(BF16)
    gnorm = gla_norm[0][None, :]

    def layer_tail(x, oa, ob, ga, gb):
        return _merge_ffn(x, oa, ob, ga, gb, vec(norm_mix_post), vec(norm_ffn_pre), vec(norm_ffn_post),
                          wpa, wpb, wout, wg, wu, wd)

    xp = x_prompt[0]
    qa, ka, va, qb, kb, vb, rb, la, ga, gb, kf, vf = _in_proj(
        xp, vec(norm_mix_pre), w_bf, wup_pad, bdec, kv_rows_every_step=False)
    f_tab = _rel_bias_row(rel_bias[0])
    oa = _attn_prompt(qa, ka, va, f_tab)
    s0 = jnp.zeros((N_HEADS_B, DK_HEAD_B, DV_HEAD_B), F32)
    ob, sp = _gla(qb, kb, vb, la, rb, gnorm, s0, c=CHUNK, tb=GLA_TB, carry_state=True)
    yp = layer_tail(xp, oa, ob, ga, gb)
    keep = min(BAND_PAST, seq)
    assert keep == TM

    xs = x_sample.reshape(dec_batch * dec_seq, D_MODEL)
    qa, ka, va, qb, kb, vb, rb, la, ga, gb, kfs, vfs = _in_proj(
        xs, vec(norm_mix_pre), w_bf, wup_pad, bdec, kv_rows_every_step=True)
    n_keys = past + dec_seq
    pad = (-n_keys) % LANES
    assert past == BAND_PAST
    oa = _attn_sample(qa, kfs, vfs, cache_attn_k[0].reshape(dec_batch, past, D_A),
                      cache_attn_v[0].reshape(dec_batch, past, D_A), f_tab, seq=dec_seq, pad=pad)
    gla_chunk = CHUNK if dec_seq % CHUNK == 0 else dec_seq
    ob, ss = _gla(qb, kb, vb, la, rb, gnorm, state_gla[0], c=gla_chunk, tb=dec_seq, carry_state=False)
    ys = layer_tail(xs, oa, ob, ga, gb)

    hd = (N_HEADS_A, HEAD_DIM_A)
    return (yp[None], ys.reshape(dec_batch, dec_seq, D_MODEL),
            kf.reshape((1, 1, keep) + hd), vf.reshape((1, 1, keep) + hd), sp[None, None],
            kfs.reshape((1, dec_batch, dec_seq) + hd), vfs.reshape((1, dec_batch, dec_seq) + hd), ss[None])
```

```python
import functools

import numpy as np
import jax
import jax.numpy as jnp
from jax import lax
from jax.experimental import pallas as pl
from jax.experimental.pallas import tpu as pltpu

F32 = jnp.float32
BF16 = jnp.bfloat16

D_MODEL = 1024
CHUNK = 64
BAND_CHUNKS = 8
BAND_PAST = BAND_CHUNKS * CHUNK
N_HEADS_A = 8
HEAD_DIM_A = 64
D_A = N_HEADS_A * HEAD_DIM_A
REL_CLIP = 128
N_HEADS_B = 4
DK_HEAD_B = 128
DV_HEAD_B = 256
DK_B = N_HEADS_B * DK_HEAD_B
DV_B = N_HEADS_B * DV_HEAD_B
GATE_RANK = 16
GATE_TEMP = 16.0
D_FF = 2816
EPS = 1e-6

LANES = 128
SUBLANES = 8
BF16_ROWS = 16
LOG2E = 1.4426950408889634
VMEM_LIMIT_BYTES = 56 * 1024 * 1024
NEG_BIG = -1e30

_PIECES = (("qa", D_A), ("ka", D_A), ("va", D_A), ("qb", DK_B), ("kb", DK_B), ("vb", DV_B),
           ("rb", DV_B), ("dlr", GATE_RANK), ("ga", D_MODEL), ("gb", D_MODEL))
_OFF = {}
_o = 0
for _n, _w in _PIECES:
    _OFF[_n] = (_o, _o + _w)
    _o += _w
D_IN = _o
DLR_PAD = LANES
assert _OFF["dlr"][0] % LANES == 0

TM = 512
ATT_QB = 256
ATT_WIN = ATT_QB + BAND_PAST
BIAS_ORIGIN = ATT_QB
BIAS_F_LEN = 1024
GLA_TB = 512
GLA_CHUNK = 128
GLA_UNROLL = 2
GLA_SEQS_PER_STEP = 4
FF_CHUNK = 256
REALIGN_ROWS = 128
N_FF_CHUNKS = D_FF // FF_CHUNK


def _const_spec(shape):
    nd = len(shape)
    return pl.BlockSpec(shape, lambda i: (0,) * nd, pipeline_mode=pl.Buffered(1))


def _params():
    return pltpu.CompilerParams(dimension_semantics=("arbitrary",), vmem_limit_bytes=VMEM_LIMIT_BYTES)


def _rms(x):
    return x * lax.rsqrt(jnp.mean(x * x, axis=-1, keepdims=True) + EPS)


def _dot(a, b):
    return jnp.dot(a, b, preferred_element_type=F32)


def _dot_nt(a, b):
    return lax.dot_general(a, b, (((1,), (1,)), ((), ())), preferred_element_type=F32)


def _dot_tn(a, b):
    return lax.dot_general(a, b, (((0,), (0,)), ((), ())), preferred_element_type=F32)


def _in_proj_kernel(x_ref, g_ref, w_ref, wup_ref, bdec_ref,
                    qa_ref, ka_ref, va_ref, qb_ref, kb_ref, vb_ref, rs_ref, la_ref, ga_ref, gb_ref,
                    kf_ref, vf_ref, wgate_ref, *, kv_rows_every_step):
    dlr_lo = _OFF["dlr"][0]

    @pl.when(pl.program_id(0) == 0)
    def _():
        for r in range(0, D_MODEL, REALIGN_ROWS):
            tail = w_ref[r:r + REALIGN_ROWS, dlr_lo:]
            for j, name in enumerate(("ga", "gb")):
                lo, hi = _OFF[name]
                wgate_ref[j, r:r + REALIGN_ROWS, :] = tail[:, lo - dlr_lo:hi - dlr_lo]

    h = (_rms(x_ref[...]) * g_ref[...]).astype(BF16)

    def proj(name):
        lo, hi = _OFF[name]
        return _dot(h, w_ref[:, lo:hi])

    dlr = _dot(h, w_ref[:, dlr_lo:dlr_lo + DLR_PAD])
    z = _dot(dlr.astype(BF16), wup_ref[...]) + bdec_ref[...]
    la_ref[...] = (jnp.minimum(z, 0.0) - jnp.log1p(jnp.exp(-jnp.abs(z)))) * (LOG2E / GATE_TEMP)
    qa_ref[...] = (proj("qa") * (HEAD_DIM_A ** -0.5 * LOG2E)).astype(BF16)
    ka = proj("ka")
    va = proj("va")
    ka_ref[...] = ka.astype(BF16)
    va_ref[...] = va.astype(BF16)
    if kv_rows_every_step:
        kf_ref[...] = ka
        vf_ref[...] = va
    else:
        @pl.when(pl.program_id(0) == pl.num_programs(0) - 1)
        def _():
            kf_ref[...] = ka
            vf_ref[...] = va
    qb_ref[...] = (proj("qb") * (DK_HEAD_B ** -0.5)).astype(BF16)
    kb_ref[...] = proj("kb").astype(BF16)
    vb_ref[...] = proj("vb").astype(BF16)
    r = proj("rb")
    rs_ref[...] = (r * jax.nn.sigmoid(r)).astype(BF16)
    ga_ref[...] = _dot(h, wgate_ref[0]).astype(BF16)
    gb_ref[...] = _dot(h, wgate_ref[1]).astype(BF16)


def _in_proj(x, g, w_bf, wup_pad, bdec, *, kv_rows_every_step):
    t = x.shape[0]
    n = t // TM
    row = lambda w: pl.BlockSpec((TM, w), lambda i: (i, 0))
    if kv_rows_every_step:
        kv_rows, kv_spec = t, row(D_A)
    else:
        kv_rows, kv_spec = TM, pl.BlockSpec((TM, D_A), lambda i: (0, 0))
    widths = (D_A, D_A, D_A, DK_B, DK_B, DV_B, DV_B, DK_B, D_MODEL, D_MODEL)
    dtypes = (BF16,) * 7 + (F32, BF16, BF16)
    out_shape = tuple(jax.ShapeDtypeStruct((t, w), d) for w, d in zip(widths, dtypes))
    out_shape += (jax.ShapeDtypeStruct((kv_rows, D_A), F32),) * 2
    out_specs = tuple(row(w) for w in widths) + (kv_spec, kv_spec)
    return pl.pallas_call(
        functools.partial(_in_proj_kernel, kv_rows_every_step=kv_rows_every_step),
        grid=(n,),
        in_specs=[row(D_MODEL), _const_spec((1, D_MODEL)), _const_spec((D_MODEL, D_IN)),
                  _const_spec((DLR_PAD, DK_B)), _const_spec((1, DK_B))],
        out_specs=out_specs,
        out_shape=out_shape,
        scratch_shapes=[pltpu.VMEM((2, D_MODEL, D_MODEL), BF16)],
        compiler_params=_params(),
        name="in_proj",
    )(x, g, w_bf, wup_pad, bdec)


def _softmax_pv(s, v):
    m = jnp.max(s, axis=-1, keepdims=True)
    e = jnp.exp2(s - m).astype(BF16)
    pv = _dot(e, jnp.concatenate([v, jnp.ones_like(v)], axis=1))
    return pv[:, :LANES] / pv[:, LANES:LANES + 1]


def _head_pair_attention(q_pair, k_pair, v_pair, bias0, bias1):
    nq = q_pair.shape[0]
    lane = lax.broadcasted_iota(jnp.int32, (1, LANES), 1)
    first = lane < HEAD_DIM_A
    zero = jnp.zeros_like(q_pair)
    qs = jnp.concatenate([jnp.where(first, q_pair, zero), jnp.where(first, zero, q_pair)], axis=0)
    s = _dot_nt(qs, k_pair) + jnp.concatenate([bias0, bias1], axis=0)
    pv = _softmax_pv(s, v_pair)
    return jnp.where(first, pv[:nq], pv[nq:])


def _expand_rel_bias(f_ref, bias_ref, n_q, n_k, first_valid, n_valid, band):
    t = lax.broadcasted_iota(jnp.int32, (n_q, n_k), 0)
    k = lax.broadcasted_iota(jnp.int32, (n_q, n_k), 1)
    keep = (k < n_valid) & (k >= first_valid)
    if band:
        jq, jk = t // CHUNK, k // CHUNK
        keep = keep & (jk >= jq) & (jk <= jq + BAND_CHUNKS)
    for h in range(N_HEADS_A):
        rows = jnp.broadcast_to(f_ref[h:h + 1, :], (n_q, BIAS_F_LEN))
        rolled = pltpu.roll(rows, BIAS_F_LEN - BIAS_ORIGIN, 1, stride=1, stride_axis=0)
        bias_ref[h] = jnp.where(keep, rolled[:, :n_k] * LOG2E, NEG_BIG)


def _attn_prompt_kernel(q_ref, k0_ref, k1_ref, k2_ref, v0_ref, v1_ref, v2_ref, f_ref, o_ref, bias_ref):
    i = pl.program_id(0)

    @pl.when(i * ATT_QB <= BAND_PAST)
    def _():
        _expand_rel_bias(f_ref, bias_ref, ATT_QB, ATT_WIN, BAND_PAST - ATT_QB * i, ATT_WIN, band=True)

    kwin = jnp.concatenate([k0_ref[...], k1_ref[...], k2_ref[...]], axis=0)
    vwin = jnp.concatenate([v0_ref[...], v1_ref[...], v2_ref[...]], axis=0)
    for p in range(N_HEADS_A // 2):
        sl = slice(p * LANES, (p + 1) * LANES)
        o = _head_pair_attention(q_ref[:, sl], kwin[:, sl], vwin[:, sl],
                                 bias_ref[2 * p], bias_ref[2 * p + 1])
        o_ref[:, sl] = o.astype(BF16)


def _attn_prompt(qa, ka, va, f_tab):
    t = qa.shape[0]
    n = t // ATT_QB
    blk = lambda back: pl.BlockSpec((ATT_QB, D_A), lambda i: (jnp.maximum(i - back, 0), 0))
    return pl.pallas_call(
        _attn_prompt_kernel,
        grid=(n,),
        in_specs=[blk(0), blk(2), blk(1), blk(0), blk(2), blk(1), blk(0),
                  _const_spec((N_HEADS_A, BIAS_F_LEN))],
        out_specs=blk(0),
        out_shape=jax.ShapeDtypeStruct((t, D_A), BF16),
        scratch_shapes=[pltpu.VMEM((N_HEADS_A, ATT_QB, ATT_WIN), F32)],
        compiler_params=_params(),
        name="attn_prompt",
    )(qa, ka, ka, ka, va, va, va, f_tab)


def _attn_sample_kernel(q_ref, kc_ref, vc_ref, kn_ref, vn_ref, f_ref, o_ref, bias_ref, *, pad):
    n_q, n_k = bias_ref.shape[1:]

    @pl.when(pl.program_id(0) == 0)
    def _():
        _expand_rel_bias(f_ref, bias_ref, n_q, n_k, 0, n_k - pad, band=False)

    zpad = jnp.zeros((pad, D_A), BF16)
    kall = jnp.concatenate([kc_ref[...], kn_ref[...].astype(BF16), zpad], axis=0)
    vall = jnp.concatenate([vc_ref[...], vn_ref[...].astype(BF16), zpad], axis=0)
    for p in range(N_HEADS_A // 2):
        sl = slice(p * LANES, (p + 1) * LANES)
        o = _head_pair_attention(q_ref[:, sl], kall[:, sl], vall[:, sl],
                                 bias_ref[2 * p], bias_ref[2 * p + 1])
        o_ref[:, sl] = o.astype(BF16)


def _attn_sample(qa, kn, vn, cache_k, cache_v, f_tab, *, seq, pad):
    t = qa.shape[0]
    nb, w, _ = cache_k.shape
    row = pl.BlockSpec((seq, D_A), lambda i: (i, 0))
    cache = pl.BlockSpec((None, w, D_A), lambda i: (i, 0, 0))
    return pl.pallas_call(
        functools.partial(_attn_sample_kernel, pad=pad),
        grid=(nb,),
        in_specs=[row, cache, cache, row, row, _const_spec((N_HEADS_A, BIAS_F_LEN))],
        out_specs=row,
        out_shape=jax.ShapeDtypeStruct((t, D_A), BF16),
        scratch_shapes=[pltpu.VMEM((N_HEADS_A, seq, w + seq + pad), F32)],
        compiler_params=_params(),
        name="attn_sample",
    )(qa, cache_k, cache_v, kn, vn, f_tab)


def _gla_levels(c):
    out, hs = [], 1
    while hs < c:
        out.append(hs)
        hs *= 2
    return out


def _gla_masks(c):
    t = np.arange(c)
    masks = []
    for hs in _gla_levels(c):
        blk = t // (2 * hs)
        second = (t % (2 * hs)) >= hs
        masks.append((blk[:, None] == blk[None, :]) & second[:, None] & ~second[None, :])
    masks.append(np.eye(c, dtype=bool))
    return np.stack(masks).astype(np.float32)


def _split_row_bcast(x, hs, row):
    c, w = x.shape
    blk = 2 * hs
    if blk >= SUBLANES:
        xr = x.reshape(c // blk, blk, w)
        return jnp.broadcast_to(xr[:, hs - 1:hs, :], (c // blk, blk, w)).reshape(c, w)
    tiles = (c // SUBLANES, SUBLANES, w)
    x3, pos = x.reshape(tiles), (row & (blk - 1)).reshape(tiles)
    if hs == 1:
        out = jnp.where(pos == 1, pltpu.roll(x3, 1, 1), x3)
    else:
        assert hs == 2
        nxt = jnp.where((pos & 1) == 1, x3, pltpu.roll(x3, SUBLANES - 1, 1))
        out = jnp.where(pos >= 2, pltpu.roll(nxt, 2, 1), nxt)
    return out.reshape(c, w)


def _level_step(p, qf, kf, hs, row):
    c, w = p.shape
    if hs < SUBLANES:
        tot = _split_row_bcast(p, hs, row)
        second = (row & hs) != 0
        return jnp.where(second, p, tot - p), jnp.where(second, p + tot, p), jnp.where(second, qf, kf)
    xs, ps, qks = [], [], []
    for lo in range(0, c, 2 * hs):
        first, second = p[lo:lo + hs], p[lo + hs:lo + 2 * hs]
        tot = jnp.broadcast_to(first[hs - 1:hs], (hs, w))
        xs += [tot - first, second]
        ps += [first, second + tot]
        qks += [kf[lo:lo + hs], qf[lo + hs:lo + 2 * hs]]
    if hs % BF16_ROWS == 0:
        qk = jnp.concatenate(qks, axis=0)
    else:
        qk = jnp.where((row & hs) != 0, qf, kf)
    return jnp.concatenate(xs, axis=0), jnp.concatenate(ps, axis=0), qk


def _gla_chunk(rows, c, q_ref, k_ref, v_ref, la_ref, rs_ref, gn_ref, mask_ref, st_in, st_out, o_ref,
               *, state_t):
    levels = _gla_levels(c)
    n_lvl = len(levels)
    row = lax.broadcasted_iota(jnp.int32, (c, DK_HEAD_B), 0)
    for h in range(N_HEADS_B):
        hk = slice(h * DK_HEAD_B, (h + 1) * DK_HEAD_B)
        hv = slice(h * DV_HEAD_B, (h + 1) * DV_HEAD_B)
        qf = q_ref[rows, hk]
        kf = k_ref[rows, hk]
        vh = v_ref[rows, hv]
        p = la_ref[rows, hk]
        a = mask_ref[n_lvl] * _dot_nt(qf, kf)
        for l, hs in enumerate(levels):
            x, p, qk = _level_step(p, qf, kf, hs, row)
            z = qk * jnp.exp2(x).astype(BF16)
            a = a + mask_ref[l] * _dot_nt(z, z)
        b = p
        b_last = b[c - 1:c, :]
        o = _dot(a.astype(BF16), vh)
        st = st_in[h]
        qd = qf * jnp.exp2(b).astype(BF16)
        kd = kf * jnp.exp2(b_last - b).astype(BF16)
        d_last = jnp.exp2(b_last)
        if state_t:
            o = o + _dot_nt(qd, st.astype(BF16))
            st_out[h] = st * d_last + _dot_tn(vh, kd)
        else:
            o = o + _dot(qd, st.astype(BF16))
            d_col = jnp.broadcast_to(d_last, (SUBLANES, DK_HEAD_B)).T[:, 0:1]
            st_out[h] = st * d_col + _dot_tn(kd, vh)
        o = _rms(o) * gn_ref[...] * rs_ref[rows, hv].astype(F32)
        o_ref[rows, hv] = o.astype(BF16)


def _gla_stream_kernel(q_ref, k_ref, v_ref, la_ref, rs_ref, gn_ref, mask_ref, s0_ref,
                       o_ref, sout_ref, st_ref, *, c, n_chunks, unroll):
    @pl.when(pl.program_id(0) == 0)
    def _():
        for h in range(N_HEADS_B):
            st_ref[h] = s0_ref[h].T

    def chunk(ci, carry):
        rows = pl.ds(pl.multiple_of(ci * c, c), c)
        _gla_chunk(rows, c, q_ref, k_ref, v_ref, la_ref, rs_ref, gn_ref, mask_ref, st_ref, st_ref, o_ref,
                   state_t=True)
        return carry

    lax.fori_loop(0, n_chunks, chunk, 0, unroll=unroll)

    @pl.when(pl.program_id(0) == pl.num_programs(0) - 1)
    def _():
        for h in range(N_HEADS_B):
            sout_ref[h] = st_ref[h].T


def _gla_seqs_kernel(q_ref, k_ref, v_ref, la_ref, rs_ref, gn_ref, mask_ref, s0_ref, o_ref, sout_ref,
                     *, c, n_seqs):
    for j in range(n_seqs):
        _gla_chunk(slice(j * c, (j + 1) * c), c, q_ref, k_ref, v_ref, la_ref, rs_ref, gn_ref, mask_ref,
                   s0_ref.at[j], sout_ref.at[j], o_ref, state_t=False)


def _gla_stream(qb, kb, vb, la, rs, gnorm, s0, *, c, tb):
    t = qb.shape[0]
    masks = jnp.asarray(_gla_masks(c), F32)
    row = lambda w: pl.BlockSpec((tb, w), lambda i: (i, 0))
    st_shape = (N_HEADS_B, DK_HEAD_B, DV_HEAD_B)
    s_spec = pl.BlockSpec(st_shape, lambda i: (0, 0, 0))
    n_chunks = tb // c
    return pl.pallas_call(
        functools.partial(_gla_stream_kernel, c=c, n_chunks=n_chunks, unroll=min(GLA_UNROLL, n_chunks)),
        grid=(t // tb,),
        in_specs=[row(DK_B), row(DK_B), row(DV_B), row(DK_B), row(DV_B), _const_spec((1, DV_HEAD_B)),
                  _const_spec(masks.shape), s_spec],
        out_specs=(row(DV_B), s_spec),
        out_shape=(jax.ShapeDtypeStruct((t, DV_B), BF16), jax.ShapeDtypeStruct(st_shape, F32)),
        scratch_shapes=[pltpu.VMEM((N_HEADS_B, DV_HEAD_B, DK_HEAD_B), F32)],
        compiler_params=_params(),
        name="gla_stream",
    )(qb, kb, vb, la, rs, gnorm, masks, s0)


def _gla_seqs(qb, kb, vb, la, rs, gnorm, s0, *, c):
    t = qb.shape[0]
    n = s0.shape[0]
    assert t == n * c and n % GLA_SEQS_PER_STEP == 0
    masks = jnp.asarray(_gla_masks(c), F32)
    tb = GLA_SEQS_PER_STEP * c
    row = lambda w: pl.BlockSpec((tb, w), lambda i: (i, 0))
    s_spec = pl.BlockSpec((GLA_SEQS_PER_STEP,) + s0.shape[1:], lambda i: (i, 0, 0, 0))
    return pl.pallas_call(
        functools.partial(_gla_seqs_kernel, c=c, n_seqs=GLA_SEQS_PER_STEP),
        grid=(n // GLA_SEQS_PER_STEP,),
        in_specs=[row(DK_B), row(DK_B), row(DV_B), row(DK_B), row(DV_B), _const_spec((1, DV_HEAD_B)),
                  _const_spec(masks.shape), s_spec],
        out_specs=(row(DV_B), s_spec),
        out_shape=(jax.ShapeDtypeStruct((t, DV_B), BF16), jax.ShapeDtypeStruct(s0.shape, F32)),
        compiler_params=_params(),
        name="gla_seqs",
    )(qb, kb, vb, la, rs, gnorm, masks, s0)


def _merge_ffn_kernel(x_ref, oa_ref, ob_ref, ga_ref, gb_ref, gpost_ref, gpre_ref, gfpost_ref,
                      wpa_ref, wpb_ref, wout_ref, wg_ref, wu_ref, wd_ref, y_ref, x1_ref, h_ref, acc_ref):
    pa = _dot(oa_ref[...], wpa_ref[...])
    pb = _dot(ob_ref[...], wpb_ref[...])
    mix = jax.nn.sigmoid(ga_ref[...].astype(F32)) * pa + jax.nn.sigmoid(gb_ref[...].astype(F32)) * pb
    x1 = x_ref[...] + _rms(_dot(mix.astype(BF16), wout_ref[...])) * gpost_ref[...]
    x1_ref[...] = x1
    h_ref[...] = (_rms(x1) * gpre_ref[...]).astype(BF16)
    acc_ref[...] = jnp.zeros_like(acc_ref)

    for j in range(N_FF_CHUNKS):
        cols = slice(j * FF_CHUNK, (j + 1) * FF_CHUNK)
        h = h_ref[...]
        g = _dot(h, wg_ref[:, cols])
        u = _dot(h, wu_ref[:, cols])
        acc_ref[...] += _dot((g * jax.nn.sigmoid(g) * u).astype(BF16), wd_ref[cols, :])
    y_ref[...] = x1_ref[...] + _rms(acc_ref[...]) * gfpost_ref[...]


def _merge_ffn(x, oa, ob, ga, gb, gpost, gpre, gfpost, wpa, wpb, wout, wg, wu, wd):
    t = x.shape[0]
    n = t // TM
    row = lambda w: pl.BlockSpec((TM, w), lambda i: (i, 0))
    vec = _const_spec((1, D_MODEL))
    return pl.pallas_call(
        _merge_ffn_kernel,
        grid=(n,),
        in_specs=[row(D_MODEL), row(D_A), row(DV_B), row(D_MODEL), row(D_MODEL), vec, vec, vec,
                  _const_spec(wpa.shape), _const_spec(wpb.shape), _const_spec(wout.shape),
                  _const_spec(wg.shape), _const_spec(wu.shape), _const_spec(wd.shape)],
        out_specs=row(D_MODEL),
        out_shape=jax.ShapeDtypeStruct((t, D_MODEL), F32),
        scratch_shapes=[pltpu.VMEM((TM, D_MODEL), F32), pltpu.VMEM((TM, D_MODEL), BF16),
                        pltpu.VMEM((TM, D_MODEL), F32)],
        compiler_params=_params(),
        name="merge_ffn",
    )(x, oa, ob, ga, gb, gpost, gpre, gfpost, wpa, wpb, wout, wg, wu, wd)


def _rel_bias_row(table):
    n_hi = BAND_PAST + BIAS_ORIGIN - REL_CLIP
    n_lo = BIAS_F_LEN - n_hi - (2 * REL_CLIP + 1)
    h = table.shape[0]
    return jnp.concatenate([jnp.broadcast_to(table[:, -1:], (h, n_hi)), table[:, ::-1],
                            jnp.broadcast_to(table[:, :1], (h, n_lo))], axis=1)


def kernel(x_prompt, x_sample, cache_attn_k, cache_attn_v, state_gla, norm_mix_pre, norm_mix_post, norm_ffn_pre, norm_ffn_post, w_in, w_decay_up, b_decay, rel_bias, gla_norm, w_proj_a, w_proj_b, w_out, w_ffn_gate, w_ffn_up, w_ffn_down):
    depth = w_in.shape[0]
    assert depth == 1, "single-layer step"
    batch, seq, _ = x_prompt.shape
    dec_batch, dec_seq, _ = x_sample.shape
    assert batch == 1 and seq % TM == 0 and (dec_batch * dec_seq) % TM == 0
    past = cache_attn_k.shape[2]

    w_bf = w_in[0].astype(BF16)
    wup_pad = jnp.concatenate([w_decay_up[0], jnp.zeros((DLR_PAD - GATE_RANK, DK_B), F32)], axis=0).astype(BF16)
    bdec = b_decay[0][None, :]
    vec = lambda a: a[0][None, :]
    wpa, wpb, wout = w_proj_a[0].astype(BF16), w_proj_b[0].astype(BF16), w_out[0].astype(BF16)
    wg, wu, wd = w_ffn_gate[0].astype(BF16), w_ffn_up[0].astype(BF16), w_ffn_down[0].astype(BF16)
    gnorm = gla_norm[0][None, :]

    def layer_tail(x, oa, ob, ga, gb):
        return _merge_ffn(x, oa, ob, ga, gb, vec(norm_mix_post), vec(norm_ffn_pre), vec(norm_ffn_post),
                          wpa, wpb, wout, wg, wu, wd)

    xp = x_prompt[0]
    qa, ka, va, qb, kb, vb, rs, la, ga, gb, kf, vf = _in_proj(
        xp, vec(norm_mix_pre), w_bf, wup_pad, bdec, kv_rows_every_step=False)
    f_tab = _rel_bias_row(rel_bias[0])
    oa = _attn_prompt(qa, ka, va, f_tab)
    s0 = jnp.zeros((N_HEADS_B, DK_HEAD_B, DV_HEAD_B), F32)
    ob, sp = _gla_stream(qb, kb, vb, la, rs, gnorm, s0, c=GLA_CHUNK, tb=GLA_TB)
    yp = layer_tail(xp, oa, ob, ga, gb)
    keep = min(BAND_PAST, seq)
    assert keep == TM

    xs = x_sample.reshape(dec_batch * dec_seq, D_MODEL)
    qa, ka, va, qb, kb, vb, rs, la, ga, gb, kfs, vfs = _in_proj(
        xs, vec(norm_mix_pre), w_bf, wup_pad, bdec, kv_rows_every_step=True)
    n_keys = past + dec_seq
    pad = (-n_keys) % LANES
    assert past == BAND_PAST
    oa = _attn_sample(qa, kfs, vfs, cache_attn_k[0].reshape(dec_batch, past, D_A).astype(BF16),
                      cache_attn_v[0].reshape(dec_batch, past, D_A).astype(BF16), f_tab, seq=dec_seq, pad=pad)
    gla_chunk = CHUNK if dec_seq % CHUNK == 0 else dec_seq
    assert gla_chunk == dec_seq
    ob, ss = _gla_seqs(qb, kb, vb, la, rs, gnorm, state_gla[0], c=gla_chunk)
    ys = layer_tail(xs, oa, ob, ga, gb)

    hd = (N_HEADS_A, HEAD_DIM_A)
    return (yp[None], ys.reshape(dec_batch, dec_seq, D_MODEL),
            kf.reshape((1, 1, keep) + hd), vf.reshape((1, 1, keep) + hd), sp[None, None],
            kfs.reshape((1, dec_batch, dec_seq) + hd), vfs.reshape((1, dec_batch, dec_seq) + hd), ss[None])
```

```python
import functools

import numpy as np
import jax
import jax.numpy as jnp
from jax import lax
from jax.experimental import pallas as pl
from jax.experimental.pallas import tpu as pltpu

F32 = jnp.float32
BF16 = jnp.bfloat16

D_MODEL = 1024
CHUNK = 64
BAND_CHUNKS = 8
BAND_PAST = BAND_CHUNKS * CHUNK
N_HEADS_A = 8
HEAD_DIM_A = 64
D_A = N_HEADS_A * HEAD_DIM_A
REL_CLIP = 128
N_HEADS_B = 4
DK_HEAD_B = 128
DV_HEAD_B = 256
DK_B = N_HEADS_B * DK_HEAD_B
DV_B = N_HEADS_B * DV_HEAD_B
GATE_RANK = 16
GATE_TEMP = 16.0
D_FF = 2816
EPS = 1e-6

LANES = 128
SUBLANES = 8
BF16_ROWS = 16
LOG2E = 1.4426950408889634
VMEM_LIMIT_BYTES = 56 * 1024 * 1024
NEG_BIG = -1e30

_PIECES = (("qa", D_A), ("ka", D_A), ("va", D_A), ("qb", DK_B), ("kb", DK_B), ("vb", DV_B),
           ("rb", DV_B), ("dlr", GATE_RANK), ("ga", D_MODEL), ("gb", D_MODEL))
_OFF = {}
_o = 0
for _n, _w in _PIECES:
    _OFF[_n] = (_o, _o + _w)
    _o += _w
D_IN = _o
DLR_PAD = LANES
assert _OFF["dlr"][0] % LANES == 0

TM = 512
ATT_QB = 256
ATT_WIN = ATT_QB + BAND_PAST
ATT_SUB = 2
BIAS_ORIGIN = ATT_QB
BIAS_F_LEN = 1024
GLA_TB = 512
GLA_CHUNK = 128
GLA_UNROLL = 2
GLA_SEQS_PER_STEP = 4
FF_CHUNK = 256
REALIGN_ROWS = 128
N_FF_CHUNKS = D_FF // FF_CHUNK


def _const_spec(shape):
    nd = len(shape)
    return pl.BlockSpec(shape, lambda i: (0,) * nd, pipeline_mode=pl.Buffered(1))


def _params():
    return pltpu.CompilerParams(dimension_semantics=("arbitrary",), vmem_limit_bytes=VMEM_LIMIT_BYTES)


def _rms(x):
    return x * lax.rsqrt(jnp.mean(x * x, axis=-1, keepdims=True) + EPS)


def _dot(a, b):
    return jnp.dot(a, b, preferred_element_type=F32)


def _dot_nt(a, b):
    return lax.dot_general(a, b, (((1,), (1,)), ((), ())), preferred_element_type=F32)


def _dot_tn(a, b):
    return lax.dot_general(a, b, (((0,), (0,)), ((), ())), preferred_element_type=F32)


def _in_proj_kernel(xp_ref, xs_ref, g_ref, w_ref, wup_ref, bdec_ref,
                    qa_ref, ka_ref, va_ref, qb_ref, kb_ref, vb_ref, rs_ref, la_ref, ga_ref, gb_ref,
                    kfp_ref, vfp_ref, kfs_ref, vfs_ref, wgate_ref, h_ref, *, n_prompt):
    dlr_lo = _OFF["dlr"][0]
    i = pl.program_id(0)

    @pl.when(pl.program_id(0) == 0)
    def _():
        for r in range(0, D_MODEL, REALIGN_ROWS):
            tail = w_ref[r:r + REALIGN_ROWS, dlr_lo:]
            for j, name in enumerate(("ga", "gb")):
                lo, hi = _OFF[name]
                wgate_ref[j, r:r + REALIGN_ROWS, :] = tail[:, lo - dlr_lo:hi - dlr_lo]

    def normalize(x_ref):
        h_ref[...] = (_rms(x_ref[...]) * g_ref[...]).astype(BF16)

    pl.when(i < n_prompt)(functools.partial(normalize, xp_ref))
    pl.when(i >= n_prompt)(functools.partial(normalize, xs_ref))
    h = h_ref[...]

    def proj(name):
        lo, hi = _OFF[name]
        return _dot(h, w_ref[:, lo:hi])

    dlr = _dot(h, w_ref[:, dlr_lo:dlr_lo + DLR_PAD])
    z = _dot(dlr.astype(BF16), wup_ref[...]) + bdec_ref[...]
    la_ref[...] = (jnp.minimum(z, 0.0) - jnp.log1p(jnp.exp(-jnp.abs(z)))) * (LOG2E / GATE_TEMP)
    qa_ref[...] = (proj("qa") * (HEAD_DIM_A ** -0.5 * LOG2E)).astype(BF16)
    ka = proj("ka")
    va = proj("va")
    ka_ref[...] = ka.astype(BF16)
    va_ref[...] = va.astype(BF16)
    @pl.when(i == n_prompt - 1)
    def _():
        kfp_ref[...] = ka
        vfp_ref[...] = va

    @pl.when(i >= n_prompt)
    def _():
        kfs_ref[...] = ka
        vfs_ref[...] = va

    qb_ref[...] = (proj("qb") * (DK_HEAD_B ** -0.5)).astype(BF16)
    kb_ref[...] = proj("kb").astype(BF16)
    vb_ref[...] = proj("vb").astype(BF16)
    r = proj("rb")
    rs_ref[...] = (r * jax.nn.sigmoid(r)).astype(BF16)
    ga_ref[...] = _dot(h, wgate_ref[0]).astype(BF16)
    gb_ref[...] = _dot(h, wgate_ref[1]).astype(BF16)


def _in_proj(xp, xs, g, w_bf, wup_pad, bdec):
    tp, ts = xp.shape[0], xs.shape[0]
    n_p, n_s = tp // TM, ts // TM
    row = lambda w: pl.BlockSpec((TM, w), lambda i: (i, 0))
    p_blk = lambda w: pl.BlockSpec((TM, w), lambda i: (jnp.minimum(i, n_p - 1), 0))
    s_blk = lambda w: pl.BlockSpec((TM, w), lambda i: (jnp.maximum(i - n_p, 0), 0))
    s_in = lambda w: pl.BlockSpec((TM, w), lambda i: (jnp.maximum(i - n_p, 0), 0),
                                  pipeline_mode=pl.Buffered(1))
    last_p = pl.BlockSpec((TM, D_A), lambda i: (0, 0))
    widths = (D_A, D_A, D_A, DK_B, DK_B, DV_B, DV_B, DK_B, D_MODEL, D_MODEL)
    dtypes = (BF16,) * 7 + (F32, BF16, BF16)
    out_shape = tuple(jax.ShapeDtypeStruct((tp + ts, w), d) for w, d in zip(widths, dtypes))
    out_shape += (jax.ShapeDtypeStruct((TM, D_A), F32),) * 2 + (jax.ShapeDtypeStruct((ts, D_A), F32),) * 2
    out_specs = tuple(row(w) for w in widths) + (last_p, last_p, s_blk(D_A), s_blk(D_A))
    return pl.pallas_call(
        functools.partial(_in_proj_kernel, n_prompt=n_p),
        grid=(n_p + n_s,),
        in_specs=[p_blk(D_MODEL), s_in(D_MODEL), _const_spec((1, D_MODEL)), _const_spec((D_MODEL, D_IN)),
                  _const_spec((DLR_PAD, DK_B)), _const_spec((1, DK_B))],
        out_specs=out_specs,
        out_shape=out_shape,
        scratch_shapes=[pltpu.VMEM((2, D_MODEL, D_MODEL), BF16), pltpu.VMEM((TM, D_MODEL), BF16)],
        compiler_params=_params(),
        name="in_proj",
    )(xp, xs, g, w_bf, wup_pad, bdec)


def _softmax_pv(s, v):
    m = jnp.max(s, axis=-1, keepdims=True)
    e = jnp.exp2(s - m).astype(BF16)
    pv = _dot(e, jnp.concatenate([v, jnp.ones_like(v)], axis=1))
    return pv[:, :LANES] / pv[:, LANES:LANES + 1]


def _head_pair_attention(q_pair, k_pair, v_pair, bias0, bias1):
    nq = q_pair.shape[0]
    lane = lax.broadcasted_iota(jnp.int32, (1, LANES), 1)
    first = lane < HEAD_DIM_A
    zero = jnp.zeros_like(q_pair)
    qs = jnp.concatenate([jnp.where(first, q_pair, zero), jnp.where(first, zero, q_pair)], axis=0)
    s = _dot_nt(qs, k_pair) + jnp.concatenate([bias0, bias1], axis=0)
    pv = _softmax_pv(s, v_pair)
    return jnp.where(first, pv[:nq], pv[nq:])


def _expand_rel_bias(f_ref, bias_ref, n_q, n_k, first_valid, n_valid, band):
    t = lax.broadcasted_iota(jnp.int32, (n_q, n_k), 0)
    k = lax.broadcasted_iota(jnp.int32, (n_q, n_k), 1)
    keep = (k < n_valid) & (k >= first_valid)
    if band:
        jq, jk = t // CHUNK, k // CHUNK
        keep = keep & (jk >= jq) & (jk <= jq + BAND_CHUNKS)
    for h in range(N_HEADS_A):
        rows = jnp.broadcast_to(f_ref[h:h + 1, :], (n_q, BIAS_F_LEN))
        rolled = pltpu.roll(rows, BIAS_F_LEN - BIAS_ORIGIN, 1, stride=1, stride_axis=0)
        bias_ref[h] = jnp.where(keep, rolled[:, :n_k] * LOG2E, NEG_BIG)


def _attn_prompt_kernel(q_ref, *refs):
    n_kb = ATT_SUB + ATT_WIN // ATT_QB - 1
    k_refs, v_refs = refs[:n_kb], refs[n_kb:2 * n_kb]
    f_ref, o_ref, bias_ref = refs[2 * n_kb:]
    i = pl.program_id(0)

    @pl.when((i - 1) * ATT_SUB * ATT_QB < BAND_PAST)
    def _():
        for sub in range(ATT_SUB):
            first_valid = BAND_PAST - ATT_QB * (ATT_SUB * i + sub)
            _expand_rel_bias(f_ref, bias_ref.at[sub], ATT_QB, ATT_WIN, first_valid, ATT_WIN, band=True)

    for sub in range(ATT_SUB):
        kwin = jnp.concatenate([r[...] for r in k_refs[sub:sub + ATT_WIN // ATT_QB]], axis=0)
        vwin = jnp.concatenate([r[...] for r in v_refs[sub:sub + ATT_WIN // ATT_QB]], axis=0)
        rows = slice(sub * ATT_QB, (sub + 1) * ATT_QB)
        for p in range(N_HEADS_A // 2):
            sl = slice(p * LANES, (p + 1) * LANES)
            o = _head_pair_attention(q_ref[rows, sl], kwin[:, sl], vwin[:, sl],
                                     bias_ref[sub, 2 * p], bias_ref[sub, 2 * p + 1])
            o_ref[rows, sl] = o.astype(BF16)


def _attn_prompt(qa, ka, va, f_tab, *, t):
    n = t // (ATT_SUB * ATT_QB)
    n_kb = ATT_SUB + ATT_WIN // ATT_QB - 1
    kblk = [pl.BlockSpec((ATT_QB, D_A), functools.partial(
        lambda j, i: (jnp.maximum(ATT_SUB * i + j - (n_kb - ATT_SUB), 0), 0), j)) for j in range(n_kb)]
    qblk = pl.BlockSpec((ATT_SUB * ATT_QB, D_A), lambda i: (i, 0))
    return pl.pallas_call(
        _attn_prompt_kernel,
        grid=(n,),
        in_specs=[qblk] + kblk + kblk + [_const_spec((N_HEADS_A, BIAS_F_LEN))],
        out_specs=qblk,
        out_shape=jax.ShapeDtypeStruct((t, D_A), BF16),
        scratch_shapes=[pltpu.VMEM((ATT_SUB, N_HEADS_A, ATT_QB, ATT_WIN), F32)],
        compiler_params=_params(),
        name="attn_prompt",
    )(qa, *([ka] * n_kb), *([va] * n_kb), f_tab)


def _attn_sample_kernel(q_ref, kc_ref, vc_ref, kn_ref, vn_ref, f_ref, o_ref, bias_ref, *, pad):
    n_q, n_k = bias_ref.shape[1:]

    @pl.when(pl.program_id(0) == 0)
    def _():
        _expand_rel_bias(f_ref, bias_ref, n_q, n_k, 0, n_k - pad, band=False)

    zpad = jnp.zeros((pad, D_A), BF16)
    kall = jnp.concatenate([kc_ref[...].astype(BF16), kn_ref[...].astype(BF16), zpad], axis=0)
    vall = jnp.concatenate([vc_ref[...].astype(BF16), vn_ref[...].astype(BF16), zpad], axis=0)
    for p in range(N_HEADS_A // 2):
        sl = slice(p * LANES, (p + 1) * LANES)
        o = _head_pair_attention(q_ref[:, sl], kall[:, sl], vall[:, sl],
                                 bias_ref[2 * p], bias_ref[2 * p + 1])
        o_ref[:, sl] = o.astype(BF16)


def _attn_sample(qa, kn, vn, cache_k, cache_v, f_tab, *, seq, pad, q_row0):
    nb, w, _ = cache_k.shape
    t = nb * seq
    row = pl.BlockSpec((seq, D_A), lambda i: (i, 0))
    q_row = pl.BlockSpec((seq, D_A), lambda i: (i + q_row0 // seq, 0))
    cache = pl.BlockSpec((None, w, D_A), lambda i: (i, 0, 0))
    return pl.pallas_call(
        functools.partial(_attn_sample_kernel, pad=pad),
        grid=(nb,),
        in_specs=[q_row, cache, cache, row, row, _const_spec((N_HEADS_A, BIAS_F_LEN))],
        out_specs=row,
        out_shape=jax.ShapeDtypeStruct((t, D_A), BF16),
        scratch_shapes=[pltpu.VMEM((N_HEADS_A, seq, w + seq + pad), F32)],
        compiler_params=_params(),
        name="attn_sample",
    )(qa, cache_k, cache_v, kn, vn, f_tab)


def _gla_levels(c):
    out, hs = [], 1
    while hs < c:
        out.append(hs)
        hs *= 2
    return out


def _gla_masks(c):
    t = np.arange(c)
    masks = []
    for hs in _gla_levels(c):
        blk = t // (2 * hs)
        second = (t % (2 * hs)) >= hs
        masks.append((blk[:, None] == blk[None, :]) & second[:, None] & ~second[None, :])
    masks.append(np.eye(c, dtype=bool))
    return np.stack(masks).astype(np.float32)


def _split_row_bcast(x, hs, row):
    c, w = x.shape
    blk = 2 * hs
    if blk >= SUBLANES:
        xr = x.reshape(c // blk, blk, w)
        return jnp.broadcast_to(xr[:, hs - 1:hs, :], (c // blk, blk, w)).reshape(c, w)
    tiles = (c // SUBLANES, SUBLANES, w)
    x3, pos = x.reshape(tiles), (row & (blk - 1)).reshape(tiles)
    if hs == 1:
        out = jnp.where(pos == 1, pltpu.roll(x3, 1, 1), x3)
    else:
        assert hs == 2
        nxt = jnp.where((pos & 1) == 1, x3, pltpu.roll(x3, SUBLANES - 1, 1))
        out = jnp.where(pos >= 2, pltpu.roll(nxt, 2, 1), nxt)
    return out.reshape(c, w)


def _level_step(p, qf, kf, hs, row):
    c, w = p.shape
    if hs < SUBLANES:
        tot = _split_row_bcast(p, hs, row)
        second = (row & hs) != 0
        return jnp.where(second, p, tot - p), jnp.where(second, p + tot, p), jnp.where(second, qf, kf)
    xs, ps, qks = [], [], []
    for lo in range(0, c, 2 * hs):
        first, second = p[lo:lo + hs], p[lo + hs:lo + 2 * hs]
        tot = jnp.broadcast_to(first[hs - 1:hs], (hs, w))
        xs += [tot - first, second]
        ps += [first, second + tot]
        qks += [kf[lo:lo + hs], qf[lo + hs:lo + 2 * hs]]
    if hs % BF16_ROWS == 0:
        qk = jnp.concatenate(qks, axis=0)
    else:
        qk = jnp.where((row & hs) != 0, qf, kf)
    return jnp.concatenate(xs, axis=0), jnp.concatenate(ps, axis=0), qk


def _gla_chunk(rows, c, q_ref, k_ref, v_ref, la_ref, rs_ref, gn_ref, mask_ref, st_in, st_out, o_ref,
               *, state_t):
    levels = _gla_levels(c)
    n_lvl = len(levels)
    row = lax.broadcasted_iota(jnp.int32, (c, DK_HEAD_B), 0)
    for h in range(N_HEADS_B):
        hk = slice(h * DK_HEAD_B, (h + 1) * DK_HEAD_B)
        hv = slice(h * DV_HEAD_B, (h + 1) * DV_HEAD_B)
        qf = q_ref[rows, hk]
        kf = k_ref[rows, hk]
        vh = v_ref[rows, hv]
        p = la_ref[rows, hk]
        a = mask_ref[n_lvl] * _dot_nt(qf, kf)
        for l, hs in enumerate(levels):
            x, p, qk = _level_step(p, qf, kf, hs, row)
            z = qk * jnp.exp2(x).astype(BF16)
            a = a + mask_ref[l] * _dot_nt(z, z)
        b = p
        b_last = b[c - 1:c, :]
        o = _dot(a.astype(BF16), vh)
        st = st_in[h]
        qd = qf * jnp.exp2(b).astype(BF16)
        kd = kf * jnp.exp2(b_last - b).astype(BF16)
        d_last = jnp.exp2(b_last)
        if state_t:
            o = o + _dot_nt(qd, st.astype(BF16))
            st_out[h] = st * d_last + _dot_tn(vh, kd)
        else:
            o = o + _dot(qd, st.astype(BF16))
            d_col = jnp.broadcast_to(d_last, (SUBLANES, DK_HEAD_B)).T[:, 0:1]
            st_out[h] = st * d_col + _dot_tn(kd, vh)
        o = _rms(o) * gn_ref[...] * rs_ref[rows, hv].astype(F32)
        o_ref[rows, hv] = o.astype(BF16)


def _gla_stream_kernel(q_ref, k_ref, v_ref, la_ref, rs_ref, gn_ref, mask_ref, s0_ref,
                       o_ref, sout_ref, st_ref, *, c, n_chunks, unroll):
    @pl.when(pl.program_id(0) == 0)
    def _():
        for h in range(N_HEADS_B):
            st_ref[h] = s0_ref[h].T

    def chunk(ci, carry):
        rows = pl.ds(pl.multiple_of(ci * c, c), c)
        _gla_chunk(rows, c, q_ref, k_ref, v_ref, la_ref, rs_ref, gn_ref, mask_ref, st_ref, st_ref, o_ref,
                   state_t=True)
        return carry

    lax.fori_loop(0, n_chunks, chunk, 0, unroll=unroll)

    @pl.when(pl.program_id(0) == pl.num_programs(0) - 1)
    def _():
        for h in range(N_HEADS_B):
            sout_ref[h] = st_ref[h].T


def _gla_seqs_kernel(q_ref, k_ref, v_ref, la_ref, rs_ref, gn_ref, mask_ref, s0_ref, o_ref, sout_ref,
                     *, c, n_seqs):
    for j in range(n_seqs):
        _gla_chunk(slice(j * c, (j + 1) * c), c, q_ref, k_ref, v_ref, la_ref, rs_ref, gn_ref, mask_ref,
                   s0_ref.at[j], sout_ref.at[j], o_ref, state_t=False)


def _gla_stream(qb, kb, vb, la, rs, gnorm, s0, *, c, tb, t):
    masks = jnp.asarray(_gla_masks(c), F32)
    row = lambda w: pl.BlockSpec((tb, w), lambda i: (i, 0))
    st_shape = (N_HEADS_B, DK_HEAD_B, DV_HEAD_B)
    s_spec = pl.BlockSpec(st_shape, lambda i: (0, 0, 0))
    n_chunks = tb // c
    return pl.pallas_call(
        functools.partial(_gla_stream_kernel, c=c, n_chunks=n_chunks, unroll=min(GLA_UNROLL, n_chunks)),
        grid=(t // tb,),
        in_specs=[row(DK_B), row(DK_B), row(DV_B), row(DK_B), row(DV_B), _const_spec((1, DV_HEAD_B)),
                  _const_spec(masks.shape), s_spec],
        out_specs=(row(DV_B), s_spec),
        out_shape=(jax.ShapeDtypeStruct((t, DV_B), BF16), jax.ShapeDtypeStruct(st_shape, F32)),
        scratch_shapes=[pltpu.VMEM((N_HEADS_B, DV_HEAD_B, DK_HEAD_B), F32)],
        compiler_params=_params(),
        name="gla_stream",
    )(qb, kb, vb, la, rs, gnorm, masks, s0)


def _gla_seqs(qb, kb, vb, la, rs, gnorm, s0, *, c, row0):
    n = s0.shape[0]
    t = n * c
    tb = GLA_SEQS_PER_STEP * c
    assert n % GLA_SEQS_PER_STEP == 0 and row0 % tb == 0
    masks = jnp.asarray(_gla_masks(c), F32)
    in_row = lambda w: pl.BlockSpec((tb, w), lambda i: (i + row0 // tb, 0))
    row = lambda w: pl.BlockSpec((tb, w), lambda i: (i, 0))
    s_spec = pl.BlockSpec((GLA_SEQS_PER_STEP,) + s0.shape[1:], lambda i: (i, 0, 0, 0))
    return pl.pallas_call(
        functools.partial(_gla_seqs_kernel, c=c, n_seqs=GLA_SEQS_PER_STEP),
        grid=(n // GLA_SEQS_PER_STEP,),
        in_specs=[in_row(DK_B), in_row(DK_B), in_row(DV_B), in_row(DK_B), in_row(DV_B),
                  _const_spec((1, DV_HEAD_B)),
                  _const_spec(masks.shape), s_spec],
        out_specs=(row(DV_B), s_spec),
        out_shape=(jax.ShapeDtypeStruct((t, DV_B), BF16), jax.ShapeDtypeStruct(s0.shape, F32)),
        compiler_params=_params(),
        name="gla_seqs",
    )(qb, kb, vb, la, rs, gnorm, masks, s0)


def _merge_ffn_kernel(xp_ref, xs_ref, oap_ref, oas_ref, obp_ref, obs_ref, ga_ref, gb_ref,
                      gpost_ref, gpre_ref, gfpost_ref, wpa_ref, wpb_ref, wout_ref, wg_ref, wu_ref, wd_ref,
                      yp_ref, ys_ref, x1_ref, h_ref, acc_ref, *, n_prompt):
    is_prompt = pl.program_id(0) < n_prompt
    is_sample = pl.program_id(0) >= n_prompt

    def merge(x_ref, oa_ref, ob_ref):
        pa = _dot(oa_ref[...], wpa_ref[...])
        pb = _dot(ob_ref[...], wpb_ref[...])
        mix = jax.nn.sigmoid(ga_ref[...].astype(F32)) * pa + jax.nn.sigmoid(gb_ref[...].astype(F32)) * pb
        x1 = x_ref[...] + _rms(_dot(mix.astype(BF16), wout_ref[...])) * gpost_ref[...]
        x1_ref[...] = x1
        h_ref[...] = (_rms(x1) * gpre_ref[...]).astype(BF16)
        acc_ref[...] = jnp.zeros_like(acc_ref)

    pl.when(is_prompt)(functools.partial(merge, xp_ref, oap_ref, obp_ref))
    pl.when(is_sample)(functools.partial(merge, xs_ref, oas_ref, obs_ref))

    for j in range(N_FF_CHUNKS):
        cols = slice(j * FF_CHUNK, (j + 1) * FF_CHUNK)
        h = h_ref[...]
        g = _dot(h, wg_ref[:, cols])
        u = _dot(h, wu_ref[:, cols])
        acc_ref[...] += _dot((g * jax.nn.sigmoid(g) * u).astype(BF16), wd_ref[cols, :])
    def finish(y_ref):
        y_ref[...] = x1_ref[...] + _rms(acc_ref[...]) * gfpost_ref[...]

    pl.when(is_prompt)(functools.partial(finish, yp_ref))
    pl.when(is_sample)(functools.partial(finish, ys_ref))


def _merge_ffn(xp, xs, oap, oas, obp, obs, ga, gb, gpost, gpre, gfpost, wpa, wpb, wout, wg, wu, wd):
    tp, ts = xp.shape[0], xs.shape[0]
    n_p, n_s = tp // TM, ts // TM
    row = lambda w: pl.BlockSpec((TM, w), lambda i: (i, 0))
    p_blk = lambda w: pl.BlockSpec((TM, w), lambda i: (jnp.minimum(i, n_p - 1), 0))
    s_blk = lambda w: pl.BlockSpec((TM, w), lambda i: (jnp.maximum(i - n_p, 0), 0))
    s_in = lambda w: pl.BlockSpec((TM, w), lambda i: (jnp.maximum(i - n_p, 0), 0),
                                  pipeline_mode=pl.Buffered(1))
    vec = _const_spec((1, D_MODEL))
    return pl.pallas_call(
        functools.partial(_merge_ffn_kernel, n_prompt=n_p),
        grid=(n_p + n_s,),
        in_specs=[p_blk(D_MODEL), s_in(D_MODEL), p_blk(D_A), s_in(D_A), p_blk(DV_B), s_in(DV_B),
                  row(D_MODEL), row(D_MODEL), vec, vec, vec,
                  _const_spec(wpa.shape), _const_spec(wpb.shape), _const_spec(wout.shape),
                  _const_spec(wg.shape), _const_spec(wu.shape), _const_spec(wd.shape)],
        out_specs=(p_blk(D_MODEL), s_blk(D_MODEL)),
        out_shape=(jax.ShapeDtypeStruct((tp, D_MODEL), F32), jax.ShapeDtypeStruct((ts, D_MODEL), F32)),
        scratch_shapes=[pltpu.VMEM((TM, D_MODEL), F32), pltpu.VMEM((TM, D_MODEL), BF16),
                        pltpu.VMEM((TM, D_MODEL), F32)],
        compiler_params=_params(),
        name="merge_ffn",
    )(xp, xs, oap, oas, obp, obs, ga, gb, gpost, gpre, gfpost, wpa, wpb, wout, wg, wu, wd)


def _rel_bias_row(table):
    n_hi = BAND_PAST + BIAS_ORIGIN - REL_CLIP
    n_lo = BIAS_F_LEN - n_hi - (2 * REL_CLIP + 1)
    h = table.shape[0]
    return jnp.concatenate([jnp.broadcast_to(table[:, -1:], (h, n_hi)), table[:, ::-1],
                            jnp.broadcast_to(table[:, :1], (h, n_lo))], axis=1)


def kernel(x_prompt, x_sample, cache_attn_k, cache_attn_v, state_gla, norm_mix_pre, norm_mix_post, norm_ffn_pre, norm_ffn_post, w_in, w_decay_up, b_decay, rel_bias, gla_norm, w_proj_a, w_proj_b, w_out, w_ffn_gate, w_ffn_up, w_ffn_down):
    depth = w_in.shape[0]
    assert depth == 1, "single-layer step"
    batch, seq, _ = x_prompt.shape
    dec_batch, dec_seq, _ = x_sample.shape
    assert batch == 1 and seq % TM == 0 and (dec_batch * dec_seq) % TM == 0
    past = cache_attn_k.shape[2]

    w_bf = w_in[0].astype(BF16)
    wup_pad = jnp.concatenate([w_decay_up[0], jnp.zeros((DLR_PAD - GATE_RANK, DK_B), F32)], axis=0).astype(BF16)
    bdec = b_decay[0][None, :]
    vec = lambda a: a[0][None, :]
    wpa, wpb, wout = w_proj_a[0].astype(BF16), w_proj_b[0].astype(BF16), w_out[0].astype(BF16)
    wg, wu, wd = w_ffn_gate[0].astype(BF16), w_ffn_up[0].astype(BF16), w_ffn_down[0].astype(BF16)
    gnorm = gla_norm[0][None, :]

    xp = x_prompt[0]
    xs = x_sample.reshape(dec_batch * dec_seq, D_MODEL)
    qa, ka, va, qb, kb, vb, rs, la, ga, gb, kf, vf, kfs, vfs = _in_proj(
        xp, xs, vec(norm_mix_pre), w_bf, wup_pad, bdec)
    f_tab = _rel_bias_row(rel_bias[0])
    keep = min(BAND_PAST, seq)
    assert keep == TM

    oap = _attn_prompt(qa, ka, va, f_tab, t=seq)
    s0 = jnp.zeros((N_HEADS_B, DK_HEAD_B, DV_HEAD_B), F32)
    obp, sp = _gla_stream(qb, kb, vb, la, rs, gnorm, s0, c=GLA_CHUNK, tb=GLA_TB, t=seq)

    n_keys = past + dec_seq
    pad = (-n_keys) % LANES
    assert past == BAND_PAST
    oas = _attn_sample(qa, kfs, vfs, cache_attn_k[0].reshape(dec_batch, past, D_A),
                       cache_attn_v[0].reshape(dec_batch, past, D_A), f_tab,
                       seq=dec_seq, pad=pad, q_row0=seq)
    gla_chunk = CHUNK if dec_seq % CHUNK == 0 else dec_seq
    assert gla_chunk == dec_seq
    obs, ss = _gla_seqs(qb, kb, vb, la, rs, gnorm, state_gla[0], c=gla_chunk, row0=seq)

    yp, ys = _merge_ffn(xp, xs, oap, oas, obp, obs, ga, gb,
                        vec(norm_mix_post), vec(norm_ffn_pre), vec(norm_ffn_post), wpa, wpb, wout, wg, wu, wd)


    hd = (N_HEADS_A, HEAD_DIM_A)
    return (yp[None], ys.reshape(dec_batch, dec_seq, D_MODEL),
            kf.reshape((1, 1, keep) + hd), vf.reshape((1, 1, keep) + hd), sp[None, None],
            kfs.reshape((1, dec_batch, dec_seq) + hd), vfs.reshape((1, dec_batch, dec_seq) + hd), ss[None])
```

```python
import functools

import numpy as np
import jax
import jax.numpy as jnp
from jax import lax
from jax.experimental import pallas as pl
from jax.experimental.pallas import tpu as pltpu

F32 = jnp.float32
BF16 = jnp.bfloat16

D_MODEL = 1024
CHUNK = 64
BAND_CHUNKS = 8
BAND_PAST = BAND_CHUNKS * CHUNK
N_HEADS_A = 8
HEAD_DIM_A = 64
D_A = N_HEADS_A * HEAD_DIM_A
REL_CLIP = 128
N_HEADS_B = 4
DK_HEAD_B = 128
DV_HEAD_B = 256
DK_B = N_HEADS_B * DK_HEAD_B
DV_B = N_HEADS_B * DV_HEAD_B
GATE_RANK = 16
GATE_TEMP = 16.0
D_FF = 2816
EPS = 1e-6

LANES = 128
SUBLANES = 8
BF16_ROWS = 16
LOG2E = 1.4426950408889634
VMEM_LIMIT_BYTES = 56 * 1024 * 1024
NEG_BIG = -1e30

_PIECES = (("qa", D_A), ("ka", D_A), ("va", D_A), ("qb", DK_B), ("kb", DK_B), ("vb", DV_B),
           ("rb", DV_B), ("dlr", GATE_RANK), ("ga", D_MODEL), ("gb", D_MODEL))
_OFF = {}
_o = 0
for _n, _w in _PIECES:
    _OFF[_n] = (_o, _o + _w)
    _o += _w
D_IN = _o
DLR_PAD = LANES
assert _OFF["dlr"][0] % LANES == 0

TM = 512
ATT_QB = 256
ATT_WIN = ATT_QB + BAND_PAST
ATT_SUB = 4
ATT_MASKED = BAND_PAST // ATT_QB
assert ATT_MASKED <= ATT_SUB
BIAS_ORIGIN = ATT_QB
BIAS_F_LEN = 1024
GLA_TB = 512
GLA_CHUNK = 128
GLA_UNROLL = 2
GLA_SEQS_PER_STEP = 4
FF_CHUNK = 256
REALIGN_ROWS = 128
N_FF_CHUNKS = D_FF // FF_CHUNK


def _const_spec(shape):
    nd = len(shape)
    return pl.BlockSpec(shape, lambda i: (0,) * nd, pipeline_mode=pl.Buffered(1))


def _params():
    return pltpu.CompilerParams(dimension_semantics=("arbitrary",), vmem_limit_bytes=VMEM_LIMIT_BYTES)


def _rms(x):
    return x * lax.rsqrt(jnp.mean(x * x, axis=-1, keepdims=True) + EPS)


def _dot(a, b):
    return jnp.dot(a, b, preferred_element_type=F32)


def _dot_nt(a, b):
    return lax.dot_general(a, b, (((1,), (1,)), ((), ())), preferred_element_type=F32)


def _dot_tn(a, b):
    return lax.dot_general(a, b, (((0,), (0,)), ((), ())), preferred_element_type=F32)


def _in_proj_kernel(x_ref, g_ref, w_ref, wup_ref, bdec_ref,
                    qa_ref, ka_ref, va_ref, qb_ref, kb_ref, vb_ref, rs_ref, la_ref, ga_ref, gb_ref,
                    kf_ref, vf_ref, wgate_ref, *, kv_rows_every_step):
    dlr_lo = _OFF["dlr"][0]

    @pl.when(pl.program_id(0) == 0)
    def _():
        for r in range(0, D_MODEL, REALIGN_ROWS):
            tail = w_ref[r:r + REALIGN_ROWS, dlr_lo:]
            for j, name in enumerate(("ga", "gb")):
                lo, hi = _OFF[name]
                wgate_ref[j, r:r + REALIGN_ROWS, :] = tail[:, lo - dlr_lo:hi - dlr_lo]

    h = (_rms(x_ref[...]) * g_ref[...]).astype(BF16)

    def proj(name):
        lo, hi = _OFF[name]
        return _dot(h, w_ref[:, lo:hi])

    dlr = _dot(h, w_ref[:, dlr_lo:dlr_lo + DLR_PAD])
    z = _dot(dlr.astype(BF16), wup_ref[...]) + bdec_ref[...]
    la_ref[...] = (jnp.minimum(z, 0.0) - jnp.log1p(jnp.exp(-jnp.abs(z)))) * (LOG2E / GATE_TEMP)
    qa_ref[...] = (proj("qa") * (HEAD_DIM_A ** -0.5 * LOG2E)).astype(BF16)
    ka = proj("ka")
    va = proj("va")
    ka_ref[...] = ka.astype(BF16)
    va_ref[...] = va.astype(BF16)
    if kv_rows_every_step:
        kf_ref[...] = ka
        vf_ref[...] = va
    else:
        @pl.when(pl.program_id(0) == pl.num_programs(0) - 1)
        def _():
            kf_ref[...] = ka
            vf_ref[...] = va
    qb_ref[...] = (proj("qb") * (DK_HEAD_B ** -0.5)).astype(BF16)
    kb_ref[...] = proj("kb").astype(BF16)
    vb_ref[...] = proj("vb").astype(BF16)
    r = proj("rb")
    rs_ref[...] = (r * jax.nn.sigmoid(r)).astype(BF16)
    ga_ref[...] = _dot(h, wgate_ref[0]).astype(BF16)
    gb_ref[...] = _dot(h, wgate_ref[1]).astype(BF16)


def _in_proj(x, g, w_bf, wup_pad, bdec, *, kv_rows_every_step):
    t = x.shape[0]
    n = t // TM
    row = lambda w: pl.BlockSpec((TM, w), lambda i: (i, 0))
    if kv_rows_every_step:
        kv_rows, kv_spec = t, row(D_A)
    else:
        kv_rows, kv_spec = TM, pl.BlockSpec((TM, D_A), lambda i: (0, 0))
    widths = (D_A, D_A, D_A, DK_B, DK_B, DV_B, DV_B, DK_B, D_MODEL, D_MODEL)
    dtypes = (BF16,) * 7 + (F32, BF16, BF16)
    out_shape = tuple(jax.ShapeDtypeStruct((t, w), d) for w, d in zip(widths, dtypes))
    out_shape += (jax.ShapeDtypeStruct((kv_rows, D_A), F32),) * 2
    out_specs = tuple(row(w) for w in widths) + (kv_spec, kv_spec)
    return pl.pallas_call(
        functools.partial(_in_proj_kernel, kv_rows_every_step=kv_rows_every_step),
        grid=(n,),
        in_specs=[row(D_MODEL), _const_spec((1, D_MODEL)), _const_spec((D_MODEL, D_IN)),
                  _const_spec((DLR_PAD, DK_B)), _const_spec((1, DK_B))],
        out_specs=out_specs,
        out_shape=out_shape,
        scratch_shapes=[pltpu.VMEM((2, D_MODEL, D_MODEL), BF16)],
        compiler_params=_params(),
        name="in_proj",
    )(x, g, w_bf, wup_pad, bdec)


def _softmax_pv(s, v):
    m = jnp.max(s, axis=-1, keepdims=True)
    e = jnp.exp2(s - m).astype(BF16)
    pv = _dot(e, jnp.concatenate([v, jnp.ones_like(v)], axis=1))
    return pv[:, :LANES] / pv[:, LANES:LANES + 1]


def _head_pair_attention(q_pair, k_pair, v_pair, bias0, bias1):
    nq = q_pair.shape[0]
    lane = lax.broadcasted_iota(jnp.int32, (1, LANES), 1)
    first = lane < HEAD_DIM_A
    zero = jnp.zeros_like(q_pair)
    qs = jnp.concatenate([jnp.where(first, q_pair, zero), jnp.where(first, zero, q_pair)], axis=0)
    s = _dot_nt(qs, k_pair) + jnp.concatenate([bias0, bias1], axis=0)
    pv = _softmax_pv(s, v_pair)
    return jnp.where(first, pv[:nq], pv[nq:])


def _expand_rel_bias(f_ref, bias_ref, n_q, n_k, first_valid, n_valid, band):
    t = lax.broadcasted_iota(jnp.int32, (n_q, n_k), 0)
    k = lax.broadcasted_iota(jnp.int32, (n_q, n_k), 1)
    keep = (k < n_valid) & (k >= first_valid)
    if band:
        jq, jk = t // CHUNK, k // CHUNK
        keep = keep & (jk >= jq) & (jk <= jq + BAND_CHUNKS)
    for h in range(N_HEADS_A):
        rows = jnp.broadcast_to(f_ref[h:h + 1, :], (n_q, BIAS_F_LEN))
        rolled = pltpu.roll(rows, BIAS_F_LEN - BIAS_ORIGIN, 1, stride=1, stride_axis=0)
        bias_ref[h] = jnp.where(keep, rolled[:, :n_k] * LOG2E, NEG_BIG)


def _attn_prompt_kernel(q_ref, *refs):
    n_kb = ATT_SUB + ATT_WIN // ATT_QB - 1
    k_refs, v_refs = refs[:n_kb], refs[n_kb:2 * n_kb]
    f_ref, o_ref, bias_ref = refs[2 * n_kb:]
    i = pl.program_id(0)

    @pl.when(i == 0)
    def _():
        _expand_rel_bias(f_ref, bias_ref.at[0], ATT_QB, ATT_WIN, 0, ATT_WIN, band=True)
        for g in range(ATT_MASKED):
            _expand_rel_bias(f_ref, bias_ref.at[1 + g], ATT_QB, ATT_WIN, BAND_PAST - ATT_QB * g, ATT_WIN,
                             band=True)

    for sub in range(ATT_SUB):
        kwin = jnp.concatenate([r[...] for r in k_refs[sub:sub + ATT_WIN // ATT_QB]], axis=0)
        vwin = jnp.concatenate([r[...] for r in v_refs[sub:sub + ATT_WIN // ATT_QB]], axis=0)
        rows = slice(sub * ATT_QB, (sub + 1) * ATT_QB)
        which = jnp.where(i == 0, 1 + sub, 0) if sub < ATT_MASKED else 0
        for p in range(N_HEADS_A // 2):
            sl = slice(p * LANES, (p + 1) * LANES)
            o = _head_pair_attention(q_ref[rows, sl], kwin[:, sl], vwin[:, sl],
                                     bias_ref[which, 2 * p], bias_ref[which, 2 * p + 1])
            o_ref[rows, sl] = o.astype(BF16)


def _attn_prompt(qa, ka, va, f_tab):
    t = qa.shape[0]
    n = t // (ATT_SUB * ATT_QB)
    n_kb = ATT_SUB + ATT_WIN // ATT_QB - 1
    kblk = [pl.BlockSpec((ATT_QB, D_A), functools.partial(
        lambda j, i: (jnp.maximum(ATT_SUB * i + j - (n_kb - ATT_SUB), 0), 0), j)) for j in range(n_kb)]
    qblk = pl.BlockSpec((ATT_SUB * ATT_QB, D_A), lambda i: (i, 0))
    return pl.pallas_call(
        _attn_prompt_kernel,
        grid=(n,),
        in_specs=[qblk] + kblk + kblk + [_const_spec((N_HEADS_A, BIAS_F_LEN))],
        out_specs=qblk,
        out_shape=jax.ShapeDtypeStruct((t, D_A), BF16),
        scratch_shapes=[pltpu.VMEM((1 + ATT_MASKED, N_HEADS_A, ATT_QB, ATT_WIN), F32)],
        compiler_params=_params(),
        name="attn_prompt",
    )(qa, *([ka] * n_kb), *([va] * n_kb), f_tab)


def _attn_sample_kernel(q_ref, kc_ref, vc_ref, kn_ref, vn_ref, f_ref, o_ref, bias_ref, *, pad):
    n_q, n_k = bias_ref.shape[1:]

    @pl.when(pl.program_id(0) == 0)
    def _():
        _expand_rel_bias(f_ref, bias_ref, n_q, n_k, 0, n_k - pad, band=False)

    zpad = jnp.zeros((pad, D_A), BF16)
    kall = jnp.concatenate([kc_ref[...].astype(BF16), kn_ref[...].astype(BF16), zpad], axis=0)
    vall = jnp.concatenate([vc_ref[...].astype(BF16), vn_ref[...].astype(BF16), zpad], axis=0)
    for p in range(N_HEADS_A // 2):
        sl = slice(p * LANES, (p + 1) * LANES)
        o = _head_pair_attention(q_ref[:, sl], kall[:, sl], vall[:, sl],
                                 bias_ref[2 * p], bias_ref[2 * p + 1])
        o_ref[:, sl] = o.astype(BF16)


def _attn_sample(qa, kn, vn, cache_k, cache_v, f_tab, *, seq, pad):
    t = qa.shape[0]
    nb, w, _ = cache_k.shape
    row = pl.BlockSpec((seq, D_A), lambda i: (i, 0))
    cache = pl.BlockSpec((None, w, D_A), lambda i: (i, 0, 0))
    return pl.pallas_call(
        functools.partial(_attn_sample_kernel, pad=pad),
        grid=(nb,),
        in_specs=[row, cache, cache, row, row, _const_spec((N_HEADS_A, BIAS_F_LEN))],
        out_specs=row,
        out_shape=jax.ShapeDtypeStruct((t, D_A), BF16),
        scratch_shapes=[pltpu.VMEM((N_HEADS_A, seq, w + seq + pad), F32)],
        compiler_params=_params(),
        name="attn_sample",
    )(qa, cache_k, cache_v, kn, vn, f_tab)


def _gla_levels(c):
    out, hs = [], 1
    while hs < c:
        out.append(hs)
        hs *= 2
    return out


def _gla_masks(c):
    t = np.arange(c)
    masks = []
    for hs in _gla_levels(c):
        blk = t // (2 * hs)
        second = (t % (2 * hs)) >= hs
        masks.append((blk[:, None] == blk[None, :]) & second[:, None] & ~second[None, :])
    masks.append(np.eye(c, dtype=bool))
    return np.stack(masks).astype(np.float32)


def _split_row_bcast(x, hs, row):
    c, w = x.shape
    blk = 2 * hs
    if blk >= SUBLANES:
        xr = x.reshape(c // blk, blk, w)
        return jnp.broadcast_to(xr[:, hs - 1:hs, :], (c // blk, blk, w)).reshape(c, w)
    tiles = (c // SUBLANES, SUBLANES, w)
    x3, pos = x.reshape(tiles), (row & (blk - 1)).reshape(tiles)
    if hs == 1:
        out = jnp.where(pos == 1, pltpu.roll(x3, 1, 1), x3)
    else:
        assert hs == 2
        nxt = jnp.where((pos & 1) == 1, x3, pltpu.roll(x3, SUBLANES - 1, 1))
        out = jnp.where(pos >= 2, pltpu.roll(nxt, 2, 1), nxt)
    return out.reshape(c, w)


def _level_step(p, qf, kf, hs, row):
    c, w = p.shape
    if hs < SUBLANES:
        tot = _split_row_bcast(p, hs, row)
        second = (row & hs) != 0
        return jnp.where(second, p, tot - p), jnp.where(second, p + tot, p), jnp.where(second, qf, kf)
    xs, ps, qks = [], [], []
    for lo in range(0, c, 2 * hs):
        first, second = p[lo:lo + hs], p[lo + hs:lo + 2 * hs]
        tot = jnp.broadcast_to(first[hs - 1:hs], (hs, w))
        xs += [tot - first, second]
        ps += [first, second + tot]
        qks += [kf[lo:lo + hs], qf[lo + hs:lo + 2 * hs]]
    if hs % BF16_ROWS == 0:
        qk = jnp.concatenate(qks, axis=0)
    else:
        qk = jnp.where((row & hs) != 0, qf, kf)
    return jnp.concatenate(xs, axis=0), jnp.concatenate(ps, axis=0), qk


def _gla_chunk(rows, c, q_ref, k_ref, v_ref, la_ref, rs_ref, gn_ref, mask_ref, st_in, st_out, o_ref,
               *, state_t):
    levels = _gla_levels(c)
    n_lvl = len(levels)
    row = lax.broadcasted_iota(jnp.int32, (c, DK_HEAD_B), 0)
    for h in range(N_HEADS_B):
        hk = slice(h * DK_HEAD_B, (h + 1) * DK_HEAD_B)
        hv = slice(h * DV_HEAD_B, (h + 1) * DV_HEAD_B)
        qf = q_ref[rows, hk]
        kf = k_ref[rows, hk]
        vh = v_ref[rows, hv]
        p = la_ref[rows, hk]
        a = mask_ref[n_lvl] * _dot_nt(qf, kf)
        for l, hs in enumerate(levels):
            x, p, qk = _level_step(p, qf, kf, hs, row)
            z = qk * jnp.exp2(x).astype(BF16)
            a = a + mask_ref[l] * _dot_nt(z, z)
        b = p
        b_last = b[c - 1:c, :]
        o = _dot(a.astype(BF16), vh)
        st = st_in[h]
        qd = qf * jnp.exp2(b).astype(BF16)
        kd = kf * jnp.exp2(b_last - b).astype(BF16)
        d_last = jnp.exp2(b_last)
        if state_t:
            o = o + _dot_nt(qd, st.astype(BF16))
            st_out[h] = st * d_last + _dot_tn(vh, kd)
        else:
            o = o + _dot(qd, st.astype(BF16))
            d_col = jnp.broadcast_to(d_last, (SUBLANES, DK_HEAD_B)).T[:, 0:1]
            st_out[h] = st * d_col + _dot_tn(kd, vh)
        o = _rms(o) * gn_ref[...] * rs_ref[rows, hv].astype(F32)
        o_ref[rows, hv] = o.astype(BF16)


def _gla_stream_kernel(q_ref, k_ref, v_ref, la_ref, rs_ref, gn_ref, mask_ref, s0_ref,
                       o_ref, sout_ref, st_ref, *, c, n_chunks, unroll):
    @pl.when(pl.program_id(0) == 0)
    def _():
        for h in range(N_HEADS_B):
            st_ref[h] = s0_ref[h].T

    def chunk(ci, carry):
        rows = pl.ds(pl.multiple_of(ci * c, c), c)
        _gla_chunk(rows, c, q_ref, k_ref, v_ref, la_ref, rs_ref, gn_ref, mask_ref, st_ref, st_ref, o_ref,
                   state_t=True)
        return carry

    lax.fori_loop(0, n_chunks, chunk, 0, unroll=unroll)

    @pl.when(pl.program_id(0) == pl.num_programs(0) - 1)
    def _():
        for h in range(N_HEADS_B):
            sout_ref[h] = st_ref[h].T


def _gla_seqs_kernel(q_ref, k_ref, v_ref, la_ref, rs_ref, gn_ref, mask_ref, s0_ref, o_ref, sout_ref,
                     *, c, n_seqs):
    for j in range(n_seqs):
        _gla_chunk(slice(j * c, (j + 1) * c), c, q_ref, k_ref, v_ref, la_ref, rs_ref, gn_ref, mask_ref,
                   s0_ref.at[j], sout_ref.at[j], o_ref, state_t=False)


def _gla_stream(qb, kb, vb, la, rs, gnorm, s0, *, c, tb):
    t = qb.shape[0]
    masks = jnp.asarray(_gla_masks(c), F32)
    row = lambda w: pl.BlockSpec((tb, w), lambda i: (i, 0))
    st_shape = (N_HEADS_B, DK_HEAD_B, DV_HEAD_B)
    s_spec = pl.BlockSpec(st_shape, lambda i: (0, 0, 0))
    n_chunks = tb // c
    return pl.pallas_call(
        functools.partial(_gla_stream_kernel, c=c, n_chunks=n_chunks, unroll=min(GLA_UNROLL, n_chunks)),
        grid=(t // tb,),
        in_specs=[row(DK_B), row(DK_B), row(DV_B), row(DK_B), row(DV_B), _const_spec((1, DV_HEAD_B)),
                  _const_spec(masks.shape), s_spec],
        out_specs=(row(DV_B), s_spec),
        out_shape=(jax.ShapeDtypeStruct((t, DV_B), BF16), jax.ShapeDtypeStruct(st_shape, F32)),
        scratch_shapes=[pltpu.VMEM((N_HEADS_B, DV_HEAD_B, DK_HEAD_B), F32)],
        compiler_params=_params(),
        name="gla_stream",
    )(qb, kb, vb, la, rs, gnorm, masks, s0)


def _gla_seqs(qb, kb, vb, la, rs, gnorm, s0, *, c):
    t = qb.shape[0]
    n = s0.shape[0]
    assert t == n * c and n % GLA_SEQS_PER_STEP == 0
    masks = jnp.asarray(_gla_masks(c), F32)
    tb = GLA_SEQS_PER_STEP * c
    row = lambda w: pl.BlockSpec((tb, w), lambda i: (i, 0))
    s_spec = pl.BlockSpec((GLA_SEQS_PER_STEP,) + s0.shape[1:], lambda i: (i, 0, 0, 0))
    return pl.pallas_call(
        functools.partial(_gla_seqs_kernel, c=c, n_seqs=GLA_SEQS_PER_STEP),
        grid=(n // GLA_SEQS_PER_STEP,),
        in_specs=[row(DK_B), row(DK_B), row(DV_B), row(DK_B), row(DV_B), _const_spec((1, DV_HEAD_B)),
                  _const_spec(masks.shape), s_spec],
        out_specs=(row(DV_B), s_spec),
        out_shape=(jax.ShapeDtypeStruct((t, DV_B), BF16), jax.ShapeDtypeStruct(s0.shape, F32)),
        compiler_params=_params(),
        name="gla_seqs",
    )(qb, kb, vb, la, rs, gnorm, masks, s0)


def _merge_ffn_kernel(x_ref, oa_ref, ob_ref, ga_ref, gb_ref, gpost_ref, gpre_ref, gfpost_ref,
                      wpa_ref, wpb_ref, wout_ref, wg_ref, wu_ref, wd_ref, y_ref, x1_ref, h_ref, acc_ref):
    pa = _dot(oa_ref[...], wpa_ref[...])
    pb = _dot(ob_ref[...], wpb_ref[...])
    mix = jax.nn.sigmoid(ga_ref[...].astype(F32)) * pa + jax.nn.sigmoid(gb_ref[...].astype(F32)) * pb
    x1 = x_ref[...] + _rms(_dot(mix.astype(BF16), wout_ref[...])) * gpost_ref[...]
    x1_ref[...] = x1
    h_ref[...] = (_rms(x1) * gpre_ref[...]).astype(BF16)
    acc_ref[...] = jnp.zeros_like(acc_ref)

    for j in range(N_FF_CHUNKS):
        cols = slice(j * FF_CHUNK, (j + 1) * FF_CHUNK)
        h = h_ref[...]
        g = _dot(h, wg_ref[:, cols])
        u = _dot(h, wu_ref[:, cols])
        acc_ref[...] += _dot((g * jax.nn.sigmoid(g) * u).astype(BF16), wd_ref[cols, :])
    y_ref[...] = x1_ref[...] + _rms(acc_ref[...]) * gfpost_ref[...]


def _merge_ffn(x, oa, ob, ga, gb, gpost, gpre, gfpost, wpa, wpb, wout, wg, wu, wd):
    t = x.shape[0]
    n = t // TM
    row = lambda w: pl.BlockSpec((TM, w), lambda i: (i, 0))
    vec = _const_spec((1, D_MODEL))
    return pl.pallas_call(
        _merge_ffn_kernel,
        grid=(n,),
        in_specs=[row(D_MODEL), row(D_A), row(DV_B), row(D_MODEL), row(D_MODEL), vec, vec, vec,
                  _const_spec(wpa.shape), _const_spec(wpb.shape), _const_spec(wout.shape),
                  _const_spec(wg.shape), _const_spec(wu.shape), _const_spec(wd.shape)],
        out_specs=row(D_MODEL),
        out_shape=jax.ShapeDtypeStruct((t, D_MODEL), F32),
        scratch_shapes=[pltpu.VMEM((TM, D_MODEL), F32), pltpu.VMEM((TM, D_MODEL), BF16),
                        pltpu.VMEM((TM, D_MODEL), F32)],
        compiler_params=_params(),
        name="merge_ffn",
    )(x, oa, ob, ga, gb, gpost, gpre, gfpost, wpa, wpb, wout, wg, wu, wd)


def _rel_bias_row(table):
    n_hi = BAND_PAST + BIAS_ORIGIN - REL_CLIP
    n_lo = BIAS_F_LEN - n_hi - (2 * REL_CLIP + 1)
    h = table.shape[0]
    return jnp.concatenate([jnp.broadcast_to(table[:, -1:], (h, n_hi)), table[:, ::-1],
                            jnp.broadcast_to(table[:, :1], (h, n_lo))], axis=1)


def kernel(x_prompt, x_sample, cache_attn_k, cache_attn_v, state_gla, norm_mix_pre, norm_mix_post, norm_ffn_pre, norm_ffn_post, w_in, w_decay_up, b_decay, rel_bias, gla_norm, w_proj_a, w_proj_b, w_out, w_ffn_gate, w_ffn_up, w_ffn_down):
    depth = w_in.shape[0]
    assert depth == 1, "single-layer step"
    batch, seq, _ = x_prompt.shape
    dec_batch, dec_seq, _ = x_sample.shape
    assert batch == 1 and seq % TM == 0 and (dec_batch * dec_seq) % TM == 0
    assert seq % (ATT_SUB * ATT_QB) == 0
    past = cache_attn_k.shape[2]

    w_bf = w_in[0].astype(BF16)
    wup_pad = jnp.concatenate([w_decay_up[0], jnp.zeros((DLR_PAD - GATE_RANK, DK_B), F32)], axis=0).astype(BF16)
    bdec = b_decay[0][None, :]
    vec = lambda a: a[0][None, :]
    wpa, wpb, wout = w_proj_a[0].astype(BF16), w_proj_b[0].astype(BF16), w_out[0].astype(BF16)
    wg, wu, wd = w_ffn_gate[0].astype(BF16), w_ffn_up[0].astype(BF16), w_ffn_down[0].astype(BF16)
    gnorm = gla_norm[0][None, :]

    def layer_tail(x, oa, ob, ga, gb):
        return _merge_ffn(x, oa, ob, ga, gb, vec(norm_mix_post), vec(norm_ffn_pre), vec(norm_ffn_post),
                          wpa, wpb, wout, wg, wu, wd)

    xp = x_prompt[0]
    qa, ka, va, qb, kb, vb, rs, la, ga, gb, kf, vf = _in_proj(
        xp, vec(norm_mix_pre), w_bf, wup_pad, bdec, kv_rows_every_step=False)
    f_tab = _rel_bias_row(rel_bias[0])
    oa = _attn_prompt(qa, ka, va, f_tab)
    s0 = jnp.zeros((N_HEADS_B, DK_HEAD_B, DV_HEAD_B), F32)
    ob, sp = _gla_stream(qb, kb, vb, la, rs, gnorm, s0, c=GLA_CHUNK, tb=GLA_TB)
    yp = layer_tail(xp, oa, ob, ga, gb)
    keep = min(BAND_PAST, seq)
    assert keep == TM

    xs = x_sample.reshape(dec_batch * dec_seq, D_MODEL)
    qa, ka, va, qb, kb, vb, rs, la, ga, gb, kfs, vfs = _in_proj(
        xs, vec(norm_mix_pre), w_bf, wup_pad, bdec, kv_rows_every_step=True)
    n_keys = past + dec_seq
    pad = (-n_keys) % LANES
    assert past == BAND_PAST
    oa = _attn_sample(qa, kfs, vfs, cache_attn_k[0].reshape(dec_batch, past, D_A),
                      cache_attn_v[0].reshape(dec_batch, past, D_A), f_tab, seq=dec_seq, pad=pad)
    gla_chunk = CHUNK if dec_seq % CHUNK == 0 else dec_seq
    assert gla_chunk == dec_seq
    ob, ss = _gla_seqs(qb, kb, vb, la, rs, gnorm, state_gla[0], c=gla_chunk)
    ys = layer_tail(xs, oa, ob, ga, gb)

    hd = (N_HEADS_A, HEAD_DIM_A)
    return (yp[None], ys.reshape(dec_batch, dec_seq, D_MODEL),
            kf.reshape((1, 1, keep) + hd), vf.reshape((1, 1, keep) + hd), sp[None, None],
            kfs.reshape((1, dec_batch, dec_seq) + hd), vfs.reshape((1, dec_batch, dec_seq) + hd), ss[None])
```

```python
import functools

import numpy as np
import jax
import jax.numpy as jnp
from jax import lax
from jax.experimental import pallas as pl
from jax.experimental.pallas import tpu as pltpu

F32 = jnp.float32
BF16 = jnp.bfloat16

D_MODEL = 1024
CHUNK = 64
BAND_CHUNKS = 8
BAND_PAST = BAND_CHUNKS * CHUNK
N_HEADS_A = 8
HEAD_DIM_A = 64
D_A = N_HEADS_A * HEAD_DIM_A
REL_CLIP = 128
N_HEADS_B = 4
DK_HEAD_B = 128
DV_HEAD_B = 256
DK_B = N_HEADS_B * DK_HEAD_B
DV_B = N_HEADS_B * DV_HEAD_B
GATE_RANK = 16
GATE_TEMP = 16.0
D_FF = 2816
EPS = 1e-6

LANES = 128
SUBLANES = 8
BF16_ROWS = 16
LOG2E = 1.4426950408889634
VMEM_LIMIT_BYTES = 56 * 1024 * 1024
NEG_BIG = -1e30

_PIECES = (("qa", D_A), ("ka", D_A), ("va", D_A), ("qb", DK_B), ("kb", DK_B), ("vb", DV_B),
           ("rb", DV_B), ("dlr", GATE_RANK), ("ga", D_MODEL), ("gb", D_MODEL))
_OFF = {}
_o = 0
for _n, _w in _PIECES:
    _OFF[_n] = (_o, _o + _w)
    _o += _w
D_IN = _o
DLR_PAD = LANES
assert _OFF["dlr"][0] % LANES == 0

TM = 512
ATT_QB = 256
ATT_WIN = ATT_QB + BAND_PAST
ATT_SUB = 4
ATT_MASKED = BAND_PAST // ATT_QB
assert ATT_MASKED <= ATT_SUB
BIAS_ORIGIN = ATT_QB
BIAS_F_LEN = 1024
GLA_TB = 512
GLA_CHUNK = 128
GLA_UNROLL = 2
GLA_SEQS_PER_STEP = 4
FF_CHUNK = 256
REALIGN_ROWS = 128
N_FF_CHUNKS = D_FF // FF_CHUNK


def _const_spec(shape):
    nd = len(shape)
    return pl.BlockSpec(shape, lambda i: (0,) * nd, pipeline_mode=pl.Buffered(1))


def _params():
    return pltpu.CompilerParams(dimension_semantics=("arbitrary",), vmem_limit_bytes=VMEM_LIMIT_BYTES)


def _rms(x):
    return x * lax.rsqrt(jnp.mean(x * x, axis=-1, keepdims=True) + EPS)


def _dot(a, b):
    return jnp.dot(a, b, preferred_element_type=F32)


def _dot_nt(a, b):
    return lax.dot_general(a, b, (((1,), (1,)), ((), ())), preferred_element_type=F32)


def _dot_tn(a, b):
    return lax.dot_general(a, b, (((0,), (0,)), ((), ())), preferred_element_type=F32)


def _in_proj_kernel(x_ref, g_ref, w_ref, wup_ref, bdec_ref,
                    qa_ref, ka_ref, va_ref, qb_ref, kb_ref, vb_ref, rs_ref, la_ref, ga_ref, gb_ref,
                    kf_ref, vf_ref, wgate_ref, *, kv_rows_every_step):
    dlr_lo = _OFF["dlr"][0]

    @pl.when(pl.program_id(0) == 0)
    def _():
        for r in range(0, D_MODEL, REALIGN_ROWS):
            tail = w_ref[r:r + REALIGN_ROWS, dlr_lo:]
            for j, name in enumerate(("ga", "gb")):
                lo, hi = _OFF[name]
                wgate_ref[j, r:r + REALIGN_ROWS, :] = tail[:, lo - dlr_lo:hi - dlr_lo]

    h = (_rms(x_ref[...]) * g_ref[...]).astype(BF16)

    def proj(name):
        lo, hi = _OFF[name]
        return _dot(h, w_ref[:, lo:hi])

    dlr = _dot(h, w_ref[:, dlr_lo:dlr_lo + DLR_PAD])
    z = _dot(dlr.astype(BF16), wup_ref[...]) + bdec_ref[...]
    la_ref[...] = (jnp.minimum(z, 0.0) - jnp.log1p(jnp.exp(-jnp.abs(z)))) * (LOG2E / GATE_TEMP)
    qa_ref[...] = (proj("qa") * (HEAD_DIM_A ** -0.5 * LOG2E)).astype(BF16)
    ka = proj("ka")
    va = proj("va")
    ka_ref[...] = ka.astype(BF16)
    va_ref[...] = va.astype(BF16)
    if kv_rows_every_step:
        kf_ref[...] = ka
        vf_ref[...] = va
    else:
        @pl.when(pl.program_id(0) == pl.num_programs(0) - 1)
        def _():
            kf_ref[...] = ka
            vf_ref[...] = va
    qb_ref[...] = (proj("qb") * (DK_HEAD_B ** -0.5)).astype(BF16)
    kb_ref[...] = proj("kb").astype(BF16)
    vb_ref[...] = proj("vb").astype(BF16)
    r = proj("rb")
    rs_ref[...] = (r * jax.nn.sigmoid(r)).astype(BF16)
    ga_ref[...] = _dot(h, wgate_ref[0]).astype(BF16)
    gb_ref[...] = _dot(h, wgate_ref[1]).astype(BF16)


def _in_proj(x, g, w_bf, wup_pad, bdec, *, kv_rows_every_step):
    t = x.shape[0]
    n = t // TM
    row = lambda w: pl.BlockSpec((TM, w), lambda i: (i, 0))
    if kv_rows_every_step:
        kv_rows, kv_spec = t, row(D_A)
    else:
        kv_rows, kv_spec = TM, pl.BlockSpec((TM, D_A), lambda i: (0, 0))
    widths = (D_A, D_A, D_A, DK_B, DK_B, DV_B, DV_B, DK_B, D_MODEL, D_MODEL)
    dtypes = (BF16,) * 7 + (F32, BF16, BF16)
    out_shape = tuple(jax.ShapeDtypeStruct((t, w), d) for w, d in zip(widths, dtypes))
    out_shape += (jax.ShapeDtypeStruct((kv_rows, D_A), F32),) * 2
    out_specs = tuple(row(w) for w in widths) + (kv_spec, kv_spec)
    return pl.pallas_call(
        functools.partial(_in_proj_kernel, kv_rows_every_step=kv_rows_every_step),
        grid=(n,),
        in_specs=[row(D_MODEL), _const_spec((1, D_MODEL)), _const_spec((D_MODEL, D_IN)),
                  _const_spec((DLR_PAD, DK_B)), _const_spec((1, DK_B))],
        out_specs=out_specs,
        out_shape=out_shape,
        scratch_shapes=[pltpu.VMEM((2, D_MODEL, D_MODEL), BF16)],
        compiler_params=_params(),
        name="in_proj",
    )(x, g, w_bf, wup_pad, bdec)


def _softmax_pv(s, v):
    m = jnp.max(s, axis=-1, keepdims=True)
    e = jnp.exp2(s - m).astype(BF16)
    pv = _dot(e, jnp.concatenate([v, jnp.ones_like(v)], axis=1))
    return pv[:, :LANES] / pv[:, LANES:LANES + 1]


def _head_pair_attention(q_pair, k_pair, v_pair, bias0, bias1):
    nq = q_pair.shape[0]
    lane = lax.broadcasted_iota(jnp.int32, (1, LANES), 1)
    first = lane < HEAD_DIM_A
    zero = jnp.zeros_like(q_pair)
    qs = jnp.concatenate([jnp.where(first, q_pair, zero), jnp.where(first, zero, q_pair)], axis=0)
    s = _dot_nt(qs, k_pair) + jnp.concatenate([bias0, bias1], axis=0)
    pv = _softmax_pv(s, v_pair)
    return jnp.where(first, pv[:nq], pv[nq:])


def _expand_rel_bias(f_ref, bias_ref, n_q, n_k, first_valid, n_valid, band):
    t = lax.broadcasted_iota(jnp.int32, (n_q, n_k), 0)
    k = lax.broadcasted_iota(jnp.int32, (n_q, n_k), 1)
    keep = (k < n_valid) & (k >= first_valid)
    if band:
        jq, jk = t // CHUNK, k // CHUNK
        keep = keep & (jk >= jq) & (jk <= jq + BAND_CHUNKS)
    for h in range(N_HEADS_A):
        rows = jnp.broadcast_to(f_ref[h:h + 1, :], (n_q, BIAS_F_LEN))
        rolled = pltpu.roll(rows, BIAS_F_LEN - BIAS_ORIGIN, 1, stride=1, stride_axis=0)
        bias_ref[h] = jnp.where(keep, rolled[:, :n_k] * LOG2E, NEG_BIG)


def _attn_prompt_kernel(q_ref, *refs):
    n_kb = ATT_SUB + ATT_WIN // ATT_QB - 1
    k_refs, v_refs = refs[:n_kb], refs[n_kb:2 * n_kb]
    f_ref, o_ref, bias_ref = refs[2 * n_kb:]
    i = pl.program_id(0)

    @pl.when(i == 0)
    def _():
        _expand_rel_bias(f_ref, bias_ref.at[0], ATT_QB, ATT_WIN, 0, ATT_WIN, band=True)
        for g in range(ATT_MASKED):
            _expand_rel_bias(f_ref, bias_ref.at[1 + g], ATT_QB, ATT_WIN, BAND_PAST - ATT_QB * g, ATT_WIN,
                             band=True)

    for sub in range(ATT_SUB):
        kwin = jnp.concatenate([r[...] for r in k_refs[sub:sub + ATT_WIN // ATT_QB]], axis=0)
        vwin = jnp.concatenate([r[...] for r in v_refs[sub:sub + ATT_WIN // ATT_QB]], axis=0)
        rows = slice(sub * ATT_QB, (sub + 1) * ATT_QB)
        which = jnp.where(i == 0, 1 + sub, 0) if sub < ATT_MASKED else 0
        for p in range(N_HEADS_A // 2):
            sl = slice(p * LANES, (p + 1) * LANES)
            o = _head_pair_attention(q_ref[rows, sl], kwin[:, sl], vwin[:, sl],
                                     bias_ref[which, 2 * p], bias_ref[which, 2 * p + 1])
            o_ref[rows, sl] = o.astype(BF16)


def _attn_prompt(qa, ka, va, f_tab):
    t = qa.shape[0]
    n = t // (ATT_SUB * ATT_QB)
    n_kb = ATT_SUB + ATT_WIN // ATT_QB - 1
    kblk = [pl.BlockSpec((ATT_QB, D_A), functools.partial(
        lambda j, i: (jnp.maximum(ATT_SUB * i + j - (n_kb - ATT_SUB), 0), 0), j)) for j in range(n_kb)]
    qblk = pl.BlockSpec((ATT_SUB * ATT_QB, D_A), lambda i: (i, 0))
    return pl.pallas_call(
        _attn_prompt_kernel,
        grid=(n,),
        in_specs=[qblk] + kblk + kblk + [_const_spec((N_HEADS_A, BIAS_F_LEN))],
        out_specs=qblk,
        out_shape=jax.ShapeDtypeStruct((t, D_A), BF16),
        scratch_shapes=[pltpu.VMEM((1 + ATT_MASKED, N_HEADS_A, ATT_QB, ATT_WIN), F32)],
        compiler_params=_params(),
        name="attn_prompt",
    )(qa, *([ka] * n_kb), *([va] * n_kb), f_tab)


def _attn_sample_kernel(q_ref, kc_ref, vc_ref, kn_ref, vn_ref, f_ref, o_ref, bias_ref, *, pad):
    n_q, n_k = bias_ref.shape[1:]

    @pl.when(pl.program_id(0) == 0)
    def _():
        _expand_rel_bias(f_ref, bias_ref, n_q, n_k, 0, n_k - pad, band=False)

    zpad = jnp.zeros((pad, D_A), BF16)
    kc = pltpu.einshape("phd->p(hd)", kc_ref[...]).astype(BF16)
    vc = pltpu.einshape("phd->p(hd)", vc_ref[...]).astype(BF16)
    kall = jnp.concatenate([kc, kn_ref[...].astype(BF16), zpad], axis=0)
    vall = jnp.concatenate([vc, vn_ref[...].astype(BF16), zpad], axis=0)
    for p in range(N_HEADS_A // 2):
        sl = slice(p * LANES, (p + 1) * LANES)
        o = _head_pair_attention(q_ref[:, sl], kall[:, sl], vall[:, sl],
                                 bias_ref[2 * p], bias_ref[2 * p + 1])
        o_ref[:, sl] = o.astype(BF16)


def _attn_sample(qa, kn, vn, cache_k, cache_v, f_tab, *, seq, pad):
    t = qa.shape[0]
    nb, w = cache_k.shape[:2]
    row = pl.BlockSpec((seq, D_A), lambda i: (i, 0))
    cache = pl.BlockSpec((None, w, N_HEADS_A, HEAD_DIM_A), lambda i: (i, 0, 0, 0))
    return pl.pallas_call(
        functools.partial(_attn_sample_kernel, pad=pad),
        grid=(nb,),
        in_specs=[row, cache, cache, row, row, _const_spec((N_HEADS_A, BIAS_F_LEN))],
        out_specs=row,
        out_shape=jax.ShapeDtypeStruct((t, D_A), BF16),
        scratch_shapes=[pltpu.VMEM((N_HEADS_A, seq, w + seq + pad), F32)],
        compiler_params=_params(),
        name="attn_sample",
    )(qa, cache_k, cache_v, kn, vn, f_tab)


def _gla_levels(c):
    out, hs = [], 1
    while hs < c:
        out.append(hs)
        hs *= 2
    return out


def _gla_masks(c):
    t = np.arange(c)
    masks = []
    for hs in _gla_levels(c):
        blk = t // (2 * hs)
        second = (t % (2 * hs)) >= hs
        masks.append((blk[:, None] == blk[None, :]) & second[:, None] & ~second[None, :])
    masks.append(np.eye(c, dtype=bool))
    return np.stack(masks).astype(np.float32)


def _split_row_bcast(x, hs, row):
    c, w = x.shape
    blk = 2 * hs
    if blk >= SUBLANES:
        xr = x.reshape(c // blk, blk, w)
        return jnp.broadcast_to(xr[:, hs - 1:hs, :], (c // blk, blk, w)).reshape(c, w)
    tiles = (c // SUBLANES, SUBLANES, w)
    x3, pos = x.reshape(tiles), (row & (blk - 1)).reshape(tiles)
    if hs == 1:
        out = jnp.where(pos == 1, pltpu.roll(x3, 1, 1), x3)
    else:
        assert hs == 2
        nxt = jnp.where((pos & 1) == 1, x3, pltpu.roll(x3, SUBLANES - 1, 1))
        out = jnp.where(pos >= 2, pltpu.roll(nxt, 2, 1), nxt)
    return out.reshape(c, w)


def _level_step(p, qf, kf, hs, row):
    c, w = p.shape
    if hs < SUBLANES:
        tot = _split_row_bcast(p, hs, row)
        second = (row & hs) != 0
        return jnp.where(second, p, tot - p), jnp.where(second, p + tot, p), jnp.where(second, qf, kf)
    xs, ps, qks = [], [], []
    for lo in range(0, c, 2 * hs):
        first, second = p[lo:lo + hs], p[lo + hs:lo + 2 * hs]
        tot = jnp.broadcast_to(first[hs - 1:hs], (hs, w))
        xs += [tot - first, second]
        ps += [first, second + tot]
        qks += [kf[lo:lo + hs], qf[lo + hs:lo + 2 * hs]]
    if hs % BF16_ROWS == 0:
        qk = jnp.concatenate(qks, axis=0)
    else:
        qk = jnp.where((row & hs) != 0, qf, kf)
    return jnp.concatenate(xs, axis=0), jnp.concatenate(ps, axis=0), qk


def _gla_chunk(rows, c, q_ref, k_ref, v_ref, la_ref, mask_ref, st_in, st_out, o_ref, *, state_t):
    levels = _gla_levels(c)
    n_lvl = len(levels)
    row = lax.broadcasted_iota(jnp.int32, (c, DK_HEAD_B), 0)
    for h in range(N_HEADS_B):
        hk = slice(h * DK_HEAD_B, (h + 1) * DK_HEAD_B)
        hv = slice(h * DV_HEAD_B, (h + 1) * DV_HEAD_B)
        qf = q_ref[rows, hk]
        kf = k_ref[rows, hk]
        vh = v_ref[rows, hv]
        p = la_ref[rows, hk]
        a = mask_ref[n_lvl] * _dot_nt(qf, kf)
        for l, hs in enumerate(levels):
            x, p, qk = _level_step(p, qf, kf, hs, row)
            z = qk * jnp.exp2(x).astype(BF16)
            a = a + mask_ref[l] * _dot_nt(z, z)
        b = p
        b_last = b[c - 1:c, :]
        o = _dot(a.astype(BF16), vh)
        st = st_in[h]
        qd = qf * jnp.exp2(b).astype(BF16)
        kd = kf * jnp.exp2(b_last - b).astype(BF16)
        d_last = jnp.exp2(b_last)
        if state_t:
            o = o + _dot_nt(qd, st.astype(BF16))
            st_out[h] = st * d_last + _dot_tn(vh, kd)
        else:
            o = o + _dot(qd, st.astype(BF16))
            d_col = jnp.broadcast_to(d_last, (SUBLANES, DK_HEAD_B)).T[:, 0:1]
            st_out[h] = st * d_col + _dot_tn(kd, vh)
        o_ref[rows, hv] = o.astype(BF16)


def _gla_stream_kernel(q_ref, k_ref, v_ref, la_ref, mask_ref, s0_ref,
                       o_ref, sout_ref, st_ref, *, c, n_chunks, unroll):
    @pl.when(pl.program_id(0) == 0)
    def _():
        for h in range(N_HEADS_B):
            st_ref[h] = s0_ref[h].T

    def chunk(ci, carry):
        rows = pl.ds(pl.multiple_of(ci * c, c), c)
        _gla_chunk(rows, c, q_ref, k_ref, v_ref, la_ref, mask_ref, st_ref, st_ref, o_ref, state_t=True)
        return carry

    lax.fori_loop(0, n_chunks, chunk, 0, unroll=unroll)

    @pl.when(pl.program_id(0) == pl.num_programs(0) - 1)
    def _():
        for h in range(N_HEADS_B):
            sout_ref[h] = st_ref[h].T


def _gla_seqs_kernel(q_ref, k_ref, v_ref, la_ref, mask_ref, s0_ref, o_ref, sout_ref, *, c, n_seqs):
    for j in range(n_seqs):
        _gla_chunk(slice(j * c, (j + 1) * c), c, q_ref, k_ref, v_ref, la_ref, mask_ref,
                   s0_ref.at[j], sout_ref.at[j], o_ref, state_t=False)


def _gla_stream(qb, kb, vb, la, s0, *, c, tb):
    t = qb.shape[0]
    masks = jnp.asarray(_gla_masks(c), F32)
    row = lambda w: pl.BlockSpec((tb, w), lambda i: (i, 0))
    st_shape = (N_HEADS_B, DK_HEAD_B, DV_HEAD_B)
    s_spec = pl.BlockSpec(st_shape, lambda i: (0, 0, 0))
    n_chunks = tb // c
    return pl.pallas_call(
        functools.partial(_gla_stream_kernel, c=c, n_chunks=n_chunks, unroll=min(GLA_UNROLL, n_chunks)),
        grid=(t // tb,),
        in_specs=[row(DK_B), row(DK_B), row(DV_B), row(DK_B), _const_spec(masks.shape), s_spec],
        out_specs=(row(DV_B), s_spec),
        out_shape=(jax.ShapeDtypeStruct((t, DV_B), BF16), jax.ShapeDtypeStruct(st_shape, F32)),
        scratch_shapes=[pltpu.VMEM((N_HEADS_B, DV_HEAD_B, DK_HEAD_B), F32)],
        compiler_params=_params(),
        name="gla_stream",
    )(qb, kb, vb, la, masks, s0)


def _gla_seqs(qb, kb, vb, la, s0, *, c):
    t = qb.shape[0]
    n = s0.shape[0]
    assert t == n * c and n % GLA_SEQS_PER_STEP == 0
    masks = jnp.asarray(_gla_masks(c), F32)
    tb = GLA_SEQS_PER_STEP * c
    row = lambda w: pl.BlockSpec((tb, w), lambda i: (i, 0))
    s_spec = pl.BlockSpec((GLA_SEQS_PER_STEP,) + s0.shape[1:], lambda i: (i, 0, 0, 0))
    return pl.pallas_call(
        functools.partial(_gla_seqs_kernel, c=c, n_seqs=GLA_SEQS_PER_STEP),
        grid=(n // GLA_SEQS_PER_STEP,),
        in_specs=[row(DK_B), row(DK_B), row(DV_B), row(DK_B), _const_spec(masks.shape), s_spec],
        out_specs=(row(DV_B), s_spec),
        out_shape=(jax.ShapeDtypeStruct((t, DV_B), BF16), jax.ShapeDtypeStruct(s0.shape, F32)),
        compiler_params=_params(),
        name="gla_seqs",
    )(qb, kb, vb, la, masks, s0)


def _merge_ffn_kernel(x_ref, oa_ref, ob_ref, rs_ref, ga_ref, gb_ref, gn_ref, gpost_ref, gpre_ref, gfpost_ref,
                      wpa_ref, wpb_ref, wout_ref, wg_ref, wu_ref, wd_ref, y_ref, x1_ref, h_ref, acc_ref):
    pa = _dot(oa_ref[...], wpa_ref[...])
    ob = jnp.concatenate(
        [_rms(ob_ref[:, h * DV_HEAD_B:(h + 1) * DV_HEAD_B].astype(F32)) * gn_ref[...] for h in range(N_HEADS_B)],
        axis=1) * rs_ref[...].astype(F32)
    pb = _dot(ob.astype(BF16), wpb_ref[...])
    mix = jax.nn.sigmoid(ga_ref[...].astype(F32)) * pa + jax.nn.sigmoid(gb_ref[...].astype(F32)) * pb
    x1 = x_ref[...] + _rms(_dot(mix.astype(BF16), wout_ref[...])) * gpost_ref[...]
    x1_ref[...] = x1
    h_ref[...] = (_rms(x1) * gpre_ref[...]).astype(BF16)
    acc_ref[...] = jnp.zeros_like(acc_ref)

    for j in range(N_FF_CHUNKS):
        cols = slice(j * FF_CHUNK, (j + 1) * FF_CHUNK)
        h = h_ref[...]
        g = _dot(h, wg_ref[:, cols])
        u = _dot(h, wu_ref[:, cols])
        acc_ref[...] += _dot((g * jax.nn.sigmoid(g) * u).astype(BF16), wd_ref[cols, :])
    y_ref[...] = x1_ref[...] + _rms(acc_ref[...]) * gfpost_ref[...]


def _merge_ffn(x, oa, ob, rs, ga, gb, gnorm, gpost, gpre, gfpost, wpa, wpb, wout, wg, wu, wd):
    t = x.shape[0]
    n = t // TM
    row = lambda w: pl.BlockSpec((TM, w), lambda i: (i, 0))
    vec = _const_spec((1, D_MODEL))
    return pl.pallas_call(
        _merge_ffn_kernel,
        grid=(n,),
        in_specs=[row(D_MODEL), row(D_A), row(DV_B), row(DV_B), row(D_MODEL), row(D_MODEL),
                  _const_spec((1, DV_HEAD_B)), vec, vec, vec,
                  _const_spec(wpa.shape), _const_spec(wpb.shape), _const_spec(wout.shape),
                  _const_spec(wg.shape), _const_spec(wu.shape), _const_spec(wd.shape)],
        out_specs=row(D_MODEL),
        out_shape=jax.ShapeDtypeStruct((t, D_MODEL), F32),
        scratch_shapes=[pltpu.VMEM((TM, D_MODEL), F32), pltpu.VMEM((TM, D_MODEL), BF16),
                        pltpu.VMEM((TM, D_MODEL), F32)],
        compiler_params=_params(),
        name="merge_ffn",
    )(x, oa, ob, rs, ga, gb, gnorm, gpost, gpre, gfpost, wpa, wpb, wout, wg, wu, wd)


def _rel_bias_row(table):
    n_hi = BAND_PAST + BIAS_ORIGIN - REL_CLIP
    n_lo = BIAS_F_LEN - n_hi - (2 * REL_CLIP + 1)
    h = table.shape[0]
    return jnp.concatenate([jnp.broadcast_to(table[:, -1:], (h, n_hi)), table[:, ::-1],
                            jnp.broadcast_to(table[:, :1], (h, n_lo))], axis=1)


def kernel(x_prompt, x_sample, cache_attn_k, cache_attn_v, state_gla, norm_mix_pre, norm_mix_post, norm_ffn_pre, norm_ffn_post, w_in, w_decay_up, b_decay, rel_bias, gla_norm, w_proj_a, w_proj_b, w_out, w_ffn_gate, w_ffn_up, w_ffn_down):
    depth = w_in.shape[0]
    assert depth == 1, "single-layer step"
    batch, seq, _ = x_prompt.shape
    dec_batch, dec_seq, _ = x_sample.shape
    assert batch == 1 and seq % TM == 0 and (dec_batch * dec_seq) % TM == 0
    assert seq % (ATT_SUB * ATT_QB) == 0
    past = cache_attn_k.shape[2]

    w_bf = w_in[0].astype(BF16)
    wup_pad = jnp.concatenate([w_decay_up[0], jnp.zeros((DLR_PAD - GATE_RANK, DK_B), F32)], axis=0).astype(BF16)
    bdec = b_decay[0][None, :]
    vec = lambda a: a[0][None, :]
    wpa, wpb, wout = w_proj_a[0].astype(BF16), w_proj_b[0].astype(BF16), w_out[0].astype(BF16)
    wg, wu, wd = w_ffn_gate[0].astype(BF16), w_ffn_up[0].astype(BF16), w_ffn_down[0].astype(BF16)
    gnorm = gla_norm[0][None, :]

    def layer_tail(x, oa, ob, rs, ga, gb):
        return _merge_ffn(x, oa, ob, rs, ga, gb, gnorm, vec(norm_mix_post), vec(norm_ffn_pre), vec(norm_ffn_post),
                          wpa, wpb, wout, wg, wu, wd)

    xp = x_prompt[0]
    qa, ka, va, qb, kb, vb, rs, la, ga, gb, kf, vf = _in_proj(
        xp, vec(norm_mix_pre), w_bf, wup_pad, bdec, kv_rows_every_step=False)
    f_tab = _rel_bias_row(rel_bias[0])
    oa = _attn_prompt(qa, ka, va, f_tab)
    s0 = jnp.zeros((N_HEADS_B, DK_HEAD_B, DV_HEAD_B), F32)
    ob, sp = _gla_stream(qb, kb, vb, la, s0, c=GLA_CHUNK, tb=GLA_TB)
    yp = layer_tail(xp, oa, ob, rs, ga, gb)
    keep = min(BAND_PAST, seq)
    assert keep == TM

    xs = x_sample.reshape(dec_batch * dec_seq, D_MODEL)
    qa, ka, va, qb, kb, vb, rs, la, ga, gb, kfs, vfs = _in_proj(
        xs, vec(norm_mix_pre), w_bf, wup_pad, bdec, kv_rows_every_step=True)
    n_keys = past + dec_seq
    pad = (-n_keys) % LANES
    assert past == BAND_PAST
    oa = _attn_sample(qa, kfs, vfs, cache_attn_k[0], cache_attn_v[0], f_tab, seq=dec_seq, pad=pad)
    gla_chunk = CHUNK if dec_seq % CHUNK == 0 else dec_seq
    assert gla_chunk == dec_seq
    ob, ss = _gla_seqs(qb, kb, vb, la, state_gla[0], c=gla_chunk)
    ys = layer_tail(xs, oa, ob, rs, ga, gb)

    hd = (N_HEADS_A, HEAD_DIM_A)
    return (yp[None], ys.reshape(dec_batch, dec_seq, D_MODEL),
            kf.reshape((1, 1, keep) + hd), vf.reshape((1, 1, keep) + hd), sp[None, None],
            kfs.reshape((1, dec_batch, dec_seq) + hd), vfs.reshape((1, dec_batch, dec_seq) + hd), ss[None])
```

```python
import functools

import numpy as np
import jax
import jax.numpy as jnp
from jax import lax
from jax.experimental import pallas as pl
from jax.experimental.pallas import tpu as pltpu

F32 = jnp.float32
BF16 = jnp.bfloat16

D_MODEL = 1024
CHUNK = 64
BAND_CHUNKS = 8
BAND_PAST = BAND_CHUNKS * CHUNK
N_HEADS_A = 8
HEAD_DIM_A = 64
D_A = N_HEADS_A * HEAD_DIM_A
REL_CLIP = 128
N_HEADS_B = 4
DK_HEAD_B = 128
DV_HEAD_B = 256
DK_B = N_HEADS_B * DK_HEAD_B
DV_B = N_HEADS_B * DV_HEAD_B
GATE_RANK = 16
GATE_TEMP = 16.0
D_FF = 2816
EPS = 1e-6

LANES = 128
SUBLANES = 8
BF16_ROWS = 16
LOG2E = 1.4426950408889634
VMEM_LIMIT_BYTES = 56 * 1024 * 1024
NEG_BIG = -1e30

_PIECES = (("qa", D_A), ("ka", D_A), ("va", D_A), ("qb", DK_B), ("kb", DK_B), ("vb", DV_B),
           ("rb", DV_B), ("dlr", GATE_RANK), ("ga", D_MODEL), ("gb", D_MODEL))
_OFF = {}
_o = 0
for _n, _w in _PIECES:
    _OFF[_n] = (_o, _o + _w)
    _o += _w
D_IN = _o
DLR_PAD = LANES
assert _OFF["dlr"][0] % LANES == 0

TM = 512
ATT_QB = 256
ATT_WIN = ATT_QB + BAND_PAST
ATT_SUB = 4
ATT_MASKED = BAND_PAST // ATT_QB
assert ATT_MASKED <= ATT_SUB
BIAS_ORIGIN = ATT_QB
BIAS_F_LEN = 1024
GLA_TB = 512
GLA_CHUNK = 128
GLA_UNROLL = 2
GLA_SEQS_PER_STEP = 4
FF_CHUNK = 256
REALIGN_ROWS = 128
N_FF_CHUNKS = D_FF // FF_CHUNK


def _const_spec(shape):
    nd = len(shape)
    return pl.BlockSpec(shape, lambda i: (0,) * nd, pipeline_mode=pl.Buffered(1))


def _params():
    return pltpu.CompilerParams(dimension_semantics=("arbitrary",), vmem_limit_bytes=VMEM_LIMIT_BYTES)


def _rms(x):
    return x * lax.rsqrt(jnp.mean(x * x, axis=-1, keepdims=True) + EPS)


def _dot(a, b):
    return jnp.dot(a, b, preferred_element_type=F32)


def _dot_nt(a, b):
    return lax.dot_general(a, b, (((1,), (1,)), ((), ())), preferred_element_type=F32)


def _dot_tn(a, b):
    return lax.dot_general(a, b, (((0,), (0,)), ((), ())), preferred_element_type=F32)


def _in_proj_kernel(x_ref, g_ref, w_ref, wup_ref, bdec_ref,
                    qa_ref, ka_ref, va_ref, qb_ref, kb_ref, vb_ref, rs_ref, la_ref, ga_ref, gb_ref,
                    kf_ref, vf_ref, wgate_ref, *, kv_rows_every_step):
    dlr_lo = _OFF["dlr"][0]

    @pl.when(pl.program_id(0) == 0)
    def _():
        for r in range(0, D_MODEL, REALIGN_ROWS):
            tail = w_ref[r:r + REALIGN_ROWS, dlr_lo:]
            for j, name in enumerate(("ga", "gb")):
                lo, hi = _OFF[name]
                wgate_ref[j, r:r + REALIGN_ROWS, :] = tail[:, lo - dlr_lo:hi - dlr_lo]

    h = (_rms(x_ref[...]) * g_ref[...]).astype(BF16)

    def proj(name):
        lo, hi = _OFF[name]
        return _dot(h, w_ref[:, lo:hi])

    dlr = _dot(h, w_ref[:, dlr_lo:dlr_lo + DLR_PAD])
    z = _dot(dlr.astype(BF16), wup_ref[...]) + bdec_ref[...]
    la_ref[...] = (jnp.minimum(z, 0.0) - jnp.log1p(jnp.exp(-jnp.abs(z)))) * (LOG2E / GATE_TEMP)
    qa_ref[...] = (proj("qa") * (HEAD_DIM_A ** -0.5 * LOG2E)).astype(BF16)
    ka = proj("ka")
    va = proj("va")
    ka_ref[...] = ka.astype(BF16)
    va_ref[...] = va.astype(BF16)
    if kv_rows_every_step:
        kf_ref[...] = ka
        vf_ref[...] = va
    else:
        @pl.when(pl.program_id(0) == pl.num_programs(0) - 1)
        def _():
            kf_ref[...] = ka
            vf_ref[...] = va
    qb_ref[...] = (proj("qb") * (DK_HEAD_B ** -0.5)).astype(BF16)
    kb_ref[...] = proj("kb").astype(BF16)
    vb_ref[...] = proj("vb").astype(BF16)
    r = proj("rb")
    rs_ref[...] = (r * jax.nn.sigmoid(r)).astype(BF16)
    ga_ref[...] = _dot(h, wgate_ref[0]).astype(BF16)
    gb_ref[...] = _dot(h, wgate_ref[1]).astype(BF16)


def _in_proj(x, g, w_bf, wup_pad, bdec, *, kv_rows_every_step):
    t = x.shape[0]
    n = t // TM
    row = lambda w: pl.BlockSpec((TM, w), lambda i: (i, 0))
    if kv_rows_every_step:
        kv_rows, kv_spec = t, row(D_A)
    else:
        kv_rows, kv_spec = TM, pl.BlockSpec((TM, D_A), lambda i: (0, 0))
    widths = (D_A, D_A, D_A, DK_B, DK_B, DV_B, DV_B, DK_B, D_MODEL, D_MODEL)
    dtypes = (BF16,) * 7 + (F32, BF16, BF16)
    out_shape = tuple(jax.ShapeDtypeStruct((t, w), d) for w, d in zip(widths, dtypes))
    out_shape += (jax.ShapeDtypeStruct((kv_rows, D_A), F32),) * 2
    out_specs = tuple(row(w) for w in widths) + (kv_spec, kv_spec)
    return pl.pallas_call(
        functools.partial(_in_proj_kernel, kv_rows_every_step=kv_rows_every_step),
        grid=(n,),
        in_specs=[row(D_MODEL), _const_spec((1, D_MODEL)), _const_spec((D_MODEL, D_IN)),
                  _const_spec((DLR_PAD, DK_B)), _const_spec((1, DK_B))],
        out_specs=out_specs,
        out_shape=out_shape,
        scratch_shapes=[pltpu.VMEM((2, D_MODEL, D_MODEL), BF16)],
        compiler_params=_params(),
        name="in_proj",
    )(x, g, w_bf, wup_pad, bdec)


def _softmax_pv(s, v):
    m = jnp.max(s, axis=-1, keepdims=True)
    e = jnp.exp2(s - m).astype(BF16)
    pv = _dot(e, jnp.concatenate([v, jnp.ones_like(v)], axis=1))
    return pv[:, :LANES] / pv[:, LANES:LANES + 1]


def _head_pair_attention(q_pair, k_pair, v_pair, bias0, bias1):
    nq = q_pair.shape[0]
    lane = lax.broadcasted_iota(jnp.int32, (1, LANES), 1)
    first = lane < HEAD_DIM_A
    zero = jnp.zeros_like(q_pair)
    qs = jnp.concatenate([jnp.where(first, q_pair, zero), jnp.where(first, zero, q_pair)], axis=0)
    s = _dot_nt(qs, k_pair) + jnp.concatenate([bias0, bias1], axis=0)
    pv = _softmax_pv(s, v_pair)
    return jnp.where(first, pv[:nq], pv[nq:])


def _expand_rel_bias(f_ref, bias_ref, n_q, n_k, first_valid, n_valid, band):
    t = lax.broadcasted_iota(jnp.int32, (n_q, n_k), 0)
    k = lax.broadcasted_iota(jnp.int32, (n_q, n_k), 1)
    keep = (k < n_valid) & (k >= first_valid)
    if band:
        jq, jk = t // CHUNK, k // CHUNK
        keep = keep & (jk >= jq) & (jk <= jq + BAND_CHUNKS)
    for h in range(N_HEADS_A):
        rows = jnp.broadcast_to(f_ref[h:h + 1, :], (n_q, BIAS_F_LEN))
        rolled = pltpu.roll(rows, BIAS_F_LEN - BIAS_ORIGIN, 1, stride=1, stride_axis=0)
        bias_ref[h] = jnp.where(keep, rolled[:, :n_k] * LOG2E, NEG_BIG)


def _attn_prompt_kernel(q_ref, *refs):
    n_kb = ATT_SUB + ATT_WIN // ATT_QB - 1
    k_refs, v_refs = refs[:n_kb], refs[n_kb:2 * n_kb]
    f_ref, o_ref, bias_ref = refs[2 * n_kb:]
    i = pl.program_id(0)

    @pl.when(i == 0)
    def _():
        _expand_rel_bias(f_ref, bias_ref.at[0], ATT_QB, ATT_WIN, 0, ATT_WIN, band=True)
        for g in range(ATT_MASKED):
            _expand_rel_bias(f_ref, bias_ref.at[1 + g], ATT_QB, ATT_WIN, BAND_PAST - ATT_QB * g, ATT_WIN,
                             band=True)

    for sub in range(ATT_SUB):
        kwin = jnp.concatenate([r[...] for r in k_refs[sub:sub + ATT_WIN // ATT_QB]], axis=0)
        vwin = jnp.concatenate([r[...] for r in v_refs[sub:sub + ATT_WIN // ATT_QB]], axis=0)
        rows = slice(sub * ATT_QB, (sub + 1) * ATT_QB)
        which = jnp.where(i == 0, 1 + sub, 0) if sub < ATT_MASKED else 0
        for p in range(N_HEADS_A // 2):
            sl = slice(p * LANES, (p + 1) * LANES)
            o = _head_pair_attention(q_ref[rows, sl], kwin[:, sl], vwin[:, sl],
                                     bias_ref[which, 2 * p], bias_ref[which, 2 * p + 1])
            o_ref[rows, sl] = o.astype(BF16)


def _attn_prompt(qa, ka, va, f_tab):
    t = qa.shape[0]
    n = t // (ATT_SUB * ATT_QB)
    n_kb = ATT_SUB + ATT_WIN // ATT_QB - 1
    kblk = [pl.BlockSpec((ATT_QB, D_A), functools.partial(
        lambda j, i: (jnp.maximum(ATT_SUB * i + j - (n_kb - ATT_SUB), 0), 0), j)) for j in range(n_kb)]
    qblk = pl.BlockSpec((ATT_SUB * ATT_QB, D_A), lambda i: (i, 0))
    return pl.pallas_call(
        _attn_prompt_kernel,
        grid=(n,),
        in_specs=[qblk] + kblk + kblk + [_const_spec((N_HEADS_A, BIAS_F_LEN))],
        out_specs=qblk,
        out_shape=jax.ShapeDtypeStruct((t, D_A), BF16),
        scratch_shapes=[pltpu.VMEM((1 + ATT_MASKED, N_HEADS_A, ATT_QB, ATT_WIN), F32)],
        compiler_params=_params(),
        name="attn_prompt",
    )(qa, *([ka] * n_kb), *([va] * n_kb), f_tab)


def _attn_sample_kernel(q_ref, kt_ref, vt_ref, kn_ref, vn_ref, f_ref, o_ref, bias_ref):
    n_q, w = q_ref.shape[0], kt_ref.shape[-1]

    @pl.when(pl.program_id(0) == 0)
    def _():
        _expand_rel_bias(f_ref, bias_ref, n_q, bias_ref.shape[-1], 0, w + n_q, band=False)

    lane = lax.broadcasted_iota(jnp.int32, (1, LANES), 1)
    first = lane < HEAD_DIM_A
    for p in range(N_HEADS_A // 2):
        sl = slice(p * LANES, (p + 1) * LANES)
        q_pair = q_ref[:, sl]
        zero = jnp.zeros_like(q_pair)
        qs = jnp.concatenate([jnp.where(first, q_pair, zero), jnp.where(first, zero, q_pair)], axis=0)
        kt = jnp.concatenate([kt_ref[2 * p], kt_ref[2 * p + 1]], axis=0).astype(BF16)
        vt = jnp.concatenate([vt_ref[2 * p], vt_ref[2 * p + 1]], axis=0).astype(BF16)
        kn = kn_ref[:, sl].astype(BF16)
        vn = vn_ref[:, sl].astype(BF16)
        bias = jnp.concatenate([bias_ref[2 * p], bias_ref[2 * p + 1]], axis=0)
        s_c = _dot(qs, kt) + bias[:, :w]
        s_n = _dot_nt(qs, kn) + bias[:, w:w + n_q]
        m = jnp.maximum(jnp.max(s_c, axis=-1, keepdims=True), jnp.max(s_n, axis=-1, keepdims=True))
        e_c = jnp.exp2(s_c - m).astype(BF16)
        e_n = jnp.exp2(s_n - m).astype(BF16)
        pv = (_dot_nt(e_c, jnp.concatenate([vt, jnp.ones_like(vt)], axis=0))
              + _dot(e_n, jnp.concatenate([vn, jnp.ones_like(vn)], axis=1)))
        pv = pv[:, :LANES] / pv[:, LANES:LANES + 1]
        o_ref[:, sl] = jnp.where(first, pv[:n_q], pv[n_q:]).astype(BF16)


def _attn_sample(qa, kn, vn, cache_kt, cache_vt, f_tab, *, seq):
    t = qa.shape[0]
    nb, _, _, w = cache_kt.shape
    n_k = w + seq
    row = pl.BlockSpec((seq, D_A), lambda i: (i, 0))
    cache = pl.BlockSpec((None, N_HEADS_A, HEAD_DIM_A, w), lambda i: (i, 0, 0, 0))
    return pl.pallas_call(
        _attn_sample_kernel,
        grid=(nb,),
        in_specs=[row, cache, cache, row, row, _const_spec((N_HEADS_A, BIAS_F_LEN))],
        out_specs=row,
        out_shape=jax.ShapeDtypeStruct((t, D_A), BF16),
        scratch_shapes=[pltpu.VMEM((N_HEADS_A, seq, n_k + (-n_k) % LANES), F32)],
        compiler_params=_params(),
        name="attn_sample",
    )(qa, cache_kt, cache_vt, kn, vn, f_tab)


def _gla_levels(c):
    out, hs = [], 1
    while hs < c:
        out.append(hs)
        hs *= 2
    return out


def _gla_masks(c):
    t = np.arange(c)
    masks = []
    for hs in _gla_levels(c):
        blk = t // (2 * hs)
        second = (t % (2 * hs)) >= hs
        masks.append((blk[:, None] == blk[None, :]) & second[:, None] & ~second[None, :])
    masks.append(np.eye(c, dtype=bool))
    return np.stack(masks).astype(np.float32)


def _split_row_bcast(x, hs, row):
    c, w = x.shape
    blk = 2 * hs
    if blk >= SUBLANES:
        xr = x.reshape(c // blk, blk, w)
        return jnp.broadcast_to(xr[:, hs - 1:hs, :], (c // blk, blk, w)).reshape(c, w)
    tiles = (c // SUBLANES, SUBLANES, w)
    x3, pos = x.reshape(tiles), (row & (blk - 1)).reshape(tiles)
    if hs == 1:
        out = jnp.where(pos == 1, pltpu.roll(x3, 1, 1), x3)
    else:
        assert hs == 2
        nxt = jnp.where((pos & 1) == 1, x3, pltpu.roll(x3, SUBLANES - 1, 1))
        out = jnp.where(pos >= 2, pltpu.roll(nxt, 2, 1), nxt)
    return out.reshape(c, w)


def _level_step(p, qf, kf, hs, row):
    c, w = p.shape
    if hs < SUBLANES:
        tot = _split_row_bcast(p, hs, row)
        second = (row & hs) != 0
        return jnp.where(second, p, tot - p), jnp.where(second, p + tot, p), jnp.where(second, qf, kf)
    xs, ps, qks = [], [], []
    for lo in range(0, c, 2 * hs):
        first, second = p[lo:lo + hs], p[lo + hs:lo + 2 * hs]
        tot = jnp.broadcast_to(first[hs - 1:hs], (hs, w))
        xs += [tot - first, second]
        ps += [first, second + tot]
        qks += [kf[lo:lo + hs], qf[lo + hs:lo + 2 * hs]]
    if hs % BF16_ROWS == 0:
        qk = jnp.concatenate(qks, axis=0)
    else:
        qk = jnp.where((row & hs) != 0, qf, kf)
    return jnp.concatenate(xs, axis=0), jnp.concatenate(ps, axis=0), qk


def _gla_chunk(rows, c, q_ref, k_ref, v_ref, la_ref, mask_ref, st_in, st_out, o_ref, *, state_t):
    levels = _gla_levels(c)
    n_lvl = len(levels)
    row = lax.broadcasted_iota(jnp.int32, (c, DK_HEAD_B), 0)
    for h in range(N_HEADS_B):
        hk = slice(h * DK_HEAD_B, (h + 1) * DK_HEAD_B)
        hv = slice(h * DV_HEAD_B, (h + 1) * DV_HEAD_B)
        qf = q_ref[rows, hk]
        kf = k_ref[rows, hk]
        vh = v_ref[rows, hv]
        p = la_ref[rows, hk]
        a = mask_ref[n_lvl] * _dot_nt(qf, kf)
        for l, hs in enumerate(levels):
            x, p, qk = _level_step(p, qf, kf, hs, row)
            z = qk * jnp.exp2(x).astype(BF16)
            a = a + mask_ref[l] * _dot_nt(z, z)
        b = p
        b_last = b[c - 1:c, :]
        o = _dot(a.astype(BF16), vh)
        st = st_in[h]
        qd = qf * jnp.exp2(b).astype(BF16)
        kd = kf * jnp.exp2(b_last - b).astype(BF16)
        d_last = jnp.exp2(b_last)
        if state_t:
            o = o + _dot_nt(qd, st.astype(BF16))
            st_out[h] = st * d_last + _dot_tn(vh, kd)
        else:
            o = o + _dot(qd, st.astype(BF16))
            d_col = jnp.broadcast_to(d_last, (SUBLANES, DK_HEAD_B)).T[:, 0:1]
            st_out[h] = st * d_col + _dot_tn(kd, vh)
        o_ref[rows, hv] = o.astype(BF16)


def _gla_stream_kernel(q_ref, k_ref, v_ref, la_ref, mask_ref, s0_ref,
                       o_ref, sout_ref, st_ref, *, c, n_chunks, unroll):
    @pl.when(pl.program_id(0) == 0)
    def _():
        for h in range(N_HEADS_B):
            st_ref[h] = s0_ref[h].T

    def chunk(ci, carry):
        rows = pl.ds(pl.multiple_of(ci * c, c), c)
        _gla_chunk(rows, c, q_ref, k_ref, v_ref, la_ref, mask_ref, st_ref, st_ref, o_ref, state_t=True)
        return carry

    lax.fori_loop(0, n_chunks, chunk, 0, unroll=unroll)

    @pl.when(pl.program_id(0) == pl.num_programs(0) - 1)
    def _():
        for h in range(N_HEADS_B):
            sout_ref[h] = st_ref[h].T


def _gla_seqs_kernel(q_ref, k_ref, v_ref, la_ref, mask_ref, s0_ref, o_ref, sout_ref, *, c, n_seqs):
    for j in range(n_seqs):
        _gla_chunk(slice(j * c, (j + 1) * c), c, q_ref, k_ref, v_ref, la_ref, mask_ref,
                   s0_ref.at[j], sout_ref.at[j], o_ref, state_t=False)


def _gla_stream(qb, kb, vb, la, s0, *, c, tb):
    t = qb.shape[0]
    masks = jnp.asarray(_gla_masks(c), F32)
    row = lambda w: pl.BlockSpec((tb, w), lambda i: (i, 0))
    st_shape = (N_HEADS_B, DK_HEAD_B, DV_HEAD_B)
    s_spec = pl.BlockSpec(st_shape, lambda i: (0, 0, 0))
    n_chunks = tb // c
    return pl.pallas_call(
        functools.partial(_gla_stream_kernel, c=c, n_chunks=n_chunks, unroll=min(GLA_UNROLL, n_chunks)),
        grid=(t // tb,),
        in_specs=[row(DK_B), row(DK_B), row(DV_B), row(DK_B), _const_spec(masks.shape), s_spec],
        out_specs=(row(DV_B), s_spec),
        out_shape=(jax.ShapeDtypeStruct((t, DV_B), BF16), jax.ShapeDtypeStruct(st_shape, F32)),
        scratch_shapes=[pltpu.VMEM((N_HEADS_B, DV_HEAD_B, DK_HEAD_B), F32)],
        compiler_params=_params(),
        name="gla_stream",
    )(qb, kb, vb, la, masks, s0)


def _gla_seqs(qb, kb, vb, la, s0, *, c):
    t = qb.shape[0]
    n = s0.shape[0]
    assert t == n * c and n % GLA_SEQS_PER_STEP == 0
    masks = jnp.asarray(_gla_masks(c), F32)
    tb = GLA_SEQS_PER_STEP * c
    row = lambda w: pl.BlockSpec((tb, w), lambda i: (i, 0))
    s_spec = pl.BlockSpec((GLA_SEQS_PER_STEP,) + s0.shape[1:], lambda i: (i, 0, 0, 0))
    return pl.pallas_call(
        functools.partial(_gla_seqs_kernel, c=c, n_seqs=GLA_SEQS_PER_STEP),
        grid=(n // GLA_SEQS_PER_STEP,),
        in_specs=[row(DK_B), row(DK_B), row(DV_B), row(DK_B), _const_spec(masks.shape), s_spec],
        out_specs=(row(DV_B), s_spec),
        out_shape=(jax.ShapeDtypeStruct((t, DV_B), BF16), jax.ShapeDtypeStruct(s0.shape, F32)),
        compiler_params=_params(),
        name="gla_seqs",
    )(qb, kb, vb, la, masks, s0)


def _merge_ffn_kernel(x_ref, oa_ref, ob_ref, rs_ref, ga_ref, gb_ref, gn_ref, gpost_ref, gpre_ref, gfpost_ref,
                      wpa_ref, wpb_ref, wout_ref, wg_ref, wu_ref, wd_ref, y_ref, x1_ref, h_ref, acc_ref):
    pa = _dot(oa_ref[...], wpa_ref[...])
    ob = jnp.concatenate(
        [_rms(ob_ref[:, h * DV_HEAD_B:(h + 1) * DV_HEAD_B].astype(F32)) * gn_ref[...] for h in range(N_HEADS_B)],
        axis=1) * rs_ref[...].astype(F32)
    pb = _dot(ob.astype(BF16), wpb_ref[...])
    mix = jax.nn.sigmoid(ga_ref[...].astype(F32)) * pa + jax.nn.sigmoid(gb_ref[...].astype(F32)) * pb
    x1 = x_ref[...] + _rms(_dot(mix.astype(BF16), wout_ref[...])) * gpost_ref[...]
    x1_ref[...] = x1
    h_ref[...] = (_rms(x1) * gpre_ref[...]).astype(BF16)
    acc_ref[...] = jnp.zeros_like(acc_ref)

    for j in range(N_FF_CHUNKS):
        cols = slice(j * FF_CHUNK, (j + 1) * FF_CHUNK)
        h = h_ref[...]
        g = _dot(h, wg_ref[:, cols])
        u = _dot(h, wu_ref[:, cols])
        acc_ref[...] += _dot((g * jax.nn.sigmoid(g) * u).astype(BF16), wd_ref[cols, :])
    y_ref[...] = x1_ref[...] + _rms(acc_ref[...]) * gfpost_ref[...]


def _merge_ffn(x, oa, ob, rs, ga, gb, gnorm, gpost, gpre, gfpost, wpa, wpb, wout, wg, wu, wd):
    t = x.shape[0]
    n = t // TM
    row = lambda w: pl.BlockSpec((TM, w), lambda i: (i, 0))
    vec = _const_spec((1, D_MODEL))
    return pl.pallas_call(
        _merge_ffn_kernel,
        grid=(n,),
        in_specs=[row(D_MODEL), row(D_A), row(DV_B), row(DV_B), row(D_MODEL), row(D_MODEL),
                  _const_spec((1, DV_HEAD_B)), vec, vec, vec,
                  _const_spec(wpa.shape), _const_spec(wpb.shape), _const_spec(wout.shape),
                  _const_spec(wg.shape), _const_spec(wu.shape), _const_spec(wd.shape)],
        out_specs=row(D_MODEL),
        out_shape=jax.ShapeDtypeStruct((t, D_MODEL), F32),
        scratch_shapes=[pltpu.VMEM((TM, D_MODEL), F32), pltpu.VMEM((TM, D_MODEL), BF16),
                        pltpu.VMEM((TM, D_MODEL), F32)],
        compiler_params=_params(),
        name="merge_ffn",
    )(x, oa, ob, rs, ga, gb, gnorm, gpost, gpre, gfpost, wpa, wpb, wout, wg, wu, wd)


def _rel_bias_row(table):
    n_hi = BAND_PAST + BIAS_ORIGIN - REL_CLIP
    n_lo = BIAS_F_LEN - n_hi - (2 * REL_CLIP + 1)
    h = table.shape[0]
    return jnp.concatenate([jnp.broadcast_to(table[:, -1:], (h, n_hi)), table[:, ::-1],
                            jnp.broadcast_to(table[:, :1], (h, n_lo))], axis=1)


def kernel(x_prompt, x_sample, cache_attn_k, cache_attn_v, state_gla, norm_mix_pre, norm_mix_post, norm_ffn_pre, norm_ffn_post, w_in, w_decay_up, b_decay, rel_bias, gla_norm, w_proj_a, w_proj_b, w_out, w_ffn_gate, w_ffn_up, w_ffn_down):
    depth = w_in.shape[0]
    assert depth == 1, "single-layer step"
    batch, seq, _ = x_prompt.shape
    dec_batch, dec_seq, _ = x_sample.shape
    assert batch == 1 and seq % TM == 0 and (dec_batch * dec_seq) % TM == 0
    assert seq % (ATT_SUB * ATT_QB) == 0
    past = cache_attn_k.shape[2]

    w_bf = w_in[0].astype(BF16)
    wup_pad = jnp.concatenate([w_decay_up[0], jnp.zeros((DLR_PAD - GATE_RANK, DK_B), F32)], axis=0).astype(BF16)
    bdec = b_decay[0][None, :]
    vec = lambda a: a[0][None, :]
    wpa, wpb, wout = w_proj_a[0].astype(BF16), w_proj_b[0].astype(BF16), w_out[0].astype(BF16)
    wg, wu, wd = w_ffn_gate[0].astype(BF16), w_ffn_up[0].astype(BF16), w_ffn_down[0].astype(BF16)
    gnorm = gla_norm[0][None, :]

    def layer_tail(x, oa, ob, rs, ga, gb):
        return _merge_ffn(x, oa, ob, rs, ga, gb, gnorm, vec(norm_mix_post), vec(norm_ffn_pre), vec(norm_ffn_post),
                          wpa, wpb, wout, wg, wu, wd)

    xp = x_prompt[0]
    qa, ka, va, qb, kb, vb, rs, la, ga, gb, kf, vf = _in_proj(
        xp, vec(norm_mix_pre), w_bf, wup_pad, bdec, kv_rows_every_step=False)
    f_tab = _rel_bias_row(rel_bias[0])
    oa = _attn_prompt(qa, ka, va, f_tab)
    s0 = jnp.zeros((N_HEADS_B, DK_HEAD_B, DV_HEAD_B), F32)
    ob, sp = _gla_stream(qb, kb, vb, la, s0, c=GLA_CHUNK, tb=GLA_TB)
    yp = layer_tail(xp, oa, ob, rs, ga, gb)
    keep = min(BAND_PAST, seq)
    assert keep == TM

    xs = x_sample.reshape(dec_batch * dec_seq, D_MODEL)
    qa, ka, va, qb, kb, vb, rs, la, ga, gb, kfs, vfs = _in_proj(
        xs, vec(norm_mix_pre), w_bf, wup_pad, bdec, kv_rows_every_step=True)
    assert past == BAND_PAST
    oa = _attn_sample(qa, kfs, vfs, jnp.transpose(cache_attn_k[0], (0, 2, 3, 1)),
                      jnp.transpose(cache_attn_v[0], (0, 2, 3, 1)), f_tab, seq=dec_seq)
    gla_chunk = CHUNK if dec_seq % CHUNK == 0 else dec_seq
    assert gla_chunk == dec_seq
    ob, ss = _gla_seqs(qb, kb, vb, la, state_gla[0], c=gla_chunk)
    ys = layer_tail(xs, oa, ob, rs, ga, gb)

    hd = (N_HEADS_A, HEAD_DIM_A)
    return (yp[None], ys.reshape(dec_batch, dec_seq, D_MODEL),
            kf.reshape((1, 1, keep) + hd), vf.reshape((1, 1, keep) + hd), sp[None, None],
            kfs.reshape((1, dec_batch, dec_seq) + hd), vfs.reshape((1, dec_batch, dec_seq) + hd), ss[None])
```

```python
import functools

import numpy as np
import jax
import jax.numpy as jnp
from jax import lax
from jax.experimental import pallas as pl
from jax.experimental.pallas import tpu as pltpu

F32 = jnp.float32
BF16 = jnp.bfloat16

D_MODEL = 1024
CHUNK = 64
BAND_CHUNKS = 8
BAND_PAST = BAND_CHUNKS * CHUNK
N_HEADS_A = 8
HEAD_DIM_A = 64
D_A = N_HEADS_A * HEAD_DIM_A
REL_CLIP = 128
N_HEADS_B = 4
DK_HEAD_B = 128
DV_HEAD_B = 256
DK_B = N_HEADS_B * DK_HEAD_B
DV_B = N_HEADS_B * DV_HEAD_B
GATE_RANK = 16
GATE_TEMP = 16.0
D_FF = 2816
EPS = 1e-6

LANES = 128
SUBLANES = 8
BF16_ROWS = 16
LOG2E = 1.4426950408889634
VMEM_LIMIT_BYTES = 56 * 1024 * 1024
NEG_BIG = -1e30

_PIECES = (("qa", D_A), ("ka", D_A), ("va", D_A), ("qb", DK_B), ("kb", DK_B), ("vb", DV_B),
           ("rb", DV_B), ("dlr", GATE_RANK), ("ga", D_MODEL), ("gb", D_MODEL))
_OFF = {}
_o = 0
for _n, _w in _PIECES:
    _OFF[_n] = (_o, _o + _w)
    _o += _w
D_IN = _o
assert all(lo % BF16_ROWS == 0 for lo, _ in _OFF.values())

TM = 512
ATT_QB = 256
ATT_WIN = ATT_QB + BAND_PAST
ATT_SUB = 4
ATT_MASKED = BAND_PAST // ATT_QB
assert ATT_MASKED <= ATT_SUB
BIAS_ORIGIN = ATT_QB
BIAS_F_LEN = 1024
GLA_TB = 512
GLA_CHUNK = 128
GLA_UNROLL = 2
GLA_SEQS_PER_STEP = 4
FF_CHUNK = 256
N_FF_CHUNKS = D_FF // FF_CHUNK


def _const_spec(shape):
    nd = len(shape)
    return pl.BlockSpec(shape, lambda i: (0,) * nd, pipeline_mode=pl.Buffered(1))


def _params():
    return pltpu.CompilerParams(dimension_semantics=("arbitrary",), vmem_limit_bytes=VMEM_LIMIT_BYTES)


def _rms(x):
    return x * lax.rsqrt(jnp.mean(x * x, axis=-1, keepdims=True) + EPS)


def _dot(a, b):
    return jnp.dot(a, b, preferred_element_type=F32)


def _dot_nt(a, b):
    return lax.dot_general(a, b, (((1,), (1,)), ((), ())), preferred_element_type=F32)


def _dot_tn(a, b):
    return lax.dot_general(a, b, (((0,), (0,)), ((), ())), preferred_element_type=F32)


def _in_proj_kernel(x_ref, g_ref, wt_ref, wup_ref, bdec_ref,
                    qa_ref, ka_ref, va_ref, qb_ref, kb_ref, vb_ref, rs_ref, la_ref, ga_ref, gb_ref,
                    kf_ref, vf_ref, *, kv_rows_every_step):
    h = (_rms(x_ref[...]) * g_ref[...]).astype(BF16)

    def proj(name):
        lo, hi = _OFF[name]
        return _dot_nt(h, wt_ref[lo:hi, :])

    z = _dot(proj("dlr").astype(BF16), wup_ref[...]) + bdec_ref[...]
    la_ref[...] = (jnp.minimum(z, 0.0) - jnp.log1p(jnp.exp(-jnp.abs(z)))) * (LOG2E / GATE_TEMP)
    qa_ref[...] = (proj("qa") * (HEAD_DIM_A ** -0.5 * LOG2E)).astype(BF16)
    ka = proj("ka")
    va = proj("va")
    ka_ref[...] = ka.astype(BF16)
    va_ref[...] = va.astype(BF16)
    if kv_rows_every_step:
        kf_ref[...] = ka
        vf_ref[...] = va
    else:
        @pl.when(pl.program_id(0) == pl.num_programs(0) - 1)
        def _():
            kf_ref[...] = ka
            vf_ref[...] = va
    qb_ref[...] = (proj("qb") * (DK_HEAD_B ** -0.5)).astype(BF16)
    kb_ref[...] = proj("kb").astype(BF16)
    vb_ref[...] = proj("vb").astype(BF16)
    r = proj("rb")
    rs_ref[...] = (r * jax.nn.sigmoid(r)).astype(BF16)
    ga_ref[...] = proj("ga").astype(BF16)
    gb_ref[...] = proj("gb").astype(BF16)


def _in_proj(x, g, wt_bf, wup, bdec, *, kv_rows_every_step):
    t = x.shape[0]
    n = t // TM
    row = lambda w: pl.BlockSpec((TM, w), lambda i: (i, 0))
    if kv_rows_every_step:
        kv_rows, kv_spec = t, row(D_A)
    else:
        kv_rows, kv_spec = TM, pl.BlockSpec((TM, D_A), lambda i: (0, 0))
    widths = (D_A, D_A, D_A, DK_B, DK_B, DV_B, DV_B, DK_B, D_MODEL, D_MODEL)
    dtypes = (BF16,) * 7 + (F32, BF16, BF16)
    out_shape = tuple(jax.ShapeDtypeStruct((t, w), d) for w, d in zip(widths, dtypes))
    out_shape += (jax.ShapeDtypeStruct((kv_rows, D_A), F32),) * 2
    out_specs = tuple(row(w) for w in widths) + (kv_spec, kv_spec)
    return pl.pallas_call(
        functools.partial(_in_proj_kernel, kv_rows_every_step=kv_rows_every_step),
        grid=(n,),
        in_specs=[row(D_MODEL), _const_spec((1, D_MODEL)), _const_spec((D_IN, D_MODEL)),
                  _const_spec((GATE_RANK, DK_B)), _const_spec((1, DK_B))],
        out_specs=out_specs,
        out_shape=out_shape,
        compiler_params=_params(),
        name="in_proj",
    )(x, g, wt_bf, wup, bdec)


def _softmax_pv(s, v):
    m = jnp.max(s, axis=-1, keepdims=True)
    e = jnp.exp2(s - m).astype(BF16)
    pv = _dot(e, jnp.concatenate([v, jnp.ones_like(v)], axis=1))
    return pv[:, :LANES] / pv[:, LANES:LANES + 1]


def _head_pair_attention(q_pair, k_pair, v_pair, bias0, bias1):
    nq = q_pair.shape[0]
    lane = lax.broadcasted_iota(jnp.int32, (1, LANES), 1)
    first = lane < HEAD_DIM_A
    zero = jnp.zeros_like(q_pair)
    qs = jnp.concatenate([jnp.where(first, q_pair, zero), jnp.where(first, zero, q_pair)], axis=0)
    s = _dot_nt(qs, k_pair) + jnp.concatenate([bias0, bias1], axis=0)
    pv = _softmax_pv(s, v_pair)
    return jnp.where(first, pv[:nq], pv[nq:])


def _expand_rel_bias(f_ref, bias_ref, n_q, n_k, first_valid, n_valid, band):
    t = lax.broadcasted_iota(jnp.int32, (n_q, n_k), 0)
    k = lax.broadcasted_iota(jnp.int32, (n_q, n_k), 1)
    keep = (k < n_valid) & (k >= first_valid)
    if band:
        jq, jk = t // CHUNK, k // CHUNK
        keep = keep & (jk >= jq) & (jk <= jq + BAND_CHUNKS)
    for h in range(N_HEADS_A):
        rows = jnp.broadcast_to(f_ref[h:h + 1, :], (n_q, BIAS_F_LEN))
        rolled = pltpu.roll(rows, BIAS_F_LEN - BIAS_ORIGIN, 1, stride=1, stride_axis=0)
        bias_ref[h] = jnp.where(keep, rolled[:, :n_k] * LOG2E, NEG_BIG)


def _attn_prompt_kernel(q_ref, *refs):
    n_kb = ATT_SUB + ATT_WIN // ATT_QB - 1
    k_refs, v_refs = refs[:n_kb], refs[n_kb:2 * n_kb]
    f_ref, o_ref, bias_ref = refs[2 * n_kb:]
    i = pl.program_id(0)

    @pl.when(i == 0)
    def _():
        _expand_rel_bias(f_ref, bias_ref.at[0], ATT_QB, ATT_WIN, 0, ATT_WIN, band=True)
        for g in range(ATT_MASKED):
            _expand_rel_bias(f_ref, bias_ref.at[1 + g], ATT_QB, ATT_WIN, BAND_PAST - ATT_QB * g, ATT_WIN,
                             band=True)

    for sub in range(ATT_SUB):
        kwin = jnp.concatenate([r[...] for r in k_refs[sub:sub + ATT_WIN // ATT_QB]], axis=0)
        vwin = jnp.concatenate([r[...] for r in v_refs[sub:sub + ATT_WIN // ATT_QB]], axis=0)
        rows = slice(sub * ATT_QB, (sub + 1) * ATT_QB)
        which = jnp.where(i == 0, 1 + sub, 0) if sub < ATT_MASKED else 0
        for p in range(N_HEADS_A // 2):
            sl = slice(p * LANES, (p + 1) * LANES)
            o = _head_pair_attention(q_ref[rows, sl], kwin[:, sl], vwin[:, sl],
                                     bias_ref[which, 2 * p], bias_ref[which, 2 * p + 1])
            o_ref[rows, sl] = o.astype(BF16)


def _attn_prompt(qa, ka, va, f_tab):
    t = qa.shape[0]
    n = t // (ATT_SUB * ATT_QB)
    n_kb = ATT_SUB + ATT_WIN // ATT_QB - 1
    kblk = [pl.BlockSpec((ATT_QB, D_A), functools.partial(
        lambda j, i: (jnp.maximum(ATT_SUB * i + j - (n_kb - ATT_SUB), 0), 0), j)) for j in range(n_kb)]
    qblk = pl.BlockSpec((ATT_SUB * ATT_QB, D_A), lambda i: (i, 0))
    return pl.pallas_call(
        _attn_prompt_kernel,
        grid=(n,),
        in_specs=[qblk] + kblk + kblk + [_const_spec((N_HEADS_A, BIAS_F_LEN))],
        out_specs=qblk,
        out_shape=jax.ShapeDtypeStruct((t, D_A), BF16),
        scratch_shapes=[pltpu.VMEM((1 + ATT_MASKED, N_HEADS_A, ATT_QB, ATT_WIN), F32)],
        compiler_params=_params(),
        name="attn_prompt",
    )(qa, *([ka] * n_kb), *([va] * n_kb), f_tab)


def _attn_sample_kernel(q_ref, kt_ref, vt_ref, kn_ref, vn_ref, f_ref, o_ref, bias_ref):
    n_q, w = q_ref.shape[0], kt_ref.shape[-1]

    @pl.when(pl.program_id(0) == 0)
    def _():
        _expand_rel_bias(f_ref, bias_ref, n_q, bias_ref.shape[-1], 0, w + n_q, band=False)

    lane = lax.broadcasted_iota(jnp.int32, (1, LANES), 1)
    first = lane < HEAD_DIM_A
    for p in range(N_HEADS_A // 2):
        sl = slice(p * LANES, (p + 1) * LANES)
        q_pair = q_ref[:, sl]
        zero = jnp.zeros_like(q_pair)
        qs = jnp.concatenate([jnp.where(first, q_pair, zero), jnp.where(first, zero, q_pair)], axis=0)
        kt = jnp.concatenate([kt_ref[2 * p], kt_ref[2 * p + 1]], axis=0).astype(BF16)
        vt = jnp.concatenate([vt_ref[2 * p], vt_ref[2 * p + 1]], axis=0).astype(BF16)
        kn = kn_ref[:, sl].astype(BF16)
        vn = vn_ref[:, sl].astype(BF16)
        bias = jnp.concatenate([bias_ref[2 * p], bias_ref[2 * p + 1]], axis=0)
        s_c = _dot(qs, kt) + bias[:, :w]
        s_n = _dot_nt(qs, kn) + bias[:, w:w + n_q]
        m = jnp.maximum(jnp.max(s_c, axis=-1, keepdims=True), jnp.max(s_n, axis=-1, keepdims=True))
        e_c = jnp.exp2(s_c - m).astype(BF16)
        e_n = jnp.exp2(s_n - m).astype(BF16)
        pv = (_dot_nt(e_c, jnp.concatenate([vt, jnp.ones_like(vt)], axis=0))
              + _dot(e_n, jnp.concatenate([vn, jnp.ones_like(vn)], axis=1)))
        pv = pv[:, :LANES] / pv[:, LANES:LANES + 1]
        o_ref[:, sl] = jnp.where(first, pv[:n_q], pv[n_q:]).astype(BF16)


def _attn_sample(qa, kn, vn, cache_kt, cache_vt, f_tab, *, seq):
    t = qa.shape[0]
    nb, _, _, w = cache_kt.shape
    n_k = w + seq
    row = pl.BlockSpec((seq, D_A), lambda i: (i, 0))
    cache = pl.BlockSpec((None, N_HEADS_A, HEAD_DIM_A, w), lambda i: (i, 0, 0, 0))
    return pl.pallas_call(
        _attn_sample_kernel,
        grid=(nb,),
        in_specs=[row, cache, cache, row, row, _const_spec((N_HEADS_A, BIAS_F_LEN))],
        out_specs=row,
        out_shape=jax.ShapeDtypeStruct((t, D_A), BF16),
        scratch_shapes=[pltpu.VMEM((N_HEADS_A, seq, n_k + (-n_k) % LANES), F32)],
        compiler_params=_params(),
        name="attn_sample",
    )(qa, cache_kt, cache_vt, kn, vn, f_tab)


def _gla_levels(c):
    out, hs = [], 1
    while hs < c:
        out.append(hs)
        hs *= 2
    return out


def _gla_masks(c):
    t = np.arange(c)
    masks = []
    for hs in _gla_levels(c):
        blk = t // (2 * hs)
        second = (t % (2 * hs)) >= hs
        masks.append((blk[:, None] == blk[None, :]) & second[:, None] & ~second[None, :])
    masks.append(np.eye(c, dtype=bool))
    return np.stack(masks).astype(np.float32)


def _split_row_bcast(x, hs, row):
    c, w = x.shape
    blk = 2 * hs
    if blk >= SUBLANES:
        xr = x.reshape(c // blk, blk, w)
        return jnp.broadcast_to(xr[:, hs - 1:hs, :], (c // blk, blk, w)).reshape(c, w)
    tiles = (c // SUBLANES, SUBLANES, w)
    x3, pos = x.reshape(tiles), (row & (blk - 1)).reshape(tiles)
    if hs == 1:
        out = jnp.where(pos == 1, pltpu.roll(x3, 1, 1), x3)
    else:
        assert hs == 2
        nxt = jnp.where((pos & 1) == 1, x3, pltpu.roll(x3, SUBLANES - 1, 1))
        out = jnp.where(pos >= 2, pltpu.roll(nxt, 2, 1), nxt)
    return out.reshape(c, w)


def _level_step(p, qf, kf, hs, row):
    c, w = p.shape
    if hs < SUBLANES:
        tot = _split_row_bcast(p, hs, row)
        second = (row & hs) != 0
        return jnp.where(second, p, tot - p), jnp.where(second, p + tot, p), jnp.where(second, qf, kf)
    xs, ps, qks = [], [], []
    for lo in range(0, c, 2 * hs):
        first, second = p[lo:lo + hs], p[lo + hs:lo + 2 * hs]
        tot = jnp.broadcast_to(first[hs - 1:hs], (hs, w))
        xs += [tot - first, second]
        ps += [first, second + tot]
        qks += [kf[lo:lo + hs], qf[lo + hs:lo + 2 * hs]]
    if hs % BF16_ROWS == 0:
        qk = jnp.concatenate(qks, axis=0)
    else:
        qk = jnp.where((row & hs) != 0, qf, kf)
    return jnp.concatenate(xs, axis=0), jnp.concatenate(ps, axis=0), qk


def _gla_chunk(rows, c, q_ref, k_ref, v_ref, la_ref, mask_ref, st_in, st_out, o_ref, *, state_t):
    levels = _gla_levels(c)
    n_lvl = len(levels)
    row = lax.broadcasted_iota(jnp.int32, (c, DK_HEAD_B), 0)
    for h in range(N_HEADS_B):
        hk = slice(h * DK_HEAD_B, (h + 1) * DK_HEAD_B)
        hv = slice(h * DV_HEAD_B, (h + 1) * DV_HEAD_B)
        qf = q_ref[rows, hk]
        kf = k_ref[rows, hk]
        vh = v_ref[rows, hv]
        p = la_ref[rows, hk]
        a = mask_ref[n_lvl] * _dot_nt(qf, kf)
        for l, hs in enumerate(levels):
            x, p, qk = _level_step(p, qf, kf, hs, row)
            z = qk * jnp.exp2(x).astype(BF16)
            a = a + mask_ref[l] * _dot_nt(z, z)
        b = p
        b_last = b[c - 1:c, :]
        o = _dot(a.astype(BF16), vh)
        st = st_in[h]
        qd = qf * jnp.exp2(b).astype(BF16)
        kd = kf * jnp.exp2(b_last - b).astype(BF16)
        d_last = jnp.exp2(b_last)
        if state_t:
            o = o + _dot_nt(qd, st.astype(BF16))
            st_out[h] = st * d_last + _dot_tn(vh, kd)
        else:
            o = o + _dot(qd, st.astype(BF16))
            d_col = jnp.broadcast_to(d_last, (SUBLANES, DK_HEAD_B)).T[:, 0:1]
            st_out[h] = st * d_col + _dot_tn(kd, vh)
        o_ref[rows, hv] = o.astype(BF16)


def _gla_stream_kernel(q_ref, k_ref, v_ref, la_ref, mask_ref, s0_ref,
                       o_ref, sout_ref, st_ref, *, c, n_chunks, unroll):
    @pl.when(pl.program_id(0) == 0)
    def _():
        for h in range(N_HEADS_B):
            st_ref[h] = s0_ref[h].T

    def chunk(ci, carry):
        rows = pl.ds(pl.multiple_of(ci * c, c), c)
        _gla_chunk(rows, c, q_ref, k_ref, v_ref, la_ref, mask_ref, st_ref, st_ref, o_ref, state_t=True)
        return carry

    lax.fori_loop(0, n_chunks, chunk, 0, unroll=unroll)

    @pl.when(pl.program_id(0) == pl.num_programs(0) - 1)
    def _():
        for h in range(N_HEADS_B):
            sout_ref[h] = st_ref[h].T


def _gla_seqs_kernel(q_ref, k_ref, v_ref, la_ref, mask_ref, s0_ref, o_ref, sout_ref, *, c, n_seqs):
    for j in range(n_seqs):
        _gla_chunk(slice(j * c, (j + 1) * c), c, q_ref, k_ref, v_ref, la_ref, mask_ref,
                   s0_ref.at[j], sout_ref.at[j], o_ref, state_t=False)


def _gla_stream(qb, kb, vb, la, s0, *, c, tb):
    t = qb.shape[0]
    masks = jnp.asarray(_gla_masks(c), F32)
    row = lambda w: pl.BlockSpec((tb, w), lambda i: (i, 0))
    st_shape = (N_HEADS_B, DK_HEAD_B, DV_HEAD_B)
    s_spec = pl.BlockSpec(st_shape, lambda i: (0, 0, 0))
    n_chunks = tb // c
    return pl.pallas_call(
        functools.partial(_gla_stream_kernel, c=c, n_chunks=n_chunks, unroll=min(GLA_UNROLL, n_chunks)),
        grid=(t // tb,),
        in_specs=[row(DK_B), row(DK_B), row(DV_B), row(DK_B), _const_spec(masks.shape), s_spec],
        out_specs=(row(DV_B), s_spec),
        out_shape=(jax.ShapeDtypeStruct((t, DV_B), BF16), jax.ShapeDtypeStruct(st_shape, F32)),
        scratch_shapes=[pltpu.VMEM((N_HEADS_B, DV_HEAD_B, DK_HEAD_B), F32)],
        compiler_params=_params(),
        name="gla_stream",
    )(qb, kb, vb, la, masks, s0)


def _gla_seqs(qb, kb, vb, la, s0, *, c):
    t = qb.shape[0]
    n = s0.shape[0]
    assert t == n * c and n % GLA_SEQS_PER_STEP == 0
    masks = jnp.asarray(_gla_masks(c), F32)
    tb = GLA_SEQS_PER_STEP * c
    row = lambda w: pl.BlockSpec((tb, w), lambda i: (i, 0))
    s_spec = pl.BlockSpec((GLA_SEQS_PER_STEP,) + s0.shape[1:], lambda i: (i, 0, 0, 0))
    return pl.pallas_call(
        functools.partial(_gla_seqs_kernel, c=c, n_seqs=GLA_SEQS_PER_STEP),
        grid=(n // GLA_SEQS_PER_STEP,),
        in_specs=[row(DK_B), row(DK_B), row(DV_B), row(DK_B), _const_spec(masks.shape), s_spec],
        out_specs=(row(DV_B), s_spec),
        out_shape=(jax.ShapeDtypeStruct((t, DV_B), BF16), jax.ShapeDtypeStruct(s0.shape, F32)),
        compiler_params=_params(),
        name="gla_seqs",
    )(qb, kb, vb, la, masks, s0)


def _merge_ffn_kernel(x_ref, oa_ref, ob_ref, rs_ref, ga_ref, gb_ref, gn_ref, gpost_ref, gpre_ref, gfpost_ref,
                      wpa_ref, wpb_ref, wout_ref, wg_ref, wu_ref, wd_ref, y_ref, x1_ref, h_ref, acc_ref):
    pa = _dot(oa_ref[...], wpa_ref[...])
    ob = jnp.concatenate(
        [_rms(ob_ref[:, h * DV_HEAD_B:(h + 1) * DV_HEAD_B].astype(F32)) * gn_ref[...] for h in range(N_HEADS_B)],
        axis=1) * rs_ref[...].astype(F32)
    pb = _dot(ob.astype(BF16), wpb_ref[...])
    mix = jax.nn.sigmoid(ga_ref[...].astype(F32)) * pa + jax.nn.sigmoid(gb_ref[...].astype(F32)) * pb
    x1 = x_ref[...] + _rms(_dot(mix.astype(BF16), wout_ref[...])) * gpost_ref[...]
    x1_ref[...] = x1
    h_ref[...] = (_rms(x1) * gpre_ref[...]).astype(BF16)
    acc_ref[...] = jnp.zeros_like(acc_ref)

    for j in range(N_FF_CHUNKS):
        cols = slice(j * FF_CHUNK, (j + 1) * FF_CHUNK)
        h = h_ref[...]
        g = _dot(h, wg_ref[:, cols])
        u = _dot(h, wu_ref[:, cols])
        acc_ref[...] += _dot((g * jax.nn.sigmoid(g) * u).astype(BF16), wd_ref[cols, :])
    y_ref[...] = x1_ref[...] + _rms(acc_ref[...]) * gfpost_ref[...]


def _merge_ffn(x, oa, ob, rs, ga, gb, gnorm, gpost, gpre, gfpost, wpa, wpb, wout, wg, wu, wd):
    t = x.shape[0]
    n = t // TM
    row = lambda w: pl.BlockSpec((TM, w), lambda i: (i, 0))
    vec = _const_spec((1, D_MODEL))
    return pl.pallas_call(
        _merge_ffn_kernel,
        grid=(n,),
        in_specs=[row(D_MODEL), row(D_A), row(DV_B), row(DV_B), row(D_MODEL), row(D_MODEL),
                  _const_spec((1, DV_HEAD_B)), vec, vec, vec,
                  _const_spec(wpa.shape), _const_spec(wpb.shape), _const_spec(wout.shape),
                  _const_spec(wg.shape), _const_spec(wu.shape), _const_spec(wd.shape)],
        out_specs=row(D_MODEL),
        out_shape=jax.ShapeDtypeStruct((t, D_MODEL), F32),
        scratch_shapes=[pltpu.VMEM((TM, D_MODEL), F32), pltpu.VMEM((TM, D_MODEL), BF16),
                        pltpu.VMEM((TM, D_MODEL), F32)],
        compiler_params=_params(),
        name="merge_ffn",
    )(x, oa, ob, rs, ga, gb, gnorm, gpost, gpre, gfpost, wpa, wpb, wout, wg, wu, wd)


def _rel_bias_row(table):
    n_hi = BAND_PAST + BIAS_ORIGIN - REL_CLIP
    n_lo = BIAS_F_LEN - n_hi - (2 * REL_CLIP + 1)
    h = table.shape[0]
    return jnp.concatenate([jnp.broadcast_to(table[:, -1:], (h, n_hi)), table[:, ::-1],
                            jnp.broadcast_to(table[:, :1], (h, n_lo))], axis=1)


def kernel(x_prompt, x_sample, cache_attn_k, cache_attn_v, state_gla, norm_mix_pre, norm_mix_post, norm_ffn_pre, norm_ffn_post, w_in, w_decay_up, b_decay, rel_bias, gla_norm, w_proj_a, w_proj_b, w_out, w_ffn_gate, w_ffn_up, w_ffn_down):
    depth = w_in.shape[0]
    assert depth == 1, "single-layer step"
    batch, seq, _ = x_prompt.shape
    dec_batch, dec_seq, _ = x_sample.shape
    assert batch == 1 and seq % TM == 0 and (dec_batch * dec_seq) % TM == 0
    assert seq % (ATT_SUB * ATT_QB) == 0
    past = cache_attn_k.shape[2]

    wt_bf = jnp.transpose(w_in[0]).astype(BF16)
    wup = w_decay_up[0].astype(BF16)
    bdec = b_decay[0][None, :]
    vec = lambda a: a[0][None, :]
    wpa, wpb, wout = w_proj_a[0].astype(BF16), w_proj_b[0].astype(BF16), w_out[0].astype(BF16)
    wg, wu, wd = w_ffn_gate[0].astype(BF16), w_ffn_up[0].astype(BF16), w_ffn_down[0].astype(BF16)
    gnorm = gla_norm[0][None, :]

    def layer_tail(x, oa, ob, rs, ga, gb):
        return _merge_ffn(x, oa, ob, rs, ga, gb, gnorm, vec(norm_mix_post), vec(norm_ffn_pre), vec(norm_ffn_post),
                          wpa, wpb, wout, wg, wu, wd)

    xp = x_prompt[0]
    qa, ka, va, qb, kb, vb, rs, la, ga, gb, kf, vf = _in_proj(
        xp, vec(norm_mix_pre), wt_bf, wup, bdec, kv_rows_every_step=False)
    f_tab = _rel_bias_row(rel_bias[0])
    oa = _attn_prompt(qa, ka, va, f_tab)
    s0 = jnp.zeros((N_HEADS_B, DK_HEAD_B, DV_HEAD_B), F32)
    ob, sp = _gla_stream(qb, kb, vb, la, s0, c=GLA_CHUNK, tb=GLA_TB)
    yp = layer_tail(xp, oa, ob, rs, ga, gb)
    keep = min(BAND_PAST, seq)
    assert keep == TM

    xs = x_sample.reshape(dec_batch * dec_seq, D_MODEL)
    qa, ka, va, qb, kb, vb, rs, la, ga, gb, kfs, vfs = _in_proj(
        xs, vec(norm_mix_pre), wt_bf, wup, bdec, kv_rows_every_step=True)
    assert past == BAND_PAST
    oa = _attn_sample(qa, kfs, vfs, jnp.transpose(cache_attn_k[0], (0, 2, 3, 1)),
                      jnp.transpose(cache_attn_v[0], (0, 2, 3, 1)), f_tab, seq=dec_seq)
    gla_chunk = CHUNK if dec_seq % CHUNK == 0 else dec_seq
    assert gla_chunk == dec_seq
    ob, ss = _gla_seqs(qb, kb, vb, la, state_gla[0], c=gla_chunk)
    ys = layer_tail(xs, oa, ob, rs, ga, gb)

    hd = (N_HEADS_A, HEAD_DIM_A)
    return (yp[None], ys.reshape(dec_batch, dec_seq, D_MODEL),
            kf.reshape((1, 1, keep) + hd), vf.reshape((1, 1, keep) + hd), sp[None, None],
            kfs.reshape((1, dec_batch, dec_seq) + hd), vfs.reshape((1, dec_batch, dec_seq) + hd), ss[None])
```

```python
import functools

import numpy as np
import jax
import jax.numpy as jnp
from jax import lax
from jax.experimental import pallas as pl
from jax.experimental.pallas import tpu as pltpu

F32 = jnp.float32
BF16 = jnp.bfloat16

D_MODEL = 1024
CHUNK = 64
BAND_CHUNKS = 8
BAND_PAST = BAND_CHUNKS * CHUNK
N_HEADS_A = 8
HEAD_DIM_A = 64
D_A = N_HEADS_A * HEAD_DIM_A
REL_CLIP = 128
N_HEADS_B = 4
DK_HEAD_B = 128
DV_HEAD_B = 256
DK_B = N_HEADS_B * DK_HEAD_B
DV_B = N_HEADS_B * DV_HEAD_B
GATE_RANK = 16
GATE_TEMP = 16.0
D_FF = 2816
EPS = 1e-6

LANES = 128
SUBLANES = 8
BF16_ROWS = 16
LOG2E = 1.4426950408889634
VMEM_LIMIT_BYTES = 56 * 1024 * 1024
NEG_BIG = -1e30

_PIECES = (("qa", D_A), ("ka", D_A), ("va", D_A), ("qb", DK_B), ("kb", DK_B), ("vb", DV_B),
           ("rb", DV_B), ("dlr", GATE_RANK), ("ga", D_MODEL), ("gb", D_MODEL))
_OFF = {}
_o = 0
for _n, _w in _PIECES:
    _OFF[_n] = (_o, _o + _w)
    _o += _w
D_IN = _o
assert all(lo % BF16_ROWS == 0 for lo, _ in _OFF.values())

TM = 512
ATT_QB = 256
ATT_WIN = ATT_QB + BAND_PAST
ATT_SUB = 4
ATT_MASKED = BAND_PAST // ATT_QB
assert ATT_MASKED <= ATT_SUB
BIAS_ORIGIN = ATT_QB
BIAS_F_LEN = 1024
GLA_TB = 1024
GLA_CHUNK = 128
GLA_UNROLL = 2
GLA_SEQS_PER_STEP = 4
FF_CHUNK = 256
N_FF_CHUNKS = D_FF // FF_CHUNK


def _const_spec(shape):
    nd = len(shape)
    return pl.BlockSpec(shape, lambda i: (0,) * nd, pipeline_mode=pl.Buffered(1))


def _params():
    return pltpu.CompilerParams(dimension_semantics=("arbitrary",), vmem_limit_bytes=VMEM_LIMIT_BYTES)


def _rms(x):
    return x * lax.rsqrt(jnp.mean(x * x, axis=-1, keepdims=True) + EPS)


def _dot(a, b):
    return jnp.dot(a, b, preferred_element_type=F32)


def _dot_nt(a, b):
    return lax.dot_general(a, b, (((1,), (1,)), ((), ())), preferred_element_type=F32)


def _dot_tn(a, b):
    return lax.dot_general(a, b, (((0,), (0,)), ((), ())), preferred_element_type=F32)


def _in_proj_kernel(x_ref, g_ref, wt_ref, wup_ref, bdec_ref,
                    qa_ref, ka_ref, va_ref, qb_ref, kb_ref, vb_ref, rs_ref, la_ref, ga_ref, gb_ref,
                    kf_ref, vf_ref, *, kv_rows_every_step):
    h = (_rms(x_ref[...]) * g_ref[...]).astype(BF16)

    def proj(name):
        lo, hi = _OFF[name]
        return _dot_nt(h, wt_ref[lo:hi, :])

    z = _dot(proj("dlr").astype(BF16), wup_ref[...]) + bdec_ref[...]
    la_ref[...] = (jnp.minimum(z, 0.0) - jnp.log1p(jnp.exp(-jnp.abs(z)))) * (LOG2E / GATE_TEMP)
    qa_ref[...] = (proj("qa") * (HEAD_DIM_A ** -0.5 * LOG2E)).astype(BF16)
    ka = proj("ka")
    va = proj("va")
    ka_ref[...] = ka.astype(BF16)
    va_ref[...] = va.astype(BF16)
    if kv_rows_every_step:
        kf_ref[...] = ka
        vf_ref[...] = va
    else:
        @pl.when(pl.program_id(0) == pl.num_programs(0) - 1)
        def _():
            kf_ref[...] = ka
            vf_ref[...] = va
    qb_ref[...] = (proj("qb") * (DK_HEAD_B ** -0.5)).astype(BF16)
    kb_ref[...] = proj("kb").astype(BF16)
    vb_ref[...] = proj("vb").astype(BF16)
    r = proj("rb")
    rs_ref[...] = (r * jax.nn.sigmoid(r)).astype(BF16)
    ga_ref[...] = proj("ga").astype(BF16)
    gb_ref[...] = proj("gb").astype(BF16)


def _in_proj(x, g, wt_bf, wup, bdec, *, kv_rows_every_step):
    t = x.shape[0]
    n = t // TM
    row = lambda w: pl.BlockSpec((TM, w), lambda i: (i, 0))
    if kv_rows_every_step:
        kv_rows, kv_spec = t, row(D_A)
    else:
        kv_rows, kv_spec = TM, pl.BlockSpec((TM, D_A), lambda i: (0, 0))
    widths = (D_A, D_A, D_A, DK_B, DK_B, DV_B, DV_B, DK_B, D_MODEL, D_MODEL)
    dtypes = (BF16,) * 7 + (F32, BF16, BF16)
    out_shape = tuple(jax.ShapeDtypeStruct((t, w), d) for w, d in zip(widths, dtypes))
    out_shape += (jax.ShapeDtypeStruct((kv_rows, D_A), F32),) * 2
    out_specs = tuple(row(w) for w in widths) + (kv_spec, kv_spec)
    return pl.pallas_call(
        functools.partial(_in_proj_kernel, kv_rows_every_step=kv_rows_every_step),
        grid=(n,),
        in_specs=[row(D_MODEL), _const_spec((1, D_MODEL)), _const_spec((D_IN, D_MODEL)),
                  _const_spec((GATE_RANK, DK_B)), _const_spec((1, DK_B))],
        out_specs=out_specs,
        out_shape=out_shape,
        compiler_params=_params(),
        name="in_proj",
    )(x, g, wt_bf, wup, bdec)


def _softmax_pv(s, v):
    m = jnp.max(s, axis=-1, keepdims=True)
    e = jnp.exp2(s - m).astype(BF16)
    pv = _dot(e, jnp.concatenate([v, jnp.ones_like(v)], axis=1))
    return pv[:, :LANES] / pv[:, LANES:LANES + 1]


def _head_pair_attention(q_pair, k_pair, v_pair, bias0, bias1):
    nq = q_pair.shape[0]
    lane = lax.broadcasted_iota(jnp.int32, (1, LANES), 1)
    first = lane < HEAD_DIM_A
    zero = jnp.zeros_like(q_pair)
    qs = jnp.concatenate([jnp.where(first, q_pair, zero), jnp.where(first, zero, q_pair)], axis=0)
    s = _dot_nt(qs, k_pair) + jnp.concatenate([bias0, bias1], axis=0)
    pv = _softmax_pv(s, v_pair)
    return jnp.where(first, pv[:nq], pv[nq:])


def _expand_rel_bias(f_ref, bias_ref, n_q, n_k, first_valid, n_valid, band):
    t = lax.broadcasted_iota(jnp.int32, (n_q, n_k), 0)
    k = lax.broadcasted_iota(jnp.int32, (n_q, n_k), 1)
    keep = (k < n_valid) & (k >= first_valid)
    if band:
        jq, jk = t // CHUNK, k // CHUNK
        keep = keep & (jk >= jq) & (jk <= jq + BAND_CHUNKS)
    for h in range(N_HEADS_A):
        rows = jnp.broadcast_to(f_ref[h:h + 1, :], (n_q, BIAS_F_LEN))
        rolled = pltpu.roll(rows, BIAS_F_LEN - BIAS_ORIGIN, 1, stride=1, stride_axis=0)
        bias_ref[h] = jnp.where(keep, rolled[:, :n_k] * LOG2E, NEG_BIG)


def _attn_prompt_kernel(q_ref, *refs):
    n_kb = ATT_SUB + ATT_WIN // ATT_QB - 1
    k_refs, v_refs = refs[:n_kb], refs[n_kb:2 * n_kb]
    f_ref, o_ref, bias_ref = refs[2 * n_kb:]
    i = pl.program_id(0)

    @pl.when(i == 0)
    def _():
        _expand_rel_bias(f_ref, bias_ref.at[0], ATT_QB, ATT_WIN, 0, ATT_WIN, band=True)
        for g in range(ATT_MASKED):
            _expand_rel_bias(f_ref, bias_ref.at[1 + g], ATT_QB, ATT_WIN, BAND_PAST - ATT_QB * g, ATT_WIN,
                             band=True)

    for sub in range(ATT_SUB):
        kwin = jnp.concatenate([r[...] for r in k_refs[sub:sub + ATT_WIN // ATT_QB]], axis=0)
        vwin = jnp.concatenate([r[...] for r in v_refs[sub:sub + ATT_WIN // ATT_QB]], axis=0)
        rows = slice(sub * ATT_QB, (sub + 1) * ATT_QB)
        which = jnp.where(i == 0, 1 + sub, 0) if sub < ATT_MASKED else 0
        for p in range(N_HEADS_A // 2):
            sl = slice(p * LANES, (p + 1) * LANES)
            o = _head_pair_attention(q_ref[rows, sl], kwin[:, sl], vwin[:, sl],
                                     bias_ref[which, 2 * p], bias_ref[which, 2 * p + 1])
            o_ref[rows, sl] = o.astype(BF16)


def _attn_prompt(qa, ka, va, f_tab):
    t = qa.shape[0]
    n = t // (ATT_SUB * ATT_QB)
    n_kb = ATT_SUB + ATT_WIN // ATT_QB - 1
    kblk = [pl.BlockSpec((ATT_QB, D_A), functools.partial(
        lambda j, i: (jnp.maximum(ATT_SUB * i + j - (n_kb - ATT_SUB), 0), 0), j)) for j in range(n_kb)]
    qblk = pl.BlockSpec((ATT_SUB * ATT_QB, D_A), lambda i: (i, 0))
    return pl.pallas_call(
        _attn_prompt_kernel,
        grid=(n,),
        in_specs=[qblk] + kblk + kblk + [_const_spec((N_HEADS_A, BIAS_F_LEN))],
        out_specs=qblk,
        out_shape=jax.ShapeDtypeStruct((t, D_A), BF16),
        scratch_shapes=[pltpu.VMEM((1 + ATT_MASKED, N_HEADS_A, ATT_QB, ATT_WIN), F32)],
        compiler_params=_params(),
        name="attn_prompt",
    )(qa, *([ka] * n_kb), *([va] * n_kb), f_tab)


def _attn_sample_kernel(q_ref, kt_ref, vt_ref, kn_ref, vn_ref, f_ref, o_ref, bias_ref):
    n_q, w = q_ref.shape[0], kt_ref.shape[-1]

    @pl.when(pl.program_id(0) == 0)
    def _():
        _expand_rel_bias(f_ref, bias_ref, n_q, bias_ref.shape[-1], 0, w + n_q, band=False)

    lane = lax.broadcasted_iota(jnp.int32, (1, LANES), 1)
    first = lane < HEAD_DIM_A
    for p in range(N_HEADS_A // 2):
        sl = slice(p * LANES, (p + 1) * LANES)
        q_pair = q_ref[:, sl]
        zero = jnp.zeros_like(q_pair)
        qs = jnp.concatenate([jnp.where(first, q_pair, zero), jnp.where(first, zero, q_pair)], axis=0)
        kt = jnp.concatenate([kt_ref[2 * p], kt_ref[2 * p + 1]], axis=0).astype(BF16)
        vt = jnp.concatenate([vt_ref[2 * p], vt_ref[2 * p + 1]], axis=0).astype(BF16)
        kn = kn_ref[:, sl].astype(BF16)
        vn = vn_ref[:, sl].astype(BF16)
        bias = jnp.concatenate([bias_ref[2 * p], bias_ref[2 * p + 1]], axis=0)
        s_c = _dot(qs, kt) + bias[:, :w]
        s_n = _dot_nt(qs, kn) + bias[:, w:w + n_q]
        m = jnp.maximum(jnp.max(s_c, axis=-1, keepdims=True), jnp.max(s_n, axis=-1, keepdims=True))
        e_c = jnp.exp2(s_c - m).astype(BF16)
        e_n = jnp.exp2(s_n - m).astype(BF16)
        pv = (_dot_nt(e_c, jnp.concatenate([vt, jnp.ones_like(vt)], axis=0))
              + _dot(e_n, jnp.concatenate([vn, jnp.ones_like(vn)], axis=1)))
        pv = pv[:, :LANES] / pv[:, LANES:LANES + 1]
        o_ref[:, sl] = jnp.where(first, pv[:n_q], pv[n_q:]).astype(BF16)


def _attn_sample(qa, kn, vn, cache_kt, cache_vt, f_tab, *, seq):
    t = qa.shape[0]
    nb, _, _, w = cache_kt.shape
    n_k = w + seq
    row = pl.BlockSpec((seq, D_A), lambda i: (i, 0))
    cache = pl.BlockSpec((None, N_HEADS_A, HEAD_DIM_A, w), lambda i: (i, 0, 0, 0))
    return pl.pallas_call(
        _attn_sample_kernel,
        grid=(nb,),
        in_specs=[row, cache, cache, row, row, _const_spec((N_HEADS_A, BIAS_F_LEN))],
        out_specs=row,
        out_shape=jax.ShapeDtypeStruct((t, D_A), BF16),
        scratch_shapes=[pltpu.VMEM((N_HEADS_A, seq, n_k + (-n_k) % LANES), F32)],
        compiler_params=_params(),
        name="attn_sample",
    )(qa, cache_kt, cache_vt, kn, vn, f_tab)


def _gla_levels(c):
    out, hs = [], 1
    while hs < c:
        out.append(hs)
        hs *= 2
    return out


def _gla_masks(c):
    t = np.arange(c)
    masks = []
    for hs in _gla_levels(c):
        blk = t // (2 * hs)
        second = (t % (2 * hs)) >= hs
        masks.append((blk[:, None] == blk[None, :]) & second[:, None] & ~second[None, :])
    masks.append(np.eye(c, dtype=bool))
    return np.stack(masks).astype(np.float32)


def _split_row_bcast(x, hs, row):
    c, w = x.shape
    blk = 2 * hs
    if blk >= SUBLANES:
        xr = x.reshape(c // blk, blk, w)
        return jnp.broadcast_to(xr[:, hs - 1:hs, :], (c // blk, blk, w)).reshape(c, w)
    tiles = (c // SUBLANES, SUBLANES, w)
    x3, pos = x.reshape(tiles), (row & (blk - 1)).reshape(tiles)
    if hs == 1:
        out = jnp.where(pos == 1, pltpu.roll(x3, 1, 1), x3)
    else:
        assert hs == 2
        nxt = jnp.where((pos & 1) == 1, x3, pltpu.roll(x3, SUBLANES - 1, 1))
        out = jnp.where(pos >= 2, pltpu.roll(nxt, 2, 1), nxt)
    return out.reshape(c, w)


def _level_step(p, qf, kf, hs, row):
    c, w = p.shape
    if hs < SUBLANES:
        tot = _split_row_bcast(p, hs, row)
        second = (row & hs) != 0
        return jnp.where(second, p, tot - p), jnp.where(second, p + tot, p), jnp.where(second, qf, kf)
    xs, ps, qks = [], [], []
    for lo in range(0, c, 2 * hs):
        first, second = p[lo:lo + hs], p[lo + hs:lo + 2 * hs]
        tot = jnp.broadcast_to(first[hs - 1:hs], (hs, w))
        xs += [tot - first, second]
        ps += [first, second + tot]
        qks += [kf[lo:lo + hs], qf[lo + hs:lo + 2 * hs]]
    if hs % BF16_ROWS == 0:
        qk = jnp.concatenate(qks, axis=0)
    else:
        qk = jnp.where((row & hs) != 0, qf, kf)
    return jnp.concatenate(xs, axis=0), jnp.concatenate(ps, axis=0), qk


def _gla_chunk(rows, c, q_ref, k_ref, v_ref, la_ref, mask_ref, st_in, st_out, o_ref, *, state_t):
    levels = _gla_levels(c)
    n_lvl = len(levels)
    row = lax.broadcasted_iota(jnp.int32, (c, DK_HEAD_B), 0)
    for h in range(N_HEADS_B):
        hk = slice(h * DK_HEAD_B, (h + 1) * DK_HEAD_B)
        hv = slice(h * DV_HEAD_B, (h + 1) * DV_HEAD_B)
        qf = q_ref[rows, hk]
        kf = k_ref[rows, hk]
        vh = v_ref[rows, hv]
        p = la_ref[rows, hk]
        a = mask_ref[n_lvl] * _dot_nt(qf, kf)
        for l, hs in enumerate(levels):
            x, p, qk = _level_step(p, qf, kf, hs, row)
            z = qk * jnp.exp2(x).astype(BF16)
            a = a + mask_ref[l] * _dot_nt(z, z)
        b = p
        b_last = b[c - 1:c, :]
        o = _dot(a.astype(BF16), vh)
        st = st_in[h]
        qd = qf * jnp.exp2(b).astype(BF16)
        kd = kf * jnp.exp2(b_last - b).astype(BF16)
        d_last = jnp.exp2(b_last)
        if state_t:
            o = o + _dot_nt(qd, st.astype(BF16))
            st_out[h] = st * d_last + _dot_tn(vh, kd)
        else:
            o = o + _dot(qd, st.astype(BF16))
            d_col = jnp.broadcast_to(d_last, (SUBLANES, DK_HEAD_B)).T[:, 0:1]
            st_out[h] = st * d_col + _dot_tn(kd, vh)
        o_ref[rows, hv] = o.astype(BF16)


def _gla_stream_kernel(q_ref, k_ref, v_ref, la_ref, mask_ref, s0_ref,
                       o_ref, sout_ref, st_ref, *, c, n_chunks, unroll):
    @pl.when(pl.program_id(0) == 0)
    def _():
        for h in range(N_HEADS_B):
            st_ref[h] = s0_ref[h].T

    def chunk(ci, carry):
        rows = pl.ds(pl.multiple_of(ci * c, c), c)
        _gla_chunk(rows, c, q_ref, k_ref, v_ref, la_ref, mask_ref, st_ref, st_ref, o_ref, state_t=True)
        return carry

    lax.fori_loop(0, n_chunks, chunk, 0, unroll=unroll)

    @pl.when(pl.program_id(0) == pl.num_programs(0) - 1)
    def _():
        for h in range(N_HEADS_B):
            sout_ref[h] = st_ref[h].T


def _gla_seqs_kernel(q_ref, k_ref, v_ref, la_ref, mask_ref, s0_ref, o_ref, sout_ref, *, c, n_seqs):
    for j in range(n_seqs):
        _gla_chunk(slice(j * c, (j + 1) * c), c, q_ref, k_ref, v_ref, la_ref, mask_ref,
                   s0_ref.at[j], sout_ref.at[j], o_ref, state_t=False)


def _gla_stream(qb, kb, vb, la, s0, *, c, tb):
    t = qb.shape[0]
    masks = jnp.asarray(_gla_masks(c), F32)
    row = lambda w: pl.BlockSpec((tb, w), lambda i: (i, 0))
    st_shape = (N_HEADS_B, DK_HEAD_B, DV_HEAD_B)
    s_spec = pl.BlockSpec(st_shape, lambda i: (0, 0, 0))
    n_chunks = tb // c
    return pl.pallas_call(
        functools.partial(_gla_stream_kernel, c=c, n_chunks=n_chunks, unroll=min(GLA_UNROLL, n_chunks)),
        grid=(t // tb,),
        in_specs=[row(DK_B), row(DK_B), row(DV_B), row(DK_B), _const_spec(masks.shape), s_spec],
        out_specs=(row(DV_B), s_spec),
        out_shape=(jax.ShapeDtypeStruct((t, DV_B), BF16), jax.ShapeDtypeStruct(st_shape, F32)),
        scratch_shapes=[pltpu.VMEM((N_HEADS_B, DV_HEAD_B, DK_HEAD_B), F32)],
        compiler_params=_params(),
        name="gla_stream",
    )(qb, kb, vb, la, masks, s0)


def _gla_seqs(qb, kb, vb, la, s0, *, c):
    t = qb.shape[0]
    n = s0.shape[0]
    assert t == n * c and n % GLA_SEQS_PER_STEP == 0
    masks = jnp.asarray(_gla_masks(c), F32)
    tb = GLA_SEQS_PER_STEP * c
    row = lambda w: pl.BlockSpec((tb, w), lambda i: (i, 0))
    s_spec = pl.BlockSpec((GLA_SEQS_PER_STEP,) + s0.shape[1:], lambda i: (i, 0, 0, 0))
    return pl.pallas_call(
        functools.partial(_gla_seqs_kernel, c=c, n_seqs=GLA_SEQS_PER_STEP),
        grid=(n // GLA_SEQS_PER_STEP,),
        in_specs=[row(DK_B), row(DK_B), row(DV_B), row(DK_B), _const_spec(masks.shape), s_spec],
        out_specs=(row(DV_B), s_spec),
        out_shape=(jax.ShapeDtypeStruct((t, DV_B), BF16), jax.ShapeDtypeStruct(s0.shape, F32)),
        compiler_params=_params(),
        name="gla_seqs",
    )(qb, kb, vb, la, masks, s0)


def _merge_ffn_kernel(x_ref, oa_ref, ob_ref, rs_ref, ga_ref, gb_ref, gn_ref, gpost_ref, gpre_ref, gfpost_ref,
                      wpa_ref, wpb_ref, wout_ref, wg_ref, wu_ref, wd_ref, y_ref, x1_ref, h_ref, acc_ref):
    pa = _dot(oa_ref[...], wpa_ref[...])
    ob = jnp.concatenate(
        [_rms(ob_ref[:, h * DV_HEAD_B:(h + 1) * DV_HEAD_B].astype(F32)) * gn_ref[...] for h in range(N_HEADS_B)],
        axis=1) * rs_ref[...].astype(F32)
    pb = _dot(ob.astype(BF16), wpb_ref[...])
    mix = jax.nn.sigmoid(ga_ref[...].astype(F32)) * pa + jax.nn.sigmoid(gb_ref[...].astype(F32)) * pb
    x1 = x_ref[...] + _rms(_dot(mix.astype(BF16), wout_ref[...])) * gpost_ref[...]
    x1_ref[...] = x1
    h_ref[...] = (_rms(x1) * gpre_ref[...]).astype(BF16)
    acc_ref[...] = jnp.zeros_like(acc_ref)

    for j in range(N_FF_CHUNKS):
        cols = slice(j * FF_CHUNK, (j + 1) * FF_CHUNK)
        h = h_ref[...]
        g = _dot(h, wg_ref[:, cols])
        u = _dot(h, wu_ref[:, cols])
        acc_ref[...] += _dot((g * jax.nn.sigmoid(g) * u).astype(BF16), wd_ref[cols, :])
    y_ref[...] = x1_ref[...] + _rms(acc_ref[...]) * gfpost_ref[...]


def _merge_ffn(x, oa, ob, rs, ga, gb, gnorm, gpost, gpre, gfpost, wpa, wpb, wout, wg, wu, wd):
    t = x.shape[0]
    n = t // TM
    row = lambda w: pl.BlockSpec((TM, w), lambda i: (i, 0))
    vec = _const_spec((1, D_MODEL))
    return pl.pallas_call(
        _merge_ffn_kernel,
        grid=(n,),
        in_specs=[row(D_MODEL), row(D_A), row(DV_B), row(DV_B), row(D_MODEL), row(D_MODEL),
                  _const_spec((1, DV_HEAD_B)), vec, vec, vec,
                  _const_spec(wpa.shape), _const_spec(wpb.shape), _const_spec(wout.shape),
                  _const_spec(wg.shape), _const_spec(wu.shape), _const_spec(wd.shape)],
        out_specs=row(D_MODEL),
        out_shape=jax.ShapeDtypeStruct((t, D_MODEL), F32),
        scratch_shapes=[pltpu.VMEM((TM, D_MODEL), F32), pltpu.VMEM((TM, D_MODEL), BF16),
                        pltpu.VMEM((TM, D_MODEL), F32)],
        compiler_params=_params(),
        name="merge_ffn",
    )(x, oa, ob, rs, ga, gb, gnorm, gpost, gpre, gfpost, wpa, wpb, wout, wg, wu, wd)


def _rel_bias_row(table):
    n_hi = BAND_PAST + BIAS_ORIGIN - REL_CLIP
    n_lo = BIAS_F_LEN - n_hi - (2 * REL_CLIP + 1)
    h = table.shape[0]
    return jnp.concatenate([jnp.broadcast_to(table[:, -1:], (h, n_hi)), table[:, ::-1],
                            jnp.broadcast_to(table[:, :1], (h, n_lo))], axis=1)


def kernel(x_prompt, x_sample, cache_attn_k, cache_attn_v, state_gla, norm_mix_pre, norm_mix_post, norm_ffn_pre, norm_ffn_post, w_in, w_decay_up, b_decay, rel_bias, gla_norm, w_proj_a, w_proj_b, w_out, w_ffn_gate, w_ffn_up, w_ffn_down):
    depth = w_in.shape[0]
    assert depth == 1, "single-layer step"
    batch, seq, _ = x_prompt.shape
    dec_batch, dec_seq, _ = x_sample.shape
    assert batch == 1 and seq % TM == 0 and (dec_batch * dec_seq) % TM == 0
    assert seq % (ATT_SUB * ATT_QB) == 0
    past = cache_attn_k.shape[2]

    wt_bf = jnp.transpose(w_in[0]).astype(BF16)
    wup = w_decay_up[0].astype(BF16)
    bdec = b_decay[0][None, :]
    vec = lambda a: a[0][None, :]
    wpa, wpb, wout = w_proj_a[0].astype(BF16), w_proj_b[0].astype(BF16), w_out[0].astype(BF16)
    wg, wu, wd = w_ffn_gate[0].astype(BF16), w_ffn_up[0].astype(BF16), w_ffn_down[0].astype(BF16)
    gnorm = gla_norm[0][None, :]

    def layer_tail(x, oa, ob, rs, ga, gb):
        return _merge_ffn(x, oa, ob, rs, ga, gb, gnorm, vec(norm_mix_post), vec(norm_ffn_pre), vec(norm_ffn_post),
                          wpa, wpb, wout, wg, wu, wd)

    xp = x_prompt[0]
    qa, ka, va, qb, kb, vb, rs, la, ga, gb, kf, vf = _in_proj(
        xp, vec(norm_mix_pre), wt_bf, wup, bdec, kv_rows_every_step=False)
    f_tab = _rel_bias_row(rel_bias[0])
    oa = _attn_prompt(qa, ka, va, f_tab)
    s0 = jnp.zeros((N_HEADS_B, DK_HEAD_B, DV_HEAD_B), F32)
    ob, sp = _gla_stream(qb, kb, vb, la, s0, c=GLA_CHUNK, tb=GLA_TB)
    yp = layer_tail(xp, oa, ob, rs, ga, gb)
    keep = min(BAND_PAST, seq)
    assert keep == TM

    xs = x_sample.reshape(dec_batch * dec_seq, D_MODEL)
    qa, ka, va, qb, kb, vb, rs, la, ga, gb, kfs, vfs = _in_proj(
        xs, vec(norm_mix_pre), wt_bf, wup, bdec, kv_rows_every_step=True)
    assert past == BAND_PAST
    oa = _attn_sample(qa, kfs, vfs, jnp.transpose(cache_attn_k[0], (0, 2, 3, 1)),
                      jnp.transpose(cache_attn_v[0], (0, 2, 3, 1)), f_tab, seq=dec_seq)
    gla_chunk = CHUNK if dec_seq % CHUNK == 0 else dec_seq
    assert gla_chunk == dec_seq
    ob, ss = _gla_seqs(qb, kb, vb, la, state_gla[0], c=gla_chunk)
    ys = layer_tail(xs, oa, ob, rs, ga, gb)

    hd = (N_HEADS_A, HEAD_DIM_A)
    return (yp[None], ys.reshape(dec_batch, dec_seq, D_MODEL),
            kf.reshape((1, 1, keep) + hd), vf.reshape((1, 1, keep) + hd), sp[None, None],
            kfs.reshape((1, dec_batch, dec_seq) + hd), vfs.reshape((1, dec_batch, dec_seq) + hd), ss[None])
```

```python
import functools

import numpy as np
import jax
import jax.numpy as jnp
from jax import lax
from jax.experimental import pallas as pl
from jax.experimental.pallas import tpu as pltpu

F32 = jnp.float32
BF16 = jnp.bfloat16

D_MODEL = 1024
CHUNK = 64
BAND_CHUNKS = 8
BAND_PAST = BAND_CHUNKS * CHUNK
N_HEADS_A = 8
HEAD_DIM_A = 64
D_A = N_HEADS_A * HEAD_DIM_A
REL_CLIP = 128
N_HEADS_B = 4
DK_HEAD_B = 128
DV_HEAD_B = 256
DK_B = N_HEADS_B * DK_HEAD_B
DV_B = N_HEADS_B * DV_HEAD_B
GATE_RANK = 16
GATE_TEMP = 16.0
D_FF = 2816
EPS = 1e-6

LANES = 128
SUBLANES = 8
BF16_ROWS = 16
LOG2E = 1.4426950408889634
VMEM_LIMIT_BYTES = 56 * 1024 * 1024
NEG_BIG = -1e30

_PIECES = (("qa", D_A), ("ka", D_A), ("va", D_A), ("qb", DK_B), ("kb", DK_B), ("vb", DV_B),
           ("rb", DV_B), ("dlr", GATE_RANK), ("ga", D_MODEL), ("gb", D_MODEL))
_OFF = {}
_o = 0
for _n, _w in _PIECES:
    _OFF[_n] = (_o, _o + _w)
    _o += _w
D_IN = _o
assert all(lo % BF16_ROWS == 0 for lo, _ in _OFF.values())

TM = 512
ATT_QB = 256
ATT_WIN = ATT_QB + BAND_PAST
ATT_SUB = 4
ATT_MASKED = BAND_PAST // ATT_QB
assert ATT_MASKED <= ATT_SUB
BIAS_ORIGIN = ATT_QB
BIAS_F_LEN = 1024
GLA_TB = 1024
GLA_CHUNK = 128
GLA_UNROLL = 1
GLA_SEQS_PER_STEP = 4
FF_CHUNK = 256
N_FF_CHUNKS = D_FF // FF_CHUNK


def _const_spec(shape):
    nd = len(shape)
    return pl.BlockSpec(shape, lambda i: (0,) * nd, pipeline_mode=pl.Buffered(1))


def _params():
    return pltpu.CompilerParams(dimension_semantics=("arbitrary",), vmem_limit_bytes=VMEM_LIMIT_BYTES)


def _rms(x):
    return x * lax.rsqrt(jnp.mean(x * x, axis=-1, keepdims=True) + EPS)


def _dot(a, b):
    return jnp.dot(a, b, preferred_element_type=F32)


def _dot_nt(a, b):
    return lax.dot_general(a, b, (((1,), (1,)), ((), ())), preferred_element_type=F32)


def _dot_tn(a, b):
    return lax.dot_general(a, b, (((0,), (0,)), ((), ())), preferred_element_type=F32)


def _in_proj_kernel(x_ref, g_ref, wt_ref, wup_ref, bdec_ref,
                    qa_ref, ka_ref, va_ref, qb_ref, kb_ref, vb_ref, rs_ref, la_ref, ga_ref, gb_ref,
                    kf_ref, vf_ref, *, kv_rows_every_step):
    h = (_rms(x_ref[...]) * g_ref[...]).astype(BF16)

    def proj(name):
        lo, hi = _OFF[name]
        return _dot_nt(h, wt_ref[lo:hi, :])

    z = _dot(proj("dlr").astype(BF16), wup_ref[...]) + bdec_ref[...]
    la_ref[...] = (jnp.minimum(z, 0.0) - jnp.log1p(jnp.exp(-jnp.abs(z)))) * (LOG2E / GATE_TEMP)
    qa_ref[...] = (proj("qa") * (HEAD_DIM_A ** -0.5 * LOG2E)).astype(BF16)
    ka = proj("ka")
    va = proj("va")
    ka_ref[...] = ka.astype(BF16)
    va_ref[...] = va.astype(BF16)
    if kv_rows_every_step:
        kf_ref[...] = ka
        vf_ref[...] = va
    else:
        @pl.when(pl.program_id(0) == pl.num_programs(0) - 1)
        def _():
            kf_ref[...] = ka
            vf_ref[...] = va
    qb_ref[...] = (proj("qb") * (DK_HEAD_B ** -0.5)).astype(BF16)
    kb_ref[...] = proj("kb").astype(BF16)
    vb_ref[...] = proj("vb").astype(BF16)
    r = proj("rb")
    rs_ref[...] = (r * jax.nn.sigmoid(r)).astype(BF16)
    ga_ref[...] = proj("ga").astype(BF16)
    gb_ref[...] = proj("gb").astype(BF16)


def _in_proj(x, g, wt_bf, wup, bdec, *, kv_rows_every_step):
    t = x.shape[0]
    n = t // TM
    row = lambda w: pl.BlockSpec((TM, w), lambda i: (i, 0))
    if kv_rows_every_step:
        kv_rows, kv_spec = t, row(D_A)
    else:
        kv_rows, kv_spec = TM, pl.BlockSpec((TM, D_A), lambda i: (0, 0))
    widths = (D_A, D_A, D_A, DK_B, DK_B, DV_B, DV_B, DK_B, D_MODEL, D_MODEL)
    dtypes = (BF16,) * 7 + (F32, BF16, BF16)
    out_shape = tuple(jax.ShapeDtypeStruct((t, w), d) for w, d in zip(widths, dtypes))
    out_shape += (jax.ShapeDtypeStruct((kv_rows, D_A), F32),) * 2
    out_specs = tuple(row(w) for w in widths) + (kv_spec, kv_spec)
    return pl.pallas_call(
        functools.partial(_in_proj_kernel, kv_rows_every_step=kv_rows_every_step),
        grid=(n,),
        in_specs=[row(D_MODEL), _const_spec((1, D_MODEL)), _const_spec((D_IN, D_MODEL)),
                  _const_spec((GATE_RANK, DK_B)), _const_spec((1, DK_B))],
        out_specs=out_specs,
        out_shape=out_shape,
        compiler_params=_params(),
        name="in_proj",
    )(x, g, wt_bf, wup, bdec)


def _softmax_pv(s, v):
    m = jnp.max(s, axis=-1, keepdims=True)
    e = jnp.exp2(s - m).astype(BF16)
    pv = _dot(e, jnp.concatenate([v, jnp.ones_like(v)], axis=1))
    return pv[:, :LANES] / pv[:, LANES:LANES + 1]


def _head_pair_attention(q_pair, k_pair, v_pair, bias0, bias1):
    nq = q_pair.shape[0]
    lane = lax.broadcasted_iota(jnp.int32, (1, LANES), 1)
    first = lane < HEAD_DIM_A
    zero = jnp.zeros_like(q_pair)
    qs = jnp.concatenate([jnp.where(first, q_pair, zero), jnp.where(first, zero, q_pair)], axis=0)
    s = _dot_nt(qs, k_pair) + jnp.concatenate([bias0, bias1], axis=0)
    pv = _softmax_pv(s, v_pair)
    return jnp.where(first, pv[:nq], pv[nq:])


def _expand_rel_bias(f_ref, bias_ref, n_q, n_k, first_valid, n_valid, band):
    t = lax.broadcasted_iota(jnp.int32, (n_q, n_k), 0)
    k = lax.broadcasted_iota(jnp.int32, (n_q, n_k), 1)
    keep = (k < n_valid) & (k >= first_valid)
    if band:
        jq, jk = t // CHUNK, k // CHUNK
        keep = keep & (jk >= jq) & (jk <= jq + BAND_CHUNKS)
    for h in range(N_HEADS_A):
        rows = jnp.broadcast_to(f_ref[h:h + 1, :], (n_q, BIAS_F_LEN))
        rolled = pltpu.roll(rows, BIAS_F_LEN - BIAS_ORIGIN, 1, stride=1, stride_axis=0)
        bias_ref[h] = jnp.where(keep, rolled[:, :n_k] * LOG2E, NEG_BIG)


def _attn_prompt_kernel(q_ref, *refs):
    n_kb = ATT_SUB + ATT_WIN // ATT_QB - 1
    k_refs, v_refs = refs[:n_kb], refs[n_kb:2 * n_kb]
    f_ref, o_ref, bias_ref = refs[2 * n_kb:]
    i = pl.program_id(0)

    @pl.when(i == 0)
    def _():
        _expand_rel_bias(f_ref, bias_ref.at[0], ATT_QB, ATT_WIN, 0, ATT_WIN, band=True)
        for g in range(ATT_MASKED):
            _expand_rel_bias(f_ref, bias_ref.at[1 + g], ATT_QB, ATT_WIN, BAND_PAST - ATT_QB * g, ATT_WIN,
                             band=True)

    for sub in range(ATT_SUB):
        kwin = jnp.concatenate([r[...] for r in k_refs[sub:sub + ATT_WIN // ATT_QB]], axis=0)
        vwin = jnp.concatenate([r[...] for r in v_refs[sub:sub + ATT_WIN // ATT_QB]], axis=0)
        rows = slice(sub * ATT_QB, (sub + 1) * ATT_QB)
        which = jnp.where(i == 0, 1 + sub, 0) if sub < ATT_MASKED else 0
        for p in range(N_HEADS_A // 2):
            sl = slice(p * LANES, (p + 1) * LANES)
            o = _head_pair_attention(q_ref[rows, sl], kwin[:, sl], vwin[:, sl],
                                     bias_ref[which, 2 * p], bias_ref[which, 2 * p + 1])
            o_ref[rows, sl] = o.astype(BF16)


def _attn_prompt(qa, ka, va, f_tab):
    t = qa.shape[0]
    n = t // (ATT_SUB * ATT_QB)
    n_kb = ATT_SUB + ATT_WIN // ATT_QB - 1
    kblk = [pl.BlockSpec((ATT_QB, D_A), functools.partial(
        lambda j, i: (jnp.maximum(ATT_SUB * i + j - (n_kb - ATT_SUB), 0), 0), j)) for j in range(n_kb)]
    qblk = pl.BlockSpec((ATT_SUB * ATT_QB, D_A), lambda i: (i, 0))
    return pl.pallas_call(
        _attn_prompt_kernel,
        grid=(n,),
        in_specs=[qblk] + kblk + kblk + [_const_spec((N_HEADS_A, BIAS_F_LEN))],
        out_specs=qblk,
        out_shape=jax.ShapeDtypeStruct((t, D_A), BF16),
        scratch_shapes=[pltpu.VMEM((1 + ATT_MASKED, N_HEADS_A, ATT_QB, ATT_WIN), F32)],
        compiler_params=_params(),
        name="attn_prompt",
    )(qa, *([ka] * n_kb), *([va] * n_kb), f_tab)


def _attn_sample_kernel(q_ref, kt_ref, vt_ref, kn_ref, vn_ref, f_ref, o_ref, bias_ref):
    n_q, w = q_ref.shape[0], kt_ref.shape[-1]

    @pl.when(pl.program_id(0) == 0)
    def _():
        _expand_rel_bias(f_ref, bias_ref, n_q, bias_ref.shape[-1], 0, w + n_q, band=False)

    lane = lax.broadcasted_iota(jnp.int32, (1, LANES), 1)
    first = lane < HEAD_DIM_A
    for p in range(N_HEADS_A // 2):
        sl = slice(p * LANES, (p + 1) * LANES)
        q_pair = q_ref[:, sl]
        zero = jnp.zeros_like(q_pair)
        qs = jnp.concatenate([jnp.where(first, q_pair, zero), jnp.where(first, zero, q_pair)], axis=0)
        kt = jnp.concatenate([kt_ref[2 * p], kt_ref[2 * p + 1]], axis=0).astype(BF16)
        vt = jnp.concatenate([vt_ref[2 * p], vt_ref[2 * p + 1]], axis=0).astype(BF16)
        kn = kn_ref[:, sl].astype(BF16)
        vn = vn_ref[:, sl].astype(BF16)
        bias = jnp.concatenate([bias_ref[2 * p], bias_ref[2 * p + 1]], axis=0)
        s_c = _dot(qs, kt) + bias[:, :w]
        s_n = _dot_nt(qs, kn) + bias[:, w:w + n_q]
        m = jnp.maximum(jnp.max(s_c, axis=-1, keepdims=True), jnp.max(s_n, axis=-1, keepdims=True))
        e_c = jnp.exp2(s_c - m).astype(BF16)
        e_n = jnp.exp2(s_n - m).astype(BF16)
        pv = (_dot_nt(e_c, jnp.concatenate([vt, jnp.ones_like(vt)], axis=0))
              + _dot(e_n, jnp.concatenate([vn, jnp.ones_like(vn)], axis=1)))
        pv = pv[:, :LANES] / pv[:, LANES:LANES + 1]
        o_ref[:, sl] = jnp.where(first, pv[:n_q], pv[n_q:]).astype(BF16)


def _attn_sample(qa, kn, vn, cache_kt, cache_vt, f_tab, *, seq):
    t = qa.shape[0]
    nb, _, _, w = cache_kt.shape
    n_k = w + seq
    row = pl.BlockSpec((seq, D_A), lambda i: (i, 0))
    cache = pl.BlockSpec((None, N_HEADS_A, HEAD_DIM_A, w), lambda i: (i, 0, 0, 0))
    return pl.pallas_call(
        _attn_sample_kernel,
        grid=(nb,),
        in_specs=[row, cache, cache, row, row, _const_spec((N_HEADS_A, BIAS_F_LEN))],
        out_specs=row,
        out_shape=jax.ShapeDtypeStruct((t, D_A), BF16),
        scratch_shapes=[pltpu.VMEM((N_HEADS_A, seq, n_k + (-n_k) % LANES), F32)],
        compiler_params=_params(),
        name="attn_sample",
    )(qa, cache_kt, cache_vt, kn, vn, f_tab)


def _gla_levels(c):
    out, hs = [], 1
    while hs < c:
        out.append(hs)
        hs *= 2
    return out


def _gla_masks(c):
    t = np.arange(c)
    masks = []
    for hs in _gla_levels(c):
        blk = t // (2 * hs)
        second = (t % (2 * hs)) >= hs
        masks.append((blk[:, None] == blk[None, :]) & second[:, None] & ~second[None, :])
    masks.append(np.eye(c, dtype=bool))
    return np.stack(masks).astype(np.float32)


def _split_row_bcast(x, hs, row):
    c, w = x.shape
    blk = 2 * hs
    if blk >= SUBLANES:
        xr = x.reshape(c // blk, blk, w)
        return jnp.broadcast_to(xr[:, hs - 1:hs, :], (c // blk, blk, w)).reshape(c, w)
    tiles = (c // SUBLANES, SUBLANES, w)
    x3, pos = x.reshape(tiles), (row & (blk - 1)).reshape(tiles)
    if hs == 1:
        out = jnp.where(pos == 1, pltpu.roll(x3, 1, 1), x3)
    else:
        assert hs == 2
        nxt = jnp.where((pos & 1) == 1, x3, pltpu.roll(x3, SUBLANES - 1, 1))
        out = jnp.where(pos >= 2, pltpu.roll(nxt, 2, 1), nxt)
    return out.reshape(c, w)


def _level_step(p, qf, kf, hs, row):
    c, w = p.shape
    if hs < SUBLANES:
        tot = _split_row_bcast(p, hs, row)
        second = (row & hs) != 0
        return jnp.where(second, p, tot - p), jnp.where(second, p + tot, p), jnp.where(second, qf, kf)
    xs, ps, qks = [], [], []
    for lo in range(0, c, 2 * hs):
        first, second = p[lo:lo + hs], p[lo + hs:lo + 2 * hs]
        tot = jnp.broadcast_to(first[hs - 1:hs], (hs, w))
        xs += [tot - first, second]
        ps += [first, second + tot]
        qks += [kf[lo:lo + hs], qf[lo + hs:lo + 2 * hs]]
    if hs % BF16_ROWS == 0:
        qk = jnp.concatenate(qks, axis=0)
    else:
        qk = jnp.where((row & hs) != 0, qf, kf)
    return jnp.concatenate(xs, axis=0), jnp.concatenate(ps, axis=0), qk


def _gla_chunk(rows, c, q_ref, k_ref, v_ref, la_ref, mask_ref, st_in, st_out, o_ref, *, state_t):
    levels = _gla_levels(c)
    n_lvl = len(levels)
    row = lax.broadcasted_iota(jnp.int32, (c, DK_HEAD_B), 0)
    for h in range(N_HEADS_B):
        hk = slice(h * DK_HEAD_B, (h + 1) * DK_HEAD_B)
        hv = slice(h * DV_HEAD_B, (h + 1) * DV_HEAD_B)
        qf = q_ref[rows, hk]
        kf = k_ref[rows, hk]
        vh = v_ref[rows, hv]
        p = la_ref[rows, hk]
        a = mask_ref[n_lvl] * _dot_nt(qf, kf)
        for l, hs in enumerate(levels):
            x, p, qk = _level_step(p, qf, kf, hs, row)
            z = qk * jnp.exp2(x).astype(BF16)
            a = a + mask_ref[l] * _dot_nt(z, z)
        b = p
        b_last = b[c - 1:c, :]
        o = _dot(a.astype(BF16), vh)
        st = st_in[h]
        qd = qf * jnp.exp2(b).astype(BF16)
        kd = kf * jnp.exp2(b_last - b).astype(BF16)
        d_last = jnp.exp2(b_last)
        if state_t:
            o = o + _dot_nt(qd, st.astype(BF16))
            st_out[h] = st * d_last + _dot_tn(vh, kd)
        else:
            o = o + _dot(qd, st.astype(BF16))
            d_col = jnp.broadcast_to(d_last, (SUBLANES, DK_HEAD_B)).T[:, 0:1]
            st_out[h] = st * d_col + _dot_tn(kd, vh)
        o_ref[rows, hv] = o.astype(BF16)


def _gla_stream_kernel(q_ref, k_ref, v_ref, la_ref, mask_ref, s0_ref,
                       o_ref, sout_ref, st_ref, *, c, n_chunks, unroll):
    @pl.when(pl.program_id(0) == 0)
    def _():
        for h in range(N_HEADS_B):
            st_ref[h] = s0_ref[h].T

    def chunk(ci, carry):
        rows = pl.ds(pl.multiple_of(ci * c, c), c)
        _gla_chunk(rows, c, q_ref, k_ref, v_ref, la_ref, mask_ref, st_ref, st_ref, o_ref, state_t=True)
        return carry

    lax.fori_loop(0, n_chunks, chunk, 0, unroll=unroll)

    @pl.when(pl.program_id(0) == pl.num_programs(0) - 1)
    def _():
        for h in range(N_HEADS_B):
            sout_ref[h] = st_ref[h].T


def _gla_seqs_kernel(q_ref, k_ref, v_ref, la_ref, mask_ref, s0_ref, o_ref, sout_ref, *, c, n_seqs):
    for j in range(n_seqs):
        _gla_chunk(slice(j * c, (j + 1) * c), c, q_ref, k_ref, v_ref, la_ref, mask_ref,
                   s0_ref.at[j], sout_ref.at[j], o_ref, state_t=False)


def _gla_stream(qb, kb, vb, la, s0, *, c, tb):
    t = qb.shape[0]
    masks = jnp.asarray(_gla_masks(c), F32)
    row = lambda w: pl.BlockSpec((tb, w), lambda i: (i, 0))
    st_shape = (N_HEADS_B, DK_HEAD_B, DV_HEAD_B)
    s_spec = pl.BlockSpec(st_shape, lambda i: (0, 0, 0))
    n_chunks = tb // c
    return pl.pallas_call(
        functools.partial(_gla_stream_kernel, c=c, n_chunks=n_chunks, unroll=min(GLA_UNROLL, n_chunks)),
        grid=(t // tb,),
        in_specs=[row(DK_B), row(DK_B), row(DV_B), row(DK_B), _const_spec(masks.shape), s_spec],
        out_specs=(row(DV_B), s_spec),
        out_shape=(jax.ShapeDtypeStruct((t, DV_B), BF16), jax.ShapeDtypeStruct(st_shape, F32)),
        scratch_shapes=[pltpu.VMEM((N_HEADS_B, DV_HEAD_B, DK_HEAD_B), F32)],
        compiler_params=_params(),
        name="gla_stream",
    )(qb, kb, vb, la, masks, s0)


def _gla_seqs(qb, kb, vb, la, s0, *, c):
    t = qb.shape[0]
    n = s0.shape[0]
    assert t == n * c and n % GLA_SEQS_PER_STEP == 0
    masks = jnp.asarray(_gla_masks(c), F32)
    tb = GLA_SEQS_PER_STEP * c
    row = lambda w: pl.BlockSpec((tb, w), lambda i: (i, 0))
    s_spec = pl.BlockSpec((GLA_SEQS_PER_STEP,) + s0.shape[1:], lambda i: (i, 0, 0, 0))
    return pl.pallas_call(
        functools.partial(_gla_seqs_kernel, c=c, n_seqs=GLA_SEQS_PER_STEP),
        grid=(n // GLA_SEQS_PER_STEP,),
        in_specs=[row(DK_B), row(DK_B), row(DV_B), row(DK_B), _const_spec(masks.shape), s_spec],
        out_specs=(row(DV_B), s_spec),
        out_shape=(jax.ShapeDtypeStruct((t, DV_B), BF16), jax.ShapeDtypeStruct(s0.shape, F32)),
        compiler_params=_params(),
        name="gla_seqs",
    )(qb, kb, vb, la, masks, s0)


def _merge_ffn_kernel(x_ref, oa_ref, ob_ref, rs_ref, ga_ref, gb_ref, gn_ref, gpost_ref, gpre_ref, gfpost_ref,
                      wpa_ref, wpb_ref, wout_ref, wg_ref, wu_ref, wd_ref, y_ref, x1_ref, h_ref, acc_ref):
    pa = _dot(oa_ref[...], wpa_ref[...])
    ob = jnp.concatenate(
        [_rms(ob_ref[:, h * DV_HEAD_B:(h + 1) * DV_HEAD_B].astype(F32)) * gn_ref[...] for h in range(N_HEADS_B)],
        axis=1) * rs_ref[...].astype(F32)
    pb = _dot(ob.astype(BF16), wpb_ref[...])
    mix = jax.nn.sigmoid(ga_ref[...].astype(F32)) * pa + jax.nn.sigmoid(gb_ref[...].astype(F32)) * pb
    x1 = x_ref[...] + _rms(_dot(mix.astype(BF16), wout_ref[...])) * gpost_ref[...]
    x1_ref[...] = x1
    h_ref[...] = (_rms(x1) * gpre_ref[...]).astype(BF16)
    acc_ref[...] = jnp.zeros_like(acc_ref)

    for j in range(N_FF_CHUNKS):
        cols = slice(j * FF_CHUNK, (j + 1) * FF_CHUNK)
        h = h_ref[...]
        g = _dot(h, wg_ref[:, cols])
        u = _dot(h, wu_ref[:, cols])
        acc_ref[...] += _dot((g * jax.nn.sigmoid(g) * u).astype(BF16), wd_ref[cols, :])
    y_ref[...] = x1_ref[...] + _rms(acc_ref[...]) * gfpost_ref[...]


def _merge_ffn(x, oa, ob, rs, ga, gb, gnorm, gpost, gpre, gfpost, wpa, wpb, wout, wg, wu, wd):
    t = x.shape[0]
    n = t // TM
    row = lambda w: pl.BlockSpec((TM, w), lambda i: (i, 0))
    vec = _const_spec((1, D_MODEL))
    return pl.pallas_call(
        _merge_ffn_kernel,
        grid=(n,),
        in_specs=[row(D_MODEL), row(D_A), row(DV_B), row(DV_B), row(D_MODEL), row(D_MODEL),
                  _const_spec((1, DV_HEAD_B)), vec, vec, vec,
                  _const_spec(wpa.shape), _const_spec(wpb.shape), _const_spec(wout.shape),
                  _const_spec(wg.shape), _const_spec(wu.shape), _const_spec(wd.shape)],
        out_specs=row(D_MODEL),
        out_shape=jax.ShapeDtypeStruct((t, D_MODEL), F32),
        scratch_shapes=[pltpu.VMEM((TM, D_MODEL), F32), pltpu.VMEM((TM, D_MODEL), BF16),
                        pltpu.VMEM((TM, D_MODEL), F32)],
        compiler_params=_params(),
        name="merge_ffn",
    )(x, oa, ob, rs, ga, gb, gnorm, gpost, gpre, gfpost, wpa, wpb, wout, wg, wu, wd)


def _rel_bias_row(table):
    n_hi = BAND_PAST + BIAS_ORIGIN - REL_CLIP
    n_lo = BIAS_F_LEN - n_hi - (2 * REL_CLIP + 1)
    h = table.shape[0]
    return jnp.concatenate([jnp.broadcast_to(table[:, -1:], (h, n_hi)), table[:, ::-1],
                            jnp.broadcast_to(table[:, :1], (h, n_lo))], axis=1)


def kernel(x_prompt, x_sample, cache_attn_k, cache_attn_v, state_gla, norm_mix_pre, norm_mix_post, norm_ffn_pre, norm_ffn_post, w_in, w_decay_up, b_decay, rel_bias, gla_norm, w_proj_a, w_proj_b, w_out, w_ffn_gate, w_ffn_up, w_ffn_down):
    depth = w_in.shape[0]
    assert depth == 1, "single-layer step"
    batch, seq, _ = x_prompt.shape
    dec_batch, dec_seq, _ = x_sample.shape
    assert batch == 1 and seq % TM == 0 and (dec_batch * dec_seq) % TM == 0
    assert seq % (ATT_SUB * ATT_QB) == 0
    past = cache_attn_k.shape[2]

    wt_bf = jnp.transpose(w_in[0]).astype(BF16)
    wup = w_decay_up[0].astype(BF16)
    bdec = b_decay[0][None, :]
    vec = lambda a: a[0][None, :]
    wpa, wpb, wout = w_proj_a[0].astype(BF16), w_proj_b[0].astype(BF16), w_out[0].astype(BF16)
    wg, wu, wd = w_ffn_gate[0].astype(BF16), w_ffn_up[0].astype(BF16), w_ffn_down[0].astype(BF16)
    gnorm = gla_norm[0][None, :]

    def layer_tail(x, oa, ob, rs, ga, gb):
        return _merge_ffn(x, oa, ob, rs, ga, gb, gnorm, vec(norm_mix_post), vec(norm_ffn_pre), vec(norm_ffn_post),
                          wpa, wpb, wout, wg, wu, wd)

    xp = x_prompt[0]
    qa, ka, va, qb, kb, vb, rs, la, ga, gb, kf, vf = _in_proj(
        xp, vec(norm_mix_pre), wt_bf, wup, bdec, kv_rows_every_step=False)
    f_tab = _rel_bias_row(rel_bias[0])
    oa = _attn_prompt(qa, ka, va, f_tab)
    s0 = jnp.zeros((N_HEADS_B, DK_HEAD_B, DV_HEAD_B), F32)
    ob, sp = _gla_stream(qb, kb, vb, la, s0, c=GLA_CHUNK, tb=GLA_TB)
    yp = layer_tail(xp, oa, ob, rs, ga, gb)
    keep = min(BAND_PAST, seq)
    assert keep == TM

    xs = x_sample.reshape(dec_batch * dec_seq, D_MODEL)
    qa, ka, va, qb, kb, vb, rs, la, ga, gb, kfs, vfs = _in_proj(
        xs, vec(norm_mix_pre), wt_bf, wup, bdec, kv_rows_every_step=True)
    assert past == BAND_PAST
    oa = _attn_sample(qa, kfs, vfs, jnp.transpose(cache_attn_k[0], (0, 2, 3, 1)),
                      jnp.transpose(cache_attn_v[0], (0, 2, 3, 1)), f_tab, seq=dec_seq)
    gla_chunk = CHUNK if dec_seq % CHUNK == 0 else dec_seq
    assert gla_chunk == dec_seq
    ob, ss = _gla_seqs(qb, kb, vb, la, state_gla[0], c=gla_chunk)
    ys = layer_tail(xs, oa, ob, rs, ga, gb)

    hd = (N_HEADS_A, HEAD_DIM_A)
    return (yp[None], ys.reshape(dec_batch, dec_seq, D_MODEL),
            kf.reshape((1, 1, keep) + hd), vf.reshape((1, 1, keep) + hd), sp[None, None],
            kfs.reshape((1, dec_batch, dec_seq) + hd), vfs.reshape((1, dec_batch, dec_seq) + hd), ss[None])
```

```python
import functools

import numpy as np
import jax
import jax.numpy as jnp
from jax import lax
from jax.experimental import pallas as pl
from jax.experimental.pallas import tpu as pltpu

F32 = jnp.float32
BF16 = jnp.bfloat16

D_MODEL = 1024
CHUNK = 64
BAND_CHUNKS = 8
BAND_PAST = BAND_CHUNKS * CHUNK
N_HEADS_A = 8
HEAD_DIM_A = 64
D_A = N_HEADS_A * HEAD_DIM_A
REL_CLIP = 128
N_HEADS_B = 4
DK_HEAD_B = 128
DV_HEAD_B = 256
DK_B = N_HEADS_B * DK_HEAD_B
DV_B = N_HEADS_B * DV_HEAD_B
GATE_RANK = 16
GATE_TEMP = 16.0
D_FF = 2816
EPS = 1e-6

LANES = 128
SUBLANES = 8
BF16_ROWS = 16
LOG2E = 1.4426950408889634
VMEM_LIMIT_BYTES = 56 * 1024 * 1024
NEG_BIG = -1e30

_PIECES = (("qa", D_A), ("ka", D_A), ("va", D_A), ("qb", DK_B), ("kb", DK_B), ("vb", DV_B),
           ("rb", DV_B), ("dlr", GATE_RANK), ("ga", D_MODEL), ("gb", D_MODEL))
_OFF = {}
_o = 0
for _n, _w in _PIECES:
    _OFF[_n] = (_o, _o + _w)
    _o += _w
D_IN = _o
assert all(lo % BF16_ROWS == 0 for lo, _ in _OFF.values())

TM = 512
ATT_QB = 256
ATT_WIN = ATT_QB + BAND_PAST
ATT_SUB = 4
ATT_MASKED = BAND_PAST // ATT_QB
assert ATT_MASKED <= ATT_SUB
BIAS_ORIGIN = ATT_QB
BIAS_F_LEN = 1024
GLA_TB = 1024
GLA_CHUNK = 128
GLA_UNROLL = 4
GLA_SEQS_PER_STEP = 4
FF_CHUNK = 256
N_FF_CHUNKS = D_FF // FF_CHUNK


def _const_spec(shape):
    nd = len(shape)
    return pl.BlockSpec(shape, lambda i: (0,) * nd, pipeline_mode=pl.Buffered(1))


def _params():
    return pltpu.CompilerParams(dimension_semantics=("arbitrary",), vmem_limit_bytes=VMEM_LIMIT_BYTES)


def _rms(x):
    return x * lax.rsqrt(jnp.mean(x * x, axis=-1, keepdims=True) + EPS)


def _dot(a, b):
    return jnp.dot(a, b, preferred_element_type=F32)


def _dot_nt(a, b):
    return lax.dot_general(a, b, (((1,), (1,)), ((), ())), preferred_element_type=F32)


def _dot_tn(a, b):
    return lax.dot_general(a, b, (((0,), (0,)), ((), ())), preferred_element_type=F32)


def _in_proj_kernel(x_ref, g_ref, wt_ref, wup_ref, bdec_ref,
                    qa_ref, ka_ref, va_ref, qb_ref, kb_ref, vb_ref, rs_ref, la_ref, ga_ref, gb_ref,
                    kf_ref, vf_ref, *, kv_rows_every_step):
    h = (_rms(x_ref[...]) * g_ref[...]).astype(BF16)

    def proj(name):
        lo, hi = _OFF[name]
        return _dot_nt(h, wt_ref[lo:hi, :])

    z = _dot(proj("dlr").astype(BF16), wup_ref[...]) + bdec_ref[...]
    la_ref[...] = (jnp.minimum(z, 0.0) - jnp.log1p(jnp.exp(-jnp.abs(z)))) * (LOG2E / GATE_TEMP)
    qa_ref[...] = (proj("qa") * (HEAD_DIM_A ** -0.5 * LOG2E)).astype(BF16)
    ka = proj("ka")
    va = proj("va")
    ka_ref[...] = ka.astype(BF16)
    va_ref[...] = va.astype(BF16)
    if kv_rows_every_step:
        kf_ref[...] = ka
        vf_ref[...] = va
    else:
        @pl.when(pl.program_id(0) == pl.num_programs(0) - 1)
        def _():
            kf_ref[...] = ka
            vf_ref[...] = va
    qb_ref[...] = (proj("qb") * (DK_HEAD_B ** -0.5)).astype(BF16)
    kb_ref[...] = proj("kb").astype(BF16)
    vb_ref[...] = proj("vb").astype(BF16)
    r = proj("rb")
    rs_ref[...] = (r * jax.nn.sigmoid(r)).astype(BF16)
    ga_ref[...] = proj("ga").astype(BF16)
    gb_ref[...] = proj("gb").astype(BF16)


def _in_proj(x, g, wt_bf, wup, bdec, *, kv_rows_every_step):
    t = x.shape[0]
    n = t // TM
    row = lambda w: pl.BlockSpec((TM, w), lambda i: (i, 0))
    if kv_rows_every_step:
        kv_rows, kv_spec = t, row(D_A)
    else:
        kv_rows, kv_spec = TM, pl.BlockSpec((TM, D_A), lambda i: (0, 0))
    widths = (D_A, D_A, D_A, DK_B, DK_B, DV_B, DV_B, DK_B, D_MODEL, D_MODEL)
    dtypes = (BF16,) * 7 + (F32, BF16, BF16)
    out_shape = tuple(jax.ShapeDtypeStruct((t, w), d) for w, d in zip(widths, dtypes))
    out_shape += (jax.ShapeDtypeStruct((kv_rows, D_A), F32),) * 2
    out_specs = tuple(row(w) for w in widths) + (kv_spec, kv_spec)
    return pl.pallas_call(
        functools.partial(_in_proj_kernel, kv_rows_every_step=kv_rows_every_step),
        grid=(n,),
        in_specs=[row(D_MODEL), _const_spec((1, D_MODEL)), _const_spec((D_IN, D_MODEL)),
                  _const_spec((GATE_RANK, DK_B)), _const_spec((1, DK_B))],
        out_specs=out_specs,
        out_shape=out_shape,
        compiler_params=_params(),
        name="in_proj",
    )(x, g, wt_bf, wup, bdec)


def _softmax_pv(s, v):
    m = jnp.max(s, axis=-1, keepdims=True)
    e = jnp.exp2(s - m).astype(BF16)
    pv = _dot(e, jnp.concatenate([v, jnp.ones_like(v)], axis=1))
    return pv[:, :LANES] / pv[:, LANES:LANES + 1]


def _head_pair_attention(q_pair, k_pair, v_pair, bias0, bias1):
    nq = q_pair.shape[0]
    lane = lax.broadcasted_iota(jnp.int32, (1, LANES), 1)
    first = lane < HEAD_DIM_A
    zero = jnp.zeros_like(q_pair)
    qs = jnp.concatenate([jnp.where(first, q_pair, zero), jnp.where(first, zero, q_pair)], axis=0)
    s = _dot_nt(qs, k_pair) + jnp.concatenate([bias0, bias1], axis=0)
    pv = _softmax_pv(s, v_pair)
    return jnp.where(first, pv[:nq], pv[nq:])


def _expand_rel_bias(f_ref, bias_ref, n_q, n_k, first_valid, n_valid, band):
    t = lax.broadcasted_iota(jnp.int32, (n_q, n_k), 0)
    k = lax.broadcasted_iota(jnp.int32, (n_q, n_k), 1)
    keep = (k < n_valid) & (k >= first_valid)
    if band:
        jq, jk = t // CHUNK, k // CHUNK
        keep = keep & (jk >= jq) & (jk <= jq + BAND_CHUNKS)
    for h in range(N_HEADS_A):
        rows = jnp.broadcast_to(f_ref[h:h + 1, :], (n_q, BIAS_F_LEN))
        rolled = pltpu.roll(rows, BIAS_F_LEN - BIAS_ORIGIN, 1, stride=1, stride_axis=0)
        bias_ref[h] = jnp.where(keep, rolled[:, :n_k] * LOG2E, NEG_BIG)


def _attn_prompt_kernel(q_ref, *refs):
    n_kb = ATT_SUB + ATT_WIN // ATT_QB - 1
    k_refs, v_refs = refs[:n_kb], refs[n_kb:2 * n_kb]
    f_ref, o_ref, bias_ref = refs[2 * n_kb:]
    i = pl.program_id(0)

    @pl.when(i == 0)
    def _():
        _expand_rel_bias(f_ref, bias_ref.at[0], ATT_QB, ATT_WIN, 0, ATT_WIN, band=True)
        for g in range(ATT_MASKED):
            _expand_rel_bias(f_ref, bias_ref.at[1 + g], ATT_QB, ATT_WIN, BAND_PAST - ATT_QB * g, ATT_WIN,
                             band=True)

    for sub in range(ATT_SUB):
        kwin = jnp.concatenate([r[...] for r in k_refs[sub:sub + ATT_WIN // ATT_QB]], axis=0)
        vwin = jnp.concatenate([r[...] for r in v_refs[sub:sub + ATT_WIN // ATT_QB]], axis=0)
        rows = slice(sub * ATT_QB, (sub + 1) * ATT_QB)
        which = jnp.where(i == 0, 1 + sub, 0) if sub < ATT_MASKED else 0
        for p in range(N_HEADS_A // 2):
            sl = slice(p * LANES, (p + 1) * LANES)
            o = _head_pair_attention(q_ref[rows, sl], kwin[:, sl], vwin[:, sl],
                                     bias_ref[which, 2 * p], bias_ref[which, 2 * p + 1])
            o_ref[rows, sl] = o.astype(BF16)


def _attn_prompt(qa, ka, va, f_tab):
    t = qa.shape[0]
    n = t // (ATT_SUB * ATT_QB)
    n_kb = ATT_SUB + ATT_WIN // ATT_QB - 1
    kblk = [pl.BlockSpec((ATT_QB, D_A), functools.partial(
        lambda j, i: (jnp.maximum(ATT_SUB * i + j - (n_kb - ATT_SUB), 0), 0), j)) for j in range(n_kb)]
    qblk = pl.BlockSpec((ATT_SUB * ATT_QB, D_A), lambda i: (i, 0))
    return pl.pallas_call(
        _attn_prompt_kernel,
        grid=(n,),
        in_specs=[qblk] + kblk + kblk + [_const_spec((N_HEADS_A, BIAS_F_LEN))],
        out_specs=qblk,
        out_shape=jax.ShapeDtypeStruct((t, D_A), BF16),
        scratch_shapes=[pltpu.VMEM((1 + ATT_MASKED, N_HEADS_A, ATT_QB, ATT_WIN), F32)],
        compiler_params=_params(),
        name="attn_prompt",
    )(qa, *([ka] * n_kb), *([va] * n_kb), f_tab)


def _attn_sample_kernel(q_ref, kt_ref, vt_ref, kn_ref, vn_ref, f_ref, o_ref, bias_ref):
    n_q, w = q_ref.shape[0], kt_ref.shape[-1]

    @pl.when(pl.program_id(0) == 0)
    def _():
        _expand_rel_bias(f_ref, bias_ref, n_q, bias_ref.shape[-1], 0, w + n_q, band=False)

    lane = lax.broadcasted_iota(jnp.int32, (1, LANES), 1)
    first = lane < HEAD_DIM_A
    for p in range(N_HEADS_A // 2):
        sl = slice(p * LANES, (p + 1) * LANES)
        q_pair = q_ref[:, sl]
        zero = jnp.zeros_like(q_pair)
        qs = jnp.concatenate([jnp.where(first, q_pair, zero), jnp.where(first, zero, q_pair)], axis=0)
        kt = jnp.concatenate([kt_ref[2 * p], kt_ref[2 * p + 1]], axis=0).astype(BF16)
        vt = jnp.concatenate([vt_ref[2 * p], vt_ref[2 * p + 1]], axis=0).astype(BF16)
        kn = kn_ref[:, sl].astype(BF16)
        vn = vn_ref[:, sl].astype(BF16)
        bias = jnp.concatenate([bias_ref[2 * p], bias_ref[2 * p + 1]], axis=0)
        s_c = _dot(qs, kt) + bias[:, :w]
        s_n = _dot_nt(qs, kn) + bias[:, w:w + n_q]
        m = jnp.maximum(jnp.max(s_c, axis=-1, keepdims=True), jnp.max(s_n, axis=-1, keepdims=True))
        e_c = jnp.exp2(s_c - m).astype(BF16)
        e_n = jnp.exp2(s_n - m).astype(BF16)
        pv = (_dot_nt(e_c, jnp.concatenate([vt, jnp.ones_like(vt)], axis=0))
              + _dot(e_n, jnp.concatenate([vn, jnp.ones_like(vn)], axis=1)))
        pv = pv[:, :LANES] / pv[:, LANES:LANES + 1]
        o_ref[:, sl] = jnp.where(first, pv[:n_q], pv[n_q:]).astype(BF16)


def _attn_sample(qa, kn, vn, cache_kt, cache_vt, f_tab, *, seq):
    t = qa.shape[0]
    nb, _, _, w = cache_kt.shape
    n_k = w + seq
    row = pl.BlockSpec((seq, D_A), lambda i: (i, 0))
    cache = pl.BlockSpec((None, N_HEADS_A, HEAD_DIM_A, w), lambda i: (i, 0, 0, 0))
    return pl.pallas_call(
        _attn_sample_kernel,
        grid=(nb,),
        in_specs=[row, cache, cache, row, row, _const_spec((N_HEADS_A, BIAS_F_LEN))],
        out_specs=row,
        out_shape=jax.ShapeDtypeStruct((t, D_A), BF16),
        scratch_shapes=[pltpu.VMEM((N_HEADS_A, seq, n_k + (-n_k) % LANES), F32)],
        compiler_params=_params(),
        name="attn_sample",
    )(qa, cache_kt, cache_vt, kn, vn, f_tab)


def _gla_levels(c):
    out, hs = [], 1
    while hs < c:
        out.append(hs)
        hs *= 2
    return out


def _gla_masks(c):
    t = np.arange(c)
    masks = []
    for hs in _gla_levels(c):
        blk = t // (2 * hs)
        second = (t % (2 * hs)) >= hs
        masks.append((blk[:, None] == blk[None, :]) & second[:, None] & ~second[None, :])
    masks.append(np.eye(c, dtype=bool))
    return np.stack(masks).astype(np.float32)


def _split_row_bcast(x, hs, row):
    c, w = x.shape
    blk = 2 * hs
    if blk >= SUBLANES:
        xr = x.reshape(c // blk, blk, w)
        return jnp.broadcast_to(xr[:, hs - 1:hs, :], (c // blk, blk, w)).reshape(c, w)
    tiles = (c // SUBLANES, SUBLANES, w)
    x3, pos = x.reshape(tiles), (row & (blk - 1)).reshape(tiles)
    if hs == 1:
        out = jnp.where(pos == 1, pltpu.roll(x3, 1, 1), x3)
    else:
        assert hs == 2
        nxt = jnp.where((pos & 1) == 1, x3, pltpu.roll(x3, SUBLANES - 1, 1))
        out = jnp.where(pos >= 2, pltpu.roll(nxt, 2, 1), nxt)
    return out.reshape(c, w)


def _level_step(p, qf, kf, hs, row):
    c, w = p.shape
    if hs < SUBLANES:
        tot = _split_row_bcast(p, hs, row)
        second = (row & hs) != 0
        return jnp.where(second, p, tot - p), jnp.where(second, p + tot, p), jnp.where(second, qf, kf)
    xs, ps, qks = [], [], []
    for lo in range(0, c, 2 * hs):
        first, second = p[lo:lo + hs], p[lo + hs:lo + 2 * hs]
        tot = jnp.broadcast_to(first[hs - 1:hs], (hs, w))
        xs += [tot - first, second]
        ps += [first, second + tot]
        qks += [kf[lo:lo + hs], qf[lo + hs:lo + 2 * hs]]
    if hs % BF16_ROWS == 0:
        qk = jnp.concatenate(qks, axis=0)
    else:
        qk = jnp.where((row & hs) != 0, qf, kf)
    return jnp.concatenate(xs, axis=0), jnp.concatenate(ps, axis=0), qk


def _gla_chunk(rows, c, q_ref, k_ref, v_ref, la_ref, mask_ref, st_in, st_out, o_ref, *, state_t):
    levels = _gla_levels(c)
    n_lvl = len(levels)
    row = lax.broadcasted_iota(jnp.int32, (c, DK_HEAD_B), 0)
    for h in range(N_HEADS_B):
        hk = slice(h * DK_HEAD_B, (h + 1) * DK_HEAD_B)
        hv = slice(h * DV_HEAD_B, (h + 1) * DV_HEAD_B)
        qf = q_ref[rows, hk]
        kf = k_ref[rows, hk]
        vh = v_ref[rows, hv]
        p = la_ref[rows, hk]
        a = mask_ref[n_lvl] * _dot_nt(qf, kf)
        for l, hs in enumerate(levels):
            x, p, qk = _level_step(p, qf, kf, hs, row)
            z = qk * jnp.exp2(x).astype(BF16)
            a = a + mask_ref[l] * _dot_nt(z, z)
        b = p
        b_last = b[c - 1:c, :]
        o = _dot(a.astype(BF16), vh)
        st = st_in[h]
        qd = qf * jnp.exp2(b).astype(BF16)
        kd = kf * jnp.exp2(b_last - b).astype(BF16)
        d_last = jnp.exp2(b_last)
        if state_t:
            o = o + _dot_nt(qd, st.astype(BF16))
            st_out[h] = st * d_last + _dot_tn(vh, kd)
        else:
            o = o + _dot(qd, st.astype(BF16))
            d_col = jnp.broadcast_to(d_last, (SUBLANES, DK_HEAD_B)).T[:, 0:1]
            st_out[h] = st * d_col + _dot_tn(kd, vh)
        o_ref[rows, hv] = o.astype(BF16)


def _gla_stream_kernel(q_ref, k_ref, v_ref, la_ref, mask_ref, s0_ref,
                       o_ref, sout_ref, st_ref, *, c, n_chunks, unroll):
    @pl.when(pl.program_id(0) == 0)
    def _():
        for h in range(N_HEADS_B):
            st_ref[h] = s0_ref[h].T

    def chunk(ci, carry):
        rows = pl.ds(pl.multiple_of(ci * c, c), c)
        _gla_chunk(rows, c, q_ref, k_ref, v_ref, la_ref, mask_ref, st_ref, st_ref, o_ref, state_t=True)
        return carry

    lax.fori_loop(0, n_chunks, chunk, 0, unroll=unroll)

    @pl.when(pl.program_id(0) == pl.num_programs(0) - 1)
    def _():
        for h in range(N_HEADS_B):
            sout_ref[h] = st_ref[h].T


def _gla_seqs_kernel(q_ref, k_ref, v_ref, la_ref, mask_ref, s0_ref, o_ref, sout_ref, *, c, n_seqs):
    for j in range(n_seqs):
        _gla_chunk(slice(j * c, (j + 1) * c), c, q_ref, k_ref, v_ref, la_ref, mask_ref,
                   s0_ref.at[j], sout_ref.at[j], o_ref, state_t=False)


def _gla_stream(qb, kb, vb, la, s0, *, c, tb):
    t = qb.shape[0]
    masks = jnp.asarray(_gla_masks(c), F32)
    row = lambda w: pl.BlockSpec((tb, w), lambda i: (i, 0))
    st_shape = (N_HEADS_B, DK_HEAD_B, DV_HEAD_B)
    s_spec = pl.BlockSpec(st_shape, lambda i: (0, 0, 0))
    n_chunks = tb // c
    return pl.pallas_call(
        functools.partial(_gla_stream_kernel, c=c, n_chunks=n_chunks, unroll=min(GLA_UNROLL, n_chunks)),
        grid=(t // tb,),
        in_specs=[row(DK_B), row(DK_B), row(DV_B), row(DK_B), _const_spec(masks.shape), s_spec],
        out_specs=(row(DV_B), s_spec),
        out_shape=(jax.ShapeDtypeStruct((t, DV_B), BF16), jax.ShapeDtypeStruct(st_shape, F32)),
        scratch_shapes=[pltpu.VMEM((N_HEADS_B, DV_HEAD_B, DK_HEAD_B), F32)],
        compiler_params=_params(),
        name="gla_stream",
    )(qb, kb, vb, la, masks, s0)


def _gla_seqs(qb, kb, vb, la, s0, *, c):
    t = qb.shape[0]
    n = s0.shape[0]
    assert t == n * c and n % GLA_SEQS_PER_STEP == 0
    masks = jnp.asarray(_gla_masks(c), F32)
    tb = GLA_SEQS_PER_STEP * c
    row = lambda w: pl.BlockSpec((tb, w), lambda i: (i, 0))
    s_spec = pl.BlockSpec((GLA_SEQS_PER_STEP,) + s0.shape[1:], lambda i: (i, 0, 0, 0))
    return pl.pallas_call(
        functools.partial(_gla_seqs_kernel, c=c, n_seqs=GLA_SEQS_PER_STEP),
        grid=(n // GLA_SEQS_PER_STEP,),
        in_specs=[row(DK_B), row(DK_B), row(DV_B), row(DK_B), _const_spec(masks.shape), s_spec],
        out_specs=(row(DV_B), s_spec),
        out_shape=(jax.ShapeDtypeStruct((t, DV_B), BF16), jax.ShapeDtypeStruct(s0.shape, F32)),
        compiler_params=_params(),
        name="gla_seqs",
    )(qb, kb, vb, la, masks, s0)


def _merge_ffn_kernel(x_ref, oa_ref, ob_ref, rs_ref, ga_ref, gb_ref, gn_ref, gpost_ref, gpre_ref, gfpost_ref,
                      wpa_ref, wpb_ref, wout_ref, wg_ref, wu_ref, wd_ref, y_ref, x1_ref, h_ref, acc_ref):
    pa = _dot(oa_ref[...], wpa_ref[...])
    ob = jnp.concatenate(
        [_rms(ob_ref[:, h * DV_HEAD_B:(h + 1) * DV_HEAD_B].astype(F32)) * gn_ref[...] for h in range(N_HEADS_B)],
        axis=1) * rs_ref[...].astype(F32)
    pb = _dot(ob.astype(BF16), wpb_ref[...])
    mix = jax.nn.sigmoid(ga_ref[...].astype(F32)) * pa + jax.nn.sigmoid(gb_ref[...].astype(F32)) * pb
    x1 = x_ref[...] + _rms(_dot(mix.astype(BF16), wout_ref[...])) * gpost_ref[...]
    x1_ref[...] = x1
    h_ref[...] = (_rms(x1) * gpre_ref[...]).astype(BF16)
    acc_ref[...] = jnp.zeros_like(acc_ref)

    for j in range(N_FF_CHUNKS):
        cols = slice(j * FF_CHUNK, (j + 1) * FF_CHUNK)
        h = h_ref[...]
        g = _dot(h, wg_ref[:, cols])
        u = _dot(h, wu_ref[:, cols])
        acc_ref[...] += _dot((g * jax.nn.sigmoid(g) * u).astype(BF16), wd_ref[cols, :])
    y_ref[...] = x1_ref[...] + _rms(acc_ref[...]) * gfpost_ref[...]


def _merge_ffn(x, oa, ob, rs, ga, gb, gnorm, gpost, gpre, gfpost, wpa, wpb, wout, wg, wu, wd):
    t = x.shape[0]
    n = t // TM
    row = lambda w: pl.BlockSpec((TM, w), lambda i: (i, 0))
    vec = _const_spec((1, D_MODEL))
    return pl.pallas_call(
        _merge_ffn_kernel,
        grid=(n,),
        in_specs=[row(D_MODEL), row(D_A), row(DV_B), row(DV_B), row(D_MODEL), row(D_MODEL),
                  _const_spec((1, DV_HEAD_B)), vec, vec, vec,
                  _const_spec(wpa.shape), _const_spec(wpb.shape), _const_spec(wout.shape),
                  _const_spec(wg.shape), _const_spec(wu.shape), _const_spec(wd.shape)],
        out_specs=row(D_MODEL),
        out_shape=jax.ShapeDtypeStruct((t, D_MODEL), F32),
        scratch_shapes=[pltpu.VMEM((TM, D_MODEL), F32), pltpu.VMEM((TM, D_MODEL), BF16),
                        pltpu.VMEM((TM, D_MODEL), F32)],
        compiler_params=_params(),
        name="merge_ffn",
    )(x, oa, ob, rs, ga, gb, gnorm, gpost, gpre, gfpost, wpa, wpb, wout, wg, wu, wd)


def _rel_bias_row(table):
    n_hi = BAND_PAST + BIAS_ORIGIN - REL_CLIP
    n_lo = BIAS_F_LEN - n_hi - (2 * REL_CLIP + 1)
    h = table.shape[0]
    return jnp.concatenate([jnp.broadcast_to(table[:, -1:], (h, n_hi)), table[:, ::-1],
                            jnp.broadcast_to(table[:, :1], (h, n_lo))], axis=1)


def kernel(x_prompt, x_sample, cache_attn_k, cache_attn_v, state_gla, norm_mix_pre, norm_mix_post, norm_ffn_pre, norm_ffn_post, w_in, w_decay_up, b_decay, rel_bias, gla_norm, w_proj_a, w_proj_b, w_out, w_ffn_gate, w_ffn_up, w_ffn_down):
    depth = w_in.shape[0]
    assert depth == 1, "single-layer step"
    batch, seq, _ = x_prompt.shape
    dec_batch, dec_seq, _ = x_sample.shape
    assert batch == 1 and seq % TM == 0 and (dec_batch * dec_seq) % TM == 0
    assert seq % (ATT_SUB * ATT_QB) == 0
    past = cache_attn_k.shape[2]

    wt_bf = jnp.transpose(w_in[0]).astype(BF16)
    wup = w_decay_up[0].astype(BF16)
    bdec = b_decay[0][None, :]
    vec = lambda a: a[0][None, :]
    wpa, wpb, wout = w_proj_a[0].astype(BF16), w_proj_b[0].astype(BF16), w_out[0].astype(BF16)
    wg, wu, wd = w_ffn_gate[0].astype(BF16), w_ffn_up[0].astype(BF16), w_ffn_down[0].astype(BF16)
    gnorm = gla_norm[0][None, :]

    def layer_tail(x, oa, ob, rs, ga, gb):
        return _merge_ffn(x, oa, ob, rs, ga, gb, gnorm, vec(norm_mix_post), vec(norm_ffn_pre), vec(norm_ffn_post),
                          wpa, wpb, wout, wg, wu, wd)

    xp = x_prompt[0]
    qa, ka, va, qb, kb, vb, rs, la, ga, gb, kf, vf = _in_proj(
        xp, vec(norm_mix_pre), wt_bf, wup, bdec, kv_rows_every_step=False)
    f_tab = _rel_bias_row(rel_bias[0])
    oa = _attn_prompt(qa, ka, va, f_tab)
    s0 = jnp.zeros((N_HEADS_B, DK_HEAD_B, DV_HEAD_B), F32)
    ob, sp = _gla_stream(qb, kb, vb, la, s0, c=GLA_CHUNK, tb=GLA_TB)
    yp = layer_tail(xp, oa, ob, rs, ga, gb)
    keep = min(BAND_PAST, seq)
    assert keep == TM

    xs = x_sample.reshape(dec_batch * dec_seq, D_MODEL)
    qa, ka, va, qb, kb, vb, rs, la, ga, gb, kfs, vfs = _in_proj(
        xs, vec(norm_mix_pre), wt_bf, wup, bdec, kv_rows_every_step=True)
    assert past == BAND_PAST
    oa = _attn_sample(qa, kfs, vfs, jnp.transpose(cache_attn_k[0], (0, 2, 3, 1)),
                      jnp.transpose(cache_attn_v[0], (0, 2, 3, 1)), f_tab, seq=dec_seq)
    gla_chunk = CHUNK if dec_seq % CHUNK == 0 else dec_seq
    assert gla_chunk == dec_seq
    ob, ss = _gla_seqs(qb, kb, vb, la, state_gla[0], c=gla_chunk)
    ys = layer_tail(xs, oa, ob, rs, ga, gb)

    hd = (N_HEADS_A, HEAD_DIM_A)
    return (yp[None], ys.reshape(dec_batch, dec_seq, D_MODEL),
            kf.reshape((1, 1, keep) + hd), vf.reshape((1, 1, keep) + hd), sp[None, None],
            kfs.reshape((1, dec_batch, dec_seq) + hd), vfs.reshape((1, dec_batch, dec_seq) + hd), ss[None])
```

```python
import functools

import numpy as np
import jax
import jax.numpy as jnp
from jax import lax
from jax.experimental import pallas as pl
from jax.experimental.pallas import tpu as pltpu

F32 = jnp.float32
BF16 = jnp.bfloat16

D_MODEL = 1024
CHUNK = 64
BAND_CHUNKS = 8
BAND_PAST = BAND_CHUNKS * CHUNK
N_HEADS_A = 8
HEAD_DIM_A = 64
D_A = N_HEADS_A * HEAD_DIM_A
REL_CLIP = 128
N_HEADS_B = 4
DK_HEAD_B = 128
DV_HEAD_B = 256
DK_B = N_HEADS_B * DK_HEAD_B
DV_B = N_HEADS_B * DV_HEAD_B
GATE_RANK = 16
GATE_TEMP = 16.0
D_FF = 2816
EPS = 1e-6

LANES = 128
SUBLANES = 8
BF16_ROWS = 16
LOG2E = 1.4426950408889634
VMEM_LIMIT_BYTES = 56 * 1024 * 1024
NEG_BIG = -1e30

_PIECES = (("qa", D_A), ("ka", D_A), ("va", D_A), ("qb", DK_B), ("kb", DK_B), ("vb", DV_B),
           ("rb", DV_B), ("dlr", GATE_RANK), ("ga", D_MODEL), ("gb", D_MODEL))
_OFF = {}
_o = 0
for _n, _w in _PIECES:
    _OFF[_n] = (_o, _o + _w)
    _o += _w
D_IN = _o
assert all(lo % BF16_ROWS == 0 for lo, _ in _OFF.values())

TM = 512
ATT_QB = 256
ATT_WIN = ATT_QB + BAND_PAST
ATT_SUB = 4
ATT_MASKED = BAND_PAST // ATT_QB
assert ATT_MASKED <= ATT_SUB
BIAS_ORIGIN = ATT_QB
BIAS_F_LEN = 1024
GLA_TB = 1024
GLA_CHUNK = 128
GLA_UNROLL = 8
GLA_SEQS_PER_STEP = 4
FF_CHUNK = 256
N_FF_CHUNKS = D_FF // FF_CHUNK


def _const_spec(shape):
    nd = len(shape)
    return pl.BlockSpec(shape, lambda i: (0,) * nd, pipeline_mode=pl.Buffered(1))


def _params():
    return pltpu.CompilerParams(dimension_semantics=("arbitrary",), vmem_limit_bytes=VMEM_LIMIT_BYTES)


def _rms(x):
    return x * lax.rsqrt(jnp.mean(x * x, axis=-1, keepdims=True) + EPS)


def _dot(a, b):
    return jnp.dot(a, b, preferred_element_type=F32)


def _dot_nt(a, b):
    return lax.dot_general(a, b, (((1,), (1,)), ((), ())), preferred_element_type=F32)


def _dot_tn(a, b):
    return lax.dot_general(a, b, (((0,), (0,)), ((), ())), preferred_element_type=F32)


def _in_proj_kernel(x_ref, g_ref, wt_ref, wup_ref, bdec_ref,
                    qa_ref, ka_ref, va_ref, qb_ref, kb_ref, vb_ref, rs_ref, la_ref, ga_ref, gb_ref,
                    kf_ref, vf_ref):
    h = (_rms(x_ref[...]) * g_ref[...]).astype(BF16)

    def proj(name):
        lo, hi = _OFF[name]
        return _dot_nt(h, wt_ref[lo:hi, :])

    dlr = proj("dlr").astype(BF16)
    qa_ref[...] = (proj("qa") * (HEAD_DIM_A ** -0.5 * LOG2E)).astype(BF16)
    ka = proj("ka")
    va = proj("va")
    ka_ref[...] = ka.astype(BF16)
    va_ref[...] = va.astype(BF16)
    kf_ref[...] = ka
    vf_ref[...] = va
    z = (_dot(dlr, wup_ref[...]) + bdec_ref[...]) * LOG2E
    la_ref[...] = (jnp.minimum(z, 0.0) - jnp.log2(1.0 + jnp.exp2(-jnp.abs(z)))) * (1.0 / GATE_TEMP)
    qb_ref[...] = (proj("qb") * (DK_HEAD_B ** -0.5)).astype(BF16)
    kb_ref[...] = proj("kb").astype(BF16)
    vb_ref[...] = proj("vb").astype(BF16)
    r = proj("rb")
    rs_ref[...] = (r * jax.nn.sigmoid(r)).astype(BF16)
    ga_ref[...] = proj("ga").astype(BF16)
    gb_ref[...] = proj("gb").astype(BF16)


def _in_proj(x, g, wt_bf, wup, bdec, *, kv_rows_every_step):
    t = x.shape[0]
    n = t // TM
    row = lambda w: pl.BlockSpec((TM, w), lambda i: (i, 0))
    if kv_rows_every_step:
        kv_rows, kv_spec = t, row(D_A)
    else:
        kv_rows, kv_spec = TM, pl.BlockSpec((TM, D_A), lambda i: (0, 0))
    widths = (D_A, D_A, D_A, DK_B, DK_B, DV_B, DV_B, DK_B, D_MODEL, D_MODEL)
    dtypes = (BF16,) * 7 + (F32, BF16, BF16)
    out_shape = tuple(jax.ShapeDtypeStruct((t, w), d) for w, d in zip(widths, dtypes))
    out_shape += (jax.ShapeDtypeStruct((kv_rows, D_A), F32),) * 2
    out_specs = tuple(row(w) for w in widths) + (kv_spec, kv_spec)
    return pl.pallas_call(
        _in_proj_kernel,
        grid=(n,),
        in_specs=[row(D_MODEL), _const_spec((1, D_MODEL)), _const_spec((D_IN, D_MODEL)),
                  _const_spec((GATE_RANK, DK_B)), _const_spec((1, DK_B))],
        out_specs=out_specs,
        out_shape=out_shape,
        compiler_params=_params(),
        name="in_proj",
    )(x, g, wt_bf, wup, bdec)


def _softmax_pv(s, v):
    m = jnp.max(s, axis=-1, keepdims=True)
    e = jnp.exp2(s - m).astype(BF16)
    pv = _dot(e, jnp.concatenate([v, jnp.ones_like(v)], axis=1))
    return pv[:, :LANES] / pv[:, LANES:LANES + 1]


def _head_pair_attention(q_pair, k_pair, v_pair, bias0, bias1):
    nq = q_pair.shape[0]
    lane = lax.broadcasted_iota(jnp.int32, (1, LANES), 1)
    first = lane < HEAD_DIM_A
    zero = jnp.zeros_like(q_pair)
    qs = jnp.concatenate([jnp.where(first, q_pair, zero), jnp.where(first, zero, q_pair)], axis=0)
    s = _dot_nt(qs, k_pair) + jnp.concatenate([bias0, bias1], axis=0)
    pv = _softmax_pv(s, v_pair)
    return jnp.where(first, pv[:nq], pv[nq:])


def _expand_rel_bias(f_ref, bias_ref, n_q, n_k, first_valid, n_valid, band):
    t = lax.broadcasted_iota(jnp.int32, (n_q, n_k), 0)
    k = lax.broadcasted_iota(jnp.int32, (n_q, n_k), 1)
    keep = (k < n_valid) & (k >= first_valid)
    if band:
        jq, jk = t // CHUNK, k // CHUNK
        keep = keep & (jk >= jq) & (jk <= jq + BAND_CHUNKS)
    for h in range(N_HEADS_A):
        rows = jnp.broadcast_to(f_ref[h:h + 1, :], (n_q, BIAS_F_LEN))
        rolled = pltpu.roll(rows, BIAS_F_LEN - BIAS_ORIGIN, 1, stride=1, stride_axis=0)
        bias_ref[h] = jnp.where(keep, rolled[:, :n_k] * LOG2E, NEG_BIG)


def _attn_prompt_kernel(q_ref, *refs):
    n_kb = ATT_SUB + ATT_WIN // ATT_QB - 1
    k_refs, v_refs = refs[:n_kb], refs[n_kb:2 * n_kb]
    f_ref, o_ref, bias_ref = refs[2 * n_kb:]
    i = pl.program_id(0)

    @pl.when(i == 0)
    def _():
        _expand_rel_bias(f_ref, bias_ref.at[0], ATT_QB, ATT_WIN, 0, ATT_WIN, band=True)
        for g in range(ATT_MASKED):
            _expand_rel_bias(f_ref, bias_ref.at[1 + g], ATT_QB, ATT_WIN, BAND_PAST - ATT_QB * g, ATT_WIN,
                             band=True)

    for sub in range(ATT_SUB):
        kwin = jnp.concatenate([r[...] for r in k_refs[sub:sub + ATT_WIN // ATT_QB]], axis=0)
        vwin = jnp.concatenate([r[...] for r in v_refs[sub:sub + ATT_WIN // ATT_QB]], axis=0)
        rows = slice(sub * ATT_QB, (sub + 1) * ATT_QB)
        which = jnp.where(i == 0, 1 + sub, 0) if sub < ATT_MASKED else 0
        for p in range(N_HEADS_A // 2):
            sl = slice(p * LANES, (p + 1) * LANES)
            o = _head_pair_attention(q_ref[rows, sl], kwin[:, sl], vwin[:, sl],
                                     bias_ref[which, 2 * p], bias_ref[which, 2 * p + 1])
            o_ref[rows, sl] = o.astype(BF16)


def _attn_prompt(qa, ka, va, f_tab):
    t = qa.shape[0]
    n = t // (ATT_SUB * ATT_QB)
    n_kb = ATT_SUB + ATT_WIN // ATT_QB - 1
    kblk = [pl.BlockSpec((ATT_QB, D_A), functools.partial(
        lambda j, i: (jnp.maximum(ATT_SUB * i + j - (n_kb - ATT_SUB), 0), 0), j)) for j in range(n_kb)]
    qblk = pl.BlockSpec((ATT_SUB * ATT_QB, D_A), lambda i: (i, 0))
    return pl.pallas_call(
        _attn_prompt_kernel,
        grid=(n,),
        in_specs=[qblk] + kblk + kblk + [_const_spec((N_HEADS_A, BIAS_F_LEN))],
        out_specs=qblk,
        out_shape=jax.ShapeDtypeStruct((t, D_A), BF16),
        scratch_shapes=[pltpu.VMEM((1 + ATT_MASKED, N_HEADS_A, ATT_QB, ATT_WIN), F32)],
        compiler_params=_params(),
        name="attn_prompt",
    )(qa, *([ka] * n_kb), *([va] * n_kb), f_tab)


def _attn_sample_kernel(q_ref, kt_ref, vt_ref, kn_ref, vn_ref, f_ref, o_ref, bias_ref):
    n_q, w = q_ref.shape[0], kt_ref.shape[-1]

    @pl.when(pl.program_id(0) == 0)
    def _():
        _expand_rel_bias(f_ref, bias_ref, n_q, bias_ref.shape[-1], 0, w + n_q, band=False)

    lane = lax.broadcasted_iota(jnp.int32, (1, LANES), 1)
    first = lane < HEAD_DIM_A
    for p in range(N_HEADS_A // 2):
        sl = slice(p * LANES, (p + 1) * LANES)
        q_pair = q_ref[:, sl]
        zero = jnp.zeros_like(q_pair)
        qs = jnp.concatenate([jnp.where(first, q_pair, zero), jnp.where(first, zero, q_pair)], axis=0)
        kt = jnp.concatenate([kt_ref[2 * p], kt_ref[2 * p + 1]], axis=0).astype(BF16)
        vt = jnp.concatenate([vt_ref[2 * p], vt_ref[2 * p + 1]], axis=0).astype(BF16)
        kn = kn_ref[:, sl].astype(BF16)
        vn = vn_ref[:, sl].astype(BF16)
        bias = jnp.concatenate([bias_ref[2 * p], bias_ref[2 * p + 1]], axis=0)
        s_c = _dot(qs, kt) + bias[:, :w]
        s_n = _dot_nt(qs, kn) + bias[:, w:w + n_q]
        m = jnp.maximum(jnp.max(s_c, axis=-1, keepdims=True), jnp.max(s_n, axis=-1, keepdims=True))
        e_c = jnp.exp2(s_c - m).astype(BF16)
        e_n = jnp.exp2(s_n - m).astype(BF16)
        pv = (_dot_nt(e_c, jnp.concatenate([vt, jnp.ones_like(vt)], axis=0))
              + _dot(e_n, jnp.concatenate([vn, jnp.ones_like(vn)], axis=1)))
        pv = pv[:, :LANES] / pv[:, LANES:LANES + 1]
        o_ref[:, sl] = jnp.where(first, pv[:n_q], pv[n_q:]).astype(BF16)


def _attn_sample(qa, kn, vn, cache_kt, cache_vt, f_tab, *, seq):
    t = qa.shape[0]
    nb, _, _, w = cache_kt.shape
    n_k = w + seq
    row = pl.BlockSpec((seq, D_A), lambda i: (i, 0))
    cache = pl.BlockSpec((None, N_HEADS_A, HEAD_DIM_A, w), lambda i: (i, 0, 0, 0))
    return pl.pallas_call(
        _attn_sample_kernel,
        grid=(nb,),
        in_specs=[row, cache, cache, row, row, _const_spec((N_HEADS_A, BIAS_F_LEN))],
        out_specs=row,
        out_shape=jax.ShapeDtypeStruct((t, D_A), BF16),
        scratch_shapes=[pltpu.VMEM((N_HEADS_A, seq, n_k + (-n_k) % LANES), F32)],
        compiler_params=_params(),
        name="attn_sample",
    )(qa, cache_kt, cache_vt, kn, vn, f_tab)


def _gla_levels(c):
    out, hs = [], 1
    while hs < c:
        out.append(hs)
        hs *= 2
    return out


def _gla_masks(c):
    t = np.arange(c)
    masks = []
    for hs in _gla_levels(c):
        blk = t // (2 * hs)
        second = (t % (2 * hs)) >= hs
        masks.append((blk[:, None] == blk[None, :]) & second[:, None] & ~second[None, :])
    masks.append(np.eye(c, dtype=bool))
    return np.stack(masks).astype(np.float32)


def _split_row_bcast(x, hs, row):
    c, w = x.shape
    blk = 2 * hs
    if blk >= SUBLANES:
        xr = x.reshape(c // blk, blk, w)
        return jnp.broadcast_to(xr[:, hs - 1:hs, :], (c // blk, blk, w)).reshape(c, w)
    tiles = (c // SUBLANES, SUBLANES, w)
    x3, pos = x.reshape(tiles), (row & (blk - 1)).reshape(tiles)
    if hs == 1:
        out = jnp.where(pos == 1, pltpu.roll(x3, 1, 1), x3)
    else:
        assert hs == 2
        nxt = jnp.where((pos & 1) == 1, x3, pltpu.roll(x3, SUBLANES - 1, 1))
        out = jnp.where(pos >= 2, pltpu.roll(nxt, 2, 1), nxt)
    return out.reshape(c, w)


def _level_step(p, qf, kf, hs, row):
    c, w = p.shape
    if hs < SUBLANES:
        tot = _split_row_bcast(p, hs, row)
        second = (row & hs) != 0
        return jnp.where(second, p, tot - p), jnp.where(second, p + tot, p), jnp.where(second, qf, kf)
    xs, ps, qks = [], [], []
    for lo in range(0, c, 2 * hs):
        first, second = p[lo:lo + hs], p[lo + hs:lo + 2 * hs]
        tot = jnp.broadcast_to(first[hs - 1:hs], (hs, w))
        xs += [tot - first, second]
        ps += [first, second + tot]
        qks += [kf[lo:lo + hs], qf[lo + hs:lo + 2 * hs]]
    if hs % BF16_ROWS == 0:
        qk = jnp.concatenate(qks, axis=0)
    else:
        qk = jnp.where((row & hs) != 0, qf, kf)
    return jnp.concatenate(xs, axis=0), jnp.concatenate(ps, axis=0), qk


def _gla_chunk(rows, c, q_ref, k_ref, v_ref, la_ref, mask_ref, st_in, st_out, o_ref, *, state_t):
    levels = _gla_levels(c)
    n_lvl = len(levels)
    row = lax.broadcasted_iota(jnp.int32, (c, DK_HEAD_B), 0)
    for h in range(N_HEADS_B):
        hk = slice(h * DK_HEAD_B, (h + 1) * DK_HEAD_B)
        hv = slice(h * DV_HEAD_B, (h + 1) * DV_HEAD_B)
        qf = q_ref[rows, hk]
        kf = k_ref[rows, hk]
        vh = v_ref[rows, hv]
        p = la_ref[rows, hk]
        a = mask_ref[n_lvl] * _dot_nt(qf, kf)
        for l, hs in enumerate(levels):
            x, p, qk = _level_step(p, qf, kf, hs, row)
            z = qk * jnp.exp2(x).astype(BF16)
            a = a + mask_ref[l] * _dot_nt(z, z)
        b = p
        b_last = b[c - 1:c, :]
        o = _dot(a.astype(BF16), vh)
        st = st_in[h]
        qd = qf * jnp.exp2(b).astype(BF16)
        kd = kf * jnp.exp2(b_last - b).astype(BF16)
        d_last = jnp.exp2(b_last)
        if state_t:
            o = o + _dot_nt(qd, st.astype(BF16))
            st_out[h] = st * d_last + _dot_tn(vh, kd)
        else:
            o = o + _dot(qd, st.astype(BF16))
            d_col = jnp.broadcast_to(d_last, (SUBLANES, DK_HEAD_B)).T[:, 0:1]
            st_out[h] = st * d_col + _dot_tn(kd, vh)
        o_ref[rows, hv] = o.astype(BF16)


def _gla_stream_kernel(q_ref, k_ref, v_ref, la_ref, mask_ref, s0_ref,
                       o_ref, sout_ref, st_ref, *, c, n_chunks, unroll):
    @pl.when(pl.program_id(0) == 0)
    def _():
        for h in range(N_HEADS_B):
            st_ref[h] = s0_ref[h].T

    def chunk(ci, carry):
        rows = pl.ds(pl.multiple_of(ci * c, c), c)
        _gla_chunk(rows, c, q_ref, k_ref, v_ref, la_ref, mask_ref, st_ref, st_ref, o_ref, state_t=True)
        return carry

    lax.fori_loop(0, n_chunks, chunk, 0, unroll=unroll)

    @pl.when(pl.program_id(0) == pl.num_programs(0) - 1)
    def _():
        for h in range(N_HEADS_B):
            sout_ref[h] = st_ref[h].T


def _gla_seqs_kernel(q_ref, k_ref, v_ref, la_ref, mask_ref, s0_ref, o_ref, sout_ref, *, c, n_seqs):
    for j in range(n_seqs):
        _gla_chunk(slice(j * c, (j + 1) * c), c, q_ref, k_ref, v_ref, la_ref, mask_ref,
                   s0_ref.at[j], sout_ref.at[j], o_ref, state_t=False)


def _gla_stream(qb, kb, vb, la, s0, *, c, tb):
    t = qb.shape[0]
    masks = jnp.asarray(_gla_masks(c), F32)
    row = lambda w: pl.BlockSpec((tb, w), lambda i: (i, 0))
    st_shape = (N_HEADS_B, DK_HEAD_B, DV_HEAD_B)
    s_spec = pl.BlockSpec(st_shape, lambda i: (0, 0, 0))
    n_chunks = tb // c
    return pl.pallas_call(
        functools.partial(_gla_stream_kernel, c=c, n_chunks=n_chunks, unroll=min(GLA_UNROLL, n_chunks)),
        grid=(t // tb,),
        in_specs=[row(DK_B), row(DK_B), row(DV_B), row(DK_B), _const_spec(masks.shape), s_spec],
        out_specs=(row(DV_B), s_spec),
        out_shape=(jax.ShapeDtypeStruct((t, DV_B), BF16), jax.ShapeDtypeStruct(st_shape, F32)),
        scratch_shapes=[pltpu.VMEM((N_HEADS_B, DV_HEAD_B, DK_HEAD_B), F32)],
        compiler_params=_params(),
        name="gla_stream",
    )(qb, kb, vb, la, masks, s0)


def _gla_seqs(qb, kb, vb, la, s0, *, c):
    t = qb.shape[0]
    n = s0.shape[0]
    assert t == n * c and n % GLA_SEQS_PER_STEP == 0
    masks = jnp.asarray(_gla_masks(c), F32)
    tb = GLA_SEQS_PER_STEP * c
    row = lambda w: pl.BlockSpec((tb, w), lambda i: (i, 0))
    s_spec = pl.BlockSpec((GLA_SEQS_PER_STEP,) + s0.shape[1:], lambda i: (i, 0, 0, 0))
    return pl.pallas_call(
        functools.partial(_gla_seqs_kernel, c=c, n_seqs=GLA_SEQS_PER_STEP),
        grid=(n // GLA_SEQS_PER_STEP,),
        in_specs=[row(DK_B), row(DK_B), row(DV_B), row(DK_B), _const_spec(masks.shape), s_spec],
        out_specs=(row(DV_B), s_spec),
        out_shape=(jax.ShapeDtypeStruct((t, DV_B), BF16), jax.ShapeDtypeStruct(s0.shape, F32)),
        compiler_params=_params(),
        name="gla_seqs",
    )(qb, kb, vb, la, masks, s0)


def _merge_ffn_kernel(x_ref, oa_ref, ob_ref, rs_ref, ga_ref, gb_ref, gn_ref, gpost_ref, gpre_ref, gfpost_ref,
                      wpa_ref, wpb_ref, wout_ref, wg_ref, wu_ref, wd_ref, y_ref, x1_ref, h_ref, act_ref):
    pa = _dot(oa_ref[...], wpa_ref[...])
    ob = jnp.concatenate(
        [_rms(ob_ref[:, h * DV_HEAD_B:(h + 1) * DV_HEAD_B].astype(F32)) * gn_ref[...] for h in range(N_HEADS_B)],
        axis=1) * rs_ref[...].astype(F32)
    pb = _dot(ob.astype(BF16), wpb_ref[...])
    mix = jax.nn.sigmoid(ga_ref[...].astype(F32)) * pa + jax.nn.sigmoid(gb_ref[...].astype(F32)) * pb
    x1 = x_ref[...] + _rms(_dot(mix.astype(BF16), wout_ref[...])) * gpost_ref[...]
    x1_ref[...] = x1
    h_ref[...] = (_rms(x1) * gpre_ref[...]).astype(BF16)

    for j in range(N_FF_CHUNKS):
        cols = slice(j * FF_CHUNK, (j + 1) * FF_CHUNK)
        h = h_ref[...]
        g = _dot(h, wg_ref[:, cols])
        u = _dot(h, wu_ref[:, cols])
        act_ref[:, cols] = (g * jax.nn.sigmoid(g) * u).astype(BF16)
    y_ref[...] = x1_ref[...] + _rms(_dot(act_ref[...], wd_ref[...])) * gfpost_ref[...]


def _merge_ffn(x, oa, ob, rs, ga, gb, gnorm, gpost, gpre, gfpost, wpa, wpb, wout, wg, wu, wd):
    t = x.shape[0]
    n = t // TM
    row = lambda w: pl.BlockSpec((TM, w), lambda i: (i, 0))
    vec = _const_spec((1, D_MODEL))
    return pl.pallas_call(
        _merge_ffn_kernel,
        grid=(n,),
        in_specs=[row(D_MODEL), row(D_A), row(DV_B), row(DV_B), row(D_MODEL), row(D_MODEL),
                  _const_spec((1, DV_HEAD_B)), vec, vec, vec,
                  _const_spec(wpa.shape), _const_spec(wpb.shape), _const_spec(wout.shape),
                  _const_spec(wg.shape), _const_spec(wu.shape), _const_spec(wd.shape)],
        out_specs=row(D_MODEL),
        out_shape=jax.ShapeDtypeStruct((t, D_MODEL), F32),
        scratch_shapes=[pltpu.VMEM((TM, D_MODEL), F32), pltpu.VMEM((TM, D_MODEL), BF16),
                        pltpu.VMEM((TM, D_FF), BF16)],
        compiler_params=_params(),
        name="merge_ffn",
    )(x, oa, ob, rs, ga, gb, gnorm, gpost, gpre, gfpost, wpa, wpb, wout, wg, wu, wd)


def _rel_bias_row(table):
    n_hi = BAND_PAST + BIAS_ORIGIN - REL_CLIP
    n_lo = BIAS_F_LEN - n_hi - (2 * REL_CLIP + 1)
    h = table.shape[0]
    return jnp.concatenate([jnp.broadcast_to(table[:, -1:], (h, n_hi)), table[:, ::-1],
                            jnp.broadcast_to(table[:, :1], (h, n_lo))], axis=1)


def kernel(x_prompt, x_sample, cache_attn_k, cache_attn_v, state_gla, norm_mix_pre, norm_mix_post, norm_ffn_pre, norm_ffn_post, w_in, w_decay_up, b_decay, rel_bias, gla_norm, w_proj_a, w_proj_b, w_out, w_ffn_gate, w_ffn_up, w_ffn_down):
    depth = w_in.shape[0]
    assert depth == 1, "single-layer step"
    batch, seq, _ = x_prompt.shape
    dec_batch, dec_seq, _ = x_sample.shape
    assert batch == 1 and seq % TM == 0 and (dec_batch * dec_seq) % TM == 0
    assert seq % (ATT_SUB * ATT_QB) == 0
    past = cache_attn_k.shape[2]

    wt_bf = jnp.transpose(w_in[0]).astype(BF16)
    wup = w_decay_up[0].astype(BF16)
    bdec = b_decay[0][None, :]
    vec = lambda a: a[0][None, :]
    wpa, wpb, wout = w_proj_a[0].astype(BF16), w_proj_b[0].astype(BF16), w_out[0].astype(BF16)
    wg, wu, wd = w_ffn_gate[0].astype(BF16), w_ffn_up[0].astype(BF16), w_ffn_down[0].astype(BF16)
    gnorm = gla_norm[0][None, :]

    def layer_tail(x, oa, ob, rs, ga, gb):
        return _merge_ffn(x, oa, ob, rs, ga, gb, gnorm, vec(norm_mix_post), vec(norm_ffn_pre), vec(norm_ffn_post),
                          wpa, wpb, wout, wg, wu, wd)

    xp = x_prompt[0]
    qa, ka, va, qb, kb, vb, rs, la, ga, gb, kf, vf = _in_proj(
        xp, vec(norm_mix_pre), wt_bf, wup, bdec, kv_rows_every_step=False)
    f_tab = _rel_bias_row(rel_bias[0])
    oa = _attn_prompt(qa, ka, va, f_tab)
    s0 = jnp.zeros((N_HEADS_B, DK_HEAD_B, DV_HEAD_B), F32)
    ob, sp = _gla_stream(qb, kb, vb, la, s0, c=GLA_CHUNK, tb=GLA_TB)
    yp = layer_tail(xp, oa, ob, rs, ga, gb)
    keep = min(BAND_PAST, seq)
    assert keep == TM

    xs = x_sample.reshape(dec_batch * dec_seq, D_MODEL)
    qa, ka, va, qb, kb, vb, rs, la, ga, gb, kfs, vfs = _in_proj(
        xs, vec(norm_mix_pre), wt_bf, wup, bdec, kv_rows_every_step=True)
    assert past == BAND_PAST
    oa = _attn_sample(qa, kfs, vfs, jnp.transpose(cache_attn_k[0], (0, 2, 3, 1)),
                      jnp.transpose(cache_attn_v[0], (0, 2, 3, 1)), f_tab, seq=dec_seq)
    gla_chunk = CHUNK if dec_seq % CHUNK == 0 else dec_seq
    assert gla_chunk == dec_seq
    ob, ss = _gla_seqs(qb, kb, vb, la, state_gla[0], c=gla_chunk)
    ys = layer_tail(xs, oa, ob, rs, ga, gb)

    hd = (N_HEADS_A, HEAD_DIM_A)
    return (yp[None], ys.reshape(dec_batch, dec_seq, D_MODEL),
            kf.reshape((1, 1, keep) + hd), vf.reshape((1, 1, keep) + hd), sp[None, None],
            kfs.reshape((1, dec_batch, dec_seq) + hd), vfs.reshape((1, dec_batch, dec_seq) + hd), ss[None])
```

```python
import functools

import numpy as np
import jax
import jax.numpy as jnp
from jax import lax
from jax.experimental import pallas as pl
from jax.experimental.pallas import tpu as pltpu

F32 = jnp.float32
BF16 = jnp.bfloat16

D_MODEL = 1024
CHUNK = 64
BAND_CHUNKS = 8
BAND_PAST = BAND_CHUNKS * CHUNK
N_HEADS_A = 8
HEAD_DIM_A = 64
D_A = N_HEADS_A * HEAD_DIM_A
REL_CLIP = 128
N_HEADS_B = 4
DK_HEAD_B = 128
DV_HEAD_B = 256
DK_B = N_HEADS_B * DK_HEAD_B
DV_B = N_HEADS_B * DV_HEAD_B
GATE_RANK = 16
GATE_TEMP = 16.0
D_FF = 2816
EPS = 1e-6

LANES = 128
SUBLANES = 8
BF16_ROWS = 16
LOG2E = 1.4426950408889634
VMEM_LIMIT_BYTES = 56 * 1024 * 1024
NEG_BIG = -1e30

_PIECES = (("qa", D_A), ("ka", D_A), ("va", D_A), ("qb", DK_B), ("kb", DK_B), ("vb", DV_B),
           ("rb", DV_B), ("dlr", GATE_RANK), ("ga", D_MODEL), ("gb", D_MODEL))
_OFF = {}
_o = 0
for _n, _w in _PIECES:
    _OFF[_n] = (_o, _o + _w)
    _o += _w
D_IN = _o
assert all(lo % BF16_ROWS == 0 for lo, _ in _OFF.values())

TM = 512
ATT_QB = 256
ATT_WIN = ATT_QB + BAND_PAST
ATT_SUB = 4
ATT_MASKED = BAND_PAST // ATT_QB
assert ATT_MASKED <= ATT_SUB
BIAS_ORIGIN = ATT_QB
BIAS_F_LEN = 1024
GLA_TB = 1024
GLA_CHUNK = 128
GLA_UNROLL = 8
GLA_SEQS_PER_STEP = 8
ATTS_SEQS_PER_STEP = 4
FF_CHUNK = 256
N_FF_CHUNKS = D_FF // FF_CHUNK


def _const_spec(shape):
    nd = len(shape)
    return pl.BlockSpec(shape, lambda i: (0,) * nd, pipeline_mode=pl.Buffered(1))


def _params():
    return pltpu.CompilerParams(dimension_semantics=("arbitrary",), vmem_limit_bytes=VMEM_LIMIT_BYTES)


def _rms(x):
    return x * lax.rsqrt(jnp.mean(x * x, axis=-1, keepdims=True) + EPS)


def _dot(a, b):
    return jnp.dot(a, b, preferred_element_type=F32)


def _dot_nt(a, b):
    return lax.dot_general(a, b, (((1,), (1,)), ((), ())), preferred_element_type=F32)


def _dot_tn(a, b):
    return lax.dot_general(a, b, (((0,), (0,)), ((), ())), preferred_element_type=F32)


def _in_proj_kernel(x_ref, g_ref, wt_ref, wup_ref, bdec_ref,
                    qa_ref, ka_ref, va_ref, qb_ref, kb_ref, vb_ref, rs_ref, la_ref, ga_ref, gb_ref,
                    kf_ref, vf_ref):
    h = (_rms(x_ref[...]) * g_ref[...]).astype(BF16)

    def proj(name):
        lo, hi = _OFF[name]
        return _dot_nt(h, wt_ref[lo:hi, :])

    dlr = proj("dlr").astype(BF16)
    qa_ref[...] = (proj("qa") * (HEAD_DIM_A ** -0.5 * LOG2E)).astype(BF16)
    ka = proj("ka")
    va = proj("va")
    ka_ref[...] = ka.astype(BF16)
    va_ref[...] = va.astype(BF16)
    kf_ref[...] = ka
    vf_ref[...] = va
    z = (_dot(dlr, wup_ref[...]) + bdec_ref[...]) * LOG2E
    la_ref[...] = (jnp.minimum(z, 0.0) - jnp.log2(1.0 + jnp.exp2(-jnp.abs(z)))) * (1.0 / GATE_TEMP)
    qb_ref[...] = (proj("qb") * (DK_HEAD_B ** -0.5)).astype(BF16)
    kb_ref[...] = proj("kb").astype(BF16)
    vb_ref[...] = proj("vb").astype(BF16)
    r = proj("rb")
    rs_ref[...] = (r * jax.nn.sigmoid(r)).astype(BF16)
    ga_ref[...] = proj("ga").astype(BF16)
    gb_ref[...] = proj("gb").astype(BF16)


def _in_proj(x, g, wt_bf, wup, bdec, *, kv_rows_every_step):
    t = x.shape[0]
    n = t // TM
    row = lambda w: pl.BlockSpec((TM, w), lambda i: (i, 0))
    if kv_rows_every_step:
        kv_rows, kv_spec = t, row(D_A)
    else:
        kv_rows, kv_spec = TM, pl.BlockSpec((TM, D_A), lambda i: (0, 0))
    widths = (D_A, D_A, D_A, DK_B, DK_B, DV_B, DV_B, DK_B, D_MODEL, D_MODEL)
    dtypes = (BF16,) * 7 + (F32, BF16, BF16)
    out_shape = tuple(jax.ShapeDtypeStruct((t, w), d) for w, d in zip(widths, dtypes))
    out_shape += (jax.ShapeDtypeStruct((kv_rows, D_A), F32),) * 2
    out_specs = tuple(row(w) for w in widths) + (kv_spec, kv_spec)
    return pl.pallas_call(
        _in_proj_kernel,
        grid=(n,),
        in_specs=[row(D_MODEL), _const_spec((1, D_MODEL)), _const_spec((D_IN, D_MODEL)),
                  _const_spec((GATE_RANK, DK_B)), _const_spec((1, DK_B))],
        out_specs=out_specs,
        out_shape=out_shape,
        compiler_params=_params(),
        name="in_proj",
    )(x, g, wt_bf, wup, bdec)


def _softmax_pv(s, v):
    m = jnp.max(s, axis=-1, keepdims=True)
    e = jnp.exp2(s - m).astype(BF16)
    pv = _dot(e, jnp.concatenate([v, jnp.ones_like(v)], axis=1))
    return pv[:, :LANES] / pv[:, LANES:LANES + 1]


def _head_pair_attention(q_pair, k_pair, v_pair, bias0, bias1):
    nq = q_pair.shape[0]
    lane = lax.broadcasted_iota(jnp.int32, (1, LANES), 1)
    first = lane < HEAD_DIM_A
    zero = jnp.zeros_like(q_pair)
    qs = jnp.concatenate([jnp.where(first, q_pair, zero), jnp.where(first, zero, q_pair)], axis=0)
    s = _dot_nt(qs, k_pair) + jnp.concatenate([bias0, bias1], axis=0)
    pv = _softmax_pv(s, v_pair)
    return jnp.where(first, pv[:nq], pv[nq:])


def _expand_rel_bias(f_ref, bias_ref, n_q, n_k, first_valid, n_valid, band):
    t = lax.broadcasted_iota(jnp.int32, (n_q, n_k), 0)
    k = lax.broadcasted_iota(jnp.int32, (n_q, n_k), 1)
    keep = (k < n_valid) & (k >= first_valid)
    if band:
        jq, jk = t // CHUNK, k // CHUNK
        keep = keep & (jk >= jq) & (jk <= jq + BAND_CHUNKS)
    for h in range(N_HEADS_A):
        rows = jnp.broadcast_to(f_ref[h:h + 1, :], (n_q, BIAS_F_LEN))
        rolled = pltpu.roll(rows, BIAS_F_LEN - BIAS_ORIGIN, 1, stride=1, stride_axis=0)
        bias_ref[h] = jnp.where(keep, rolled[:, :n_k] * LOG2E, NEG_BIG)


def _attn_prompt_kernel(q_ref, *refs):
    n_kb = ATT_SUB + ATT_WIN // ATT_QB - 1
    k_refs, v_refs = refs[:n_kb], refs[n_kb:2 * n_kb]
    f_ref, o_ref, bias_ref = refs[2 * n_kb:]
    i = pl.program_id(0)

    @pl.when(i == 0)
    def _():
        _expand_rel_bias(f_ref, bias_ref.at[0], ATT_QB, ATT_WIN, 0, ATT_WIN, band=True)
        for g in range(ATT_MASKED):
            _expand_rel_bias(f_ref, bias_ref.at[1 + g], ATT_QB, ATT_WIN, BAND_PAST - ATT_QB * g, ATT_WIN,
                             band=True)

    for sub in range(ATT_SUB):
        kwin = jnp.concatenate([r[...] for r in k_refs[sub:sub + ATT_WIN // ATT_QB]], axis=0)
        vwin = jnp.concatenate([r[...] for r in v_refs[sub:sub + ATT_WIN // ATT_QB]], axis=0)
        rows = slice(sub * ATT_QB, (sub + 1) * ATT_QB)
        which = jnp.where(i == 0, 1 + sub, 0) if sub < ATT_MASKED else 0
        for p in range(N_HEADS_A // 2):
            sl = slice(p * LANES, (p + 1) * LANES)
            o = _head_pair_attention(q_ref[rows, sl], kwin[:, sl], vwin[:, sl],
                                     bias_ref[which, 2 * p], bias_ref[which, 2 * p + 1])
            o_ref[rows, sl] = o.astype(BF16)


def _attn_prompt(qa, ka, va, f_tab):
    t = qa.shape[0]
    n = t // (ATT_SUB * ATT_QB)
    n_kb = ATT_SUB + ATT_WIN // ATT_QB - 1
    kblk = [pl.BlockSpec((ATT_QB, D_A), functools.partial(
        lambda j, i: (jnp.maximum(ATT_SUB * i + j - (n_kb - ATT_SUB), 0), 0), j)) for j in range(n_kb)]
    qblk = pl.BlockSpec((ATT_SUB * ATT_QB, D_A), lambda i: (i, 0))
    return pl.pallas_call(
        _attn_prompt_kernel,
        grid=(n,),
        in_specs=[qblk] + kblk + kblk + [_const_spec((N_HEADS_A, BIAS_F_LEN))],
        out_specs=qblk,
        out_shape=jax.ShapeDtypeStruct((t, D_A), BF16),
        scratch_shapes=[pltpu.VMEM((1 + ATT_MASKED, N_HEADS_A, ATT_QB, ATT_WIN), F32)],
        compiler_params=_params(),
        name="attn_prompt",
    )(qa, *([ka] * n_kb), *([va] * n_kb), f_tab)


def _attn_sample_kernel(q_ref, kt_ref, vt_ref, kn_ref, vn_ref, f_ref, o_ref, bias_ref, *, n_seqs):
    n_q, w = q_ref.shape[0] // n_seqs, kt_ref.shape[-1]

    @pl.when(pl.program_id(0) == 0)
    def _():
        _expand_rel_bias(f_ref, bias_ref, n_q, bias_ref.shape[-1], 0, w + n_q, band=False)

    lane = lax.broadcasted_iota(jnp.int32, (1, LANES), 1)
    first = lane < HEAD_DIM_A
    for j, p in [(j, p) for j in range(n_seqs) for p in range(N_HEADS_A // 2)]:
        rows = slice(j * n_q, (j + 1) * n_q)
        sl = slice(p * LANES, (p + 1) * LANES)
        q_pair = q_ref[rows, sl]
        zero = jnp.zeros_like(q_pair)
        qs = jnp.concatenate([jnp.where(first, q_pair, zero), jnp.where(first, zero, q_pair)], axis=0)
        kt = jnp.concatenate([kt_ref[j, 2 * p], kt_ref[j, 2 * p + 1]], axis=0).astype(BF16)
        vt = jnp.concatenate([vt_ref[j, 2 * p], vt_ref[j, 2 * p + 1]], axis=0).astype(BF16)
        kn = kn_ref[rows, sl].astype(BF16)
        vn = vn_ref[rows, sl].astype(BF16)
        bias = jnp.concatenate([bias_ref[2 * p], bias_ref[2 * p + 1]], axis=0)
        s_c = _dot(qs, kt) + bias[:, :w]
        s_n = _dot_nt(qs, kn) + bias[:, w:w + n_q]
        m = jnp.maximum(jnp.max(s_c, axis=-1, keepdims=True), jnp.max(s_n, axis=-1, keepdims=True))
        e_c = jnp.exp2(s_c - m).astype(BF16)
        e_n = jnp.exp2(s_n - m).astype(BF16)
        pv = (_dot_nt(e_c, jnp.concatenate([vt, jnp.ones_like(vt)], axis=0))
              + _dot(e_n, jnp.concatenate([vn, jnp.ones_like(vn)], axis=1)))
        pv = pv[:, :LANES] / pv[:, LANES:LANES + 1]
        o_ref[rows, sl] = jnp.where(first, pv[:n_q], pv[n_q:]).astype(BF16)


def _attn_sample(qa, kn, vn, cache_kt, cache_vt, f_tab, *, seq):
    t = qa.shape[0]
    nb, _, _, w = cache_kt.shape
    n_k = w + seq
    n_seqs = ATTS_SEQS_PER_STEP
    assert nb % n_seqs == 0
    row = pl.BlockSpec((n_seqs * seq, D_A), lambda i: (i, 0))
    cache = pl.BlockSpec((n_seqs, N_HEADS_A, HEAD_DIM_A, w), lambda i: (i, 0, 0, 0))
    return pl.pallas_call(
        functools.partial(_attn_sample_kernel, n_seqs=n_seqs),
        grid=(nb // n_seqs,),
        in_specs=[row, cache, cache, row, row, _const_spec((N_HEADS_A, BIAS_F_LEN))],
        out_specs=row,
        out_shape=jax.ShapeDtypeStruct((t, D_A), BF16),
        scratch_shapes=[pltpu.VMEM((N_HEADS_A, seq, n_k + (-n_k) % LANES), F32)],
        compiler_params=_params(),
        name="attn_sample",
    )(qa, cache_kt, cache_vt, kn, vn, f_tab)


def _gla_levels(c):
    out, hs = [], 1
    while hs < c:
        out.append(hs)
        hs *= 2
    return out


def _gla_masks(c):
    t = np.arange(c)
    masks = []
    for hs in _gla_levels(c):
        blk = t // (2 * hs)
        second = (t % (2 * hs)) >= hs
        masks.append((blk[:, None] == blk[None, :]) & second[:, None] & ~second[None, :])
    masks.append(np.eye(c, dtype=bool))
    return np.stack(masks).astype(np.float32)


def _split_row_bcast(x, hs, row):
    c, w = x.shape
    blk = 2 * hs
    if blk >= SUBLANES:
        xr = x.reshape(c // blk, blk, w)
        return jnp.broadcast_to(xr[:, hs - 1:hs, :], (c // blk, blk, w)).reshape(c, w)
    tiles = (c // SUBLANES, SUBLANES, w)
    x3, pos = x.reshape(tiles), (row & (blk - 1)).reshape(tiles)
    if hs == 1:
        out = jnp.where(pos == 1, pltpu.roll(x3, 1, 1), x3)
    else:
        assert hs == 2
        nxt = jnp.where((pos & 1) == 1, x3, pltpu.roll(x3, SUBLANES - 1, 1))
        out = jnp.where(pos >= 2, pltpu.roll(nxt, 2, 1), nxt)
    return out.reshape(c, w)


def _level_step(p, qf, kf, hs, row):
    c, w = p.shape
    if hs < SUBLANES:
        tot = _split_row_bcast(p, hs, row)
        second = (row & hs) != 0
        return jnp.where(second, p, tot - p), jnp.where(second, p + tot, p), jnp.where(second, qf, kf)
    xs, ps, qks = [], [], []
    for lo in range(0, c, 2 * hs):
        first, second = p[lo:lo + hs], p[lo + hs:lo + 2 * hs]
        tot = jnp.broadcast_to(first[hs - 1:hs], (hs, w))
        xs += [tot - first, second]
        ps += [first, second + tot]
        qks += [kf[lo:lo + hs], qf[lo + hs:lo + 2 * hs]]
    if hs % BF16_ROWS == 0:
        qk = jnp.concatenate(qks, axis=0)
    else:
        qk = jnp.where((row & hs) != 0, qf, kf)
    return jnp.concatenate(xs, axis=0), jnp.concatenate(ps, axis=0), qk


def _gla_chunk(rows, c, q_ref, k_ref, v_ref, la_ref, mask_ref, st_in, st_out, o_ref, *, state_t):
    levels = _gla_levels(c)
    n_lvl = len(levels)
    row = lax.broadcasted_iota(jnp.int32, (c, DK_HEAD_B), 0)
    for h in range(N_HEADS_B):
        hk = slice(h * DK_HEAD_B, (h + 1) * DK_HEAD_B)
        hv = slice(h * DV_HEAD_B, (h + 1) * DV_HEAD_B)
        qf = q_ref[rows, hk]
        kf = k_ref[rows, hk]
        vh = v_ref[rows, hv]
        p = la_ref[rows, hk]
        a = mask_ref[n_lvl] * _dot_nt(qf, kf)
        for l, hs in enumerate(levels):
            x, p, qk = _level_step(p, qf, kf, hs, row)
            z = qk * jnp.exp2(x).astype(BF16)
            a = a + mask_ref[l] * _dot_nt(z, z)
        b = p
        b_last = b[c - 1:c, :]
        o = _dot(a.astype(BF16), vh)
        st = st_in[h]
        qd = qf * jnp.exp2(b).astype(BF16)
        kd = kf * jnp.exp2(b_last - b).astype(BF16)
        d_last = jnp.exp2(b_last)
        if state_t:
            o = o + _dot_nt(qd, st.astype(BF16))
            st_out[h] = st * d_last + _dot_tn(vh, kd)
        else:
            o = o + _dot(qd, st.astype(BF16))
            d_col = jnp.broadcast_to(d_last, (SUBLANES, DK_HEAD_B)).T[:, 0:1]
            st_out[h] = st * d_col + _dot_tn(kd, vh)
        o_ref[rows, hv] = o.astype(BF16)


def _gla_stream_kernel(q_ref, k_ref, v_ref, la_ref, mask_ref, s0_ref,
                       o_ref, sout_ref, st_ref, *, c, n_chunks, unroll):
    @pl.when(pl.program_id(0) == 0)
    def _():
        for h in range(N_HEADS_B):
            st_ref[h] = s0_ref[h].T

    def chunk(ci, carry):
        rows = pl.ds(pl.multiple_of(ci * c, c), c)
        _gla_chunk(rows, c, q_ref, k_ref, v_ref, la_ref, mask_ref, st_ref, st_ref, o_ref, state_t=True)
        return carry

    lax.fori_loop(0, n_chunks, chunk, 0, unroll=unroll)

    @pl.when(pl.program_id(0) == pl.num_programs(0) - 1)
    def _():
        for h in range(N_HEADS_B):
            sout_ref[h] = st_ref[h].T


def _gla_seqs_kernel(q_ref, k_ref, v_ref, la_ref, mask_ref, s0_ref, o_ref, sout_ref, *, c, n_seqs):
    for j in range(n_seqs):
        _gla_chunk(slice(j * c, (j + 1) * c), c, q_ref, k_ref, v_ref, la_ref, mask_ref,
                   s0_ref.at[j], sout_ref.at[j], o_ref, state_t=False)


def _gla_stream(qb, kb, vb, la, s0, *, c, tb):
    t = qb.shape[0]
    masks = jnp.asarray(_gla_masks(c), F32)
    row = lambda w: pl.BlockSpec((tb, w), lambda i: (i, 0))
    st_shape = (N_HEADS_B, DK_HEAD_B, DV_HEAD_B)
    s_spec = pl.BlockSpec(st_shape, lambda i: (0, 0, 0))
    n_chunks = tb // c
    return pl.pallas_call(
        functools.partial(_gla_stream_kernel, c=c, n_chunks=n_chunks, unroll=min(GLA_UNROLL, n_chunks)),
        grid=(t // tb,),
        in_specs=[row(DK_B), row(DK_B), row(DV_B), row(DK_B), _const_spec(masks.shape), s_spec],
        out_specs=(row(DV_B), s_spec),
        out_shape=(jax.ShapeDtypeStruct((t, DV_B), BF16), jax.ShapeDtypeStruct(st_shape, F32)),
        scratch_shapes=[pltpu.VMEM((N_HEADS_B, DV_HEAD_B, DK_HEAD_B), F32)],
        compiler_params=_params(),
        name="gla_stream",
    )(qb, kb, vb, la, masks, s0)


def _gla_seqs(qb, kb, vb, la, s0, *, c):
    t = qb.shape[0]
    n = s0.shape[0]
    assert t == n * c and n % GLA_SEQS_PER_STEP == 0
    masks = jnp.asarray(_gla_masks(c), F32)
    tb = GLA_SEQS_PER_STEP * c
    row = lambda w: pl.BlockSpec((tb, w), lambda i: (i, 0))
    s_spec = pl.BlockSpec((GLA_SEQS_PER_STEP,) + s0.shape[1:], lambda i: (i, 0, 0, 0))
    return pl.pallas_call(
        functools.partial(_gla_seqs_kernel, c=c, n_seqs=GLA_SEQS_PER_STEP),
        grid=(n // GLA_SEQS_PER_STEP,),
        in_specs=[row(DK_B), row(DK_B), row(DV_B), row(DK_B), _const_spec(masks.shape), s_spec],
        out_specs=(row(DV_B), s_spec),
        out_shape=(jax.ShapeDtypeStruct((t, DV_B), BF16), jax.ShapeDtypeStruct(s0.shape, F32)),
        compiler_params=_params(),
        name="gla_seqs",
    )(qb, kb, vb, la, masks, s0)


def _merge_ffn_kernel(x_ref, oa_ref, ob_ref, rs_ref, ga_ref, gb_ref, gn_ref, gpost_ref, gpre_ref, gfpost_ref,
                      wpa_ref, wpb_ref, wout_ref, wg_ref, wu_ref, wd_ref, y_ref, x1_ref, h_ref, act_ref):
    pa = _dot(oa_ref[...], wpa_ref[...])
    ob = jnp.concatenate(
        [_rms(ob_ref[:, h * DV_HEAD_B:(h + 1) * DV_HEAD_B].astype(F32)) * gn_ref[...] for h in range(N_HEADS_B)],
        axis=1) * rs_ref[...].astype(F32)
    pb = _dot(ob.astype(BF16), wpb_ref[...])
    mix = jax.nn.sigmoid(ga_ref[...].astype(F32)) * pa + jax.nn.sigmoid(gb_ref[...].astype(F32)) * pb
    x1 = x_ref[...] + _rms(_dot(mix.astype(BF16), wout_ref[...])) * gpost_ref[...]
    x1_ref[...] = x1
    h_ref[...] = (_rms(x1) * gpre_ref[...]).astype(BF16)

    for j in range(N_FF_CHUNKS):
        cols = slice(j * FF_CHUNK, (j + 1) * FF_CHUNK)
        h = h_ref[...]
        g = _dot(h, wg_ref[:, cols])
        u = _dot(h, wu_ref[:, cols])
        act_ref[:, cols] = (g * jax.nn.sigmoid(g) * u).astype(BF16)
    y_ref[...] = x1_ref[...] + _rms(_dot(act_ref[...], wd_ref[...])) * gfpost_ref[...]


def _merge_ffn(x, oa, ob, rs, ga, gb, gnorm, gpost, gpre, gfpost, wpa, wpb, wout, wg, wu, wd):
    t = x.shape[0]
    n = t // TM
    row = lambda w: pl.BlockSpec((TM, w), lambda i: (i, 0))
    vec = _const_spec((1, D_MODEL))
    return pl.pallas_call(
        _merge_ffn_kernel,
        grid=(n,),
        in_specs=[row(D_MODEL), row(D_A), row(DV_B), row(DV_B), row(D_MODEL), row(D_MODEL),
                  _const_spec((1, DV_HEAD_B)), vec, vec, vec,
                  _const_spec(wpa.shape), _const_spec(wpb.shape), _const_spec(wout.shape),
                  _const_spec(wg.shape), _const_spec(wu.shape), _const_spec(wd.shape)],
        out_specs=row(D_MODEL),
        out_shape=jax.ShapeDtypeStruct((t, D_MODEL), F32),
        scratch_shapes=[pltpu.VMEM((TM, D_MODEL), F32), pltpu.VMEM((TM, D_MODEL), BF16),
                        pltpu.VMEM((TM, D_FF), BF16)],
        compiler_params=_params(),
        name="merge_ffn",
    )(x, oa, ob, rs, ga, gb, gnorm, gpost, gpre, gfpost, wpa, wpb, wout, wg, wu, wd)


def _rel_bias_row(table):
    n_hi = BAND_PAST + BIAS_ORIGIN - REL_CLIP
    n_lo = BIAS_F_LEN - n_hi - (2 * REL_CLIP + 1)
    h = table.shape[0]
    return jnp.concatenate([jnp.broadcast_to(table[:, -1:], (h, n_hi)), table[:, ::-1],
                            jnp.broadcast_to(table[:, :1], (h, n_lo))], axis=1)


def kernel(x_prompt, x_sample, cache_attn_k, cache_attn_v, state_gla, norm_mix_pre, norm_mix_post, norm_ffn_pre, norm_ffn_post, w_in, w_decay_up, b_decay, rel_bias, gla_norm, w_proj_a, w_proj_b, w_out, w_ffn_gate, w_ffn_up, w_ffn_down):
    depth = w_in.shape[0]
    assert depth == 1, "single-layer step"
    batch, seq, _ = x_prompt.shape
    dec_batch, dec_seq, _ = x_sample.shape
    assert batch == 1 and seq % TM == 0 and (dec_batch * dec_seq) % TM == 0
    assert seq % (ATT_SUB * ATT_QB) == 0
    past = cache_attn_k.shape[2]

    wt_bf = jnp.transpose(w_in[0]).astype(BF16)
    wup = w_decay_up[0].astype(BF16)
    bdec = b_decay[0][None, :]
    vec = lambda a: a[0][None, :]
    wpa, wpb, wout = w_proj_a[0].astype(BF16), w_proj_b[0].astype(BF16), w_out[0].astype(BF16)
    wg, wu, wd = w_ffn_gate[0].astype(BF16), w_ffn_up[0].astype(BF16), w_ffn_down[0].astype(BF16)
    gnorm = gla_norm[0][None, :]

    def layer_tail(x, oa, ob, rs, ga, gb):
        return _merge_ffn(x, oa, ob, rs, ga, gb, gnorm, vec(norm_mix_post), vec(norm_ffn_pre), vec(norm_ffn_post),
                          wpa, wpb, wout, wg, wu, wd)

    xp = x_prompt[0]
    qa, ka, va, qb, kb, vb, rs, la, ga, gb, kf, vf = _in_proj(
        xp, vec(norm_mix_pre), wt_bf, wup, bdec, kv_rows_every_step=False)
    f_tab = _rel_bias_row(rel_bias[0])
    oa = _attn_prompt(qa, ka, va, f_tab)
    s0 = jnp.zeros((N_HEADS_B, DK_HEAD_B, DV_HEAD_B), F32)
    ob, sp = _gla_stream(qb, kb, vb, la, s0, c=GLA_CHUNK, tb=GLA_TB)
    yp = layer_tail(xp, oa, ob, rs, ga, gb)
    keep = min(BAND_PAST, seq)
    assert keep == TM

    xs = x_sample.reshape(dec_batch * dec_seq, D_MODEL)
    qa, ka, va, qb, kb, vb, rs, la, ga, gb, kfs, vfs = _in_proj(
        xs, vec(norm_mix_pre), wt_bf, wup, bdec, kv_rows_every_step=True)
    assert past == BAND_PAST
    oa = _attn_sample(qa, kfs, vfs, jnp.transpose(cache_attn_k[0], (0, 2, 3, 1)),
                      jnp.transpose(cache_attn_v[0], (0, 2, 3, 1)), f_tab, seq=dec_seq)
    gla_chunk = CHUNK if dec_seq % CHUNK == 0 else dec_seq
    assert gla_chunk == dec_seq
    ob, ss = _gla_seqs(qb, kb, vb, la, state_gla[0], c=gla_chunk)
    ys = layer_tail(xs, oa, ob, rs, ga, gb)

    hd = (N_HEADS_A, HEAD_DIM_A)
    return (yp[None], ys.reshape(dec_batch, dec_seq, D_MODEL),
            kf.reshape((1, 1, keep) + hd), vf.reshape((1, 1, keep) + hd), sp[None, None],
            kfs.reshape((1, dec_batch, dec_seq) + hd), vfs.reshape((1, dec_batch, dec_seq) + hd), ss[None])
```

```python
import functools

import numpy as np
import jax
import jax.numpy as jnp
from jax import lax
from jax.experimental import pallas as pl
from jax.experimental.pallas import tpu as pltpu

F32 = jnp.float32
BF16 = jnp.bfloat16

D_MODEL = 1024
CHUNK = 64
BAND_CHUNKS = 8
BAND_PAST = BAND_CHUNKS * CHUNK
N_HEADS_A = 8
HEAD_DIM_A = 64
D_A = N_HEADS_A * HEAD_DIM_A
REL_CLIP = 128
N_HEADS_B = 4
DK_HEAD_B = 128
DV_HEAD_B = 256
DK_B = N_HEADS_B * DK_HEAD_B
DV_B = N_HEADS_B * DV_HEAD_B
GATE_RANK = 16
GATE_TEMP = 16.0
D_FF = 2816
EPS = 1e-6

LANES = 128
SUBLANES = 8
BF16_ROWS = 16
LOG2E = 1.4426950408889634
VMEM_LIMIT_BYTES = 56 * 1024 * 1024
NEG_BIG = -1e30

_PIECES = (("qa", D_A), ("ka", D_A), ("va", D_A), ("qb", DK_B), ("kb", DK_B), ("vb", DV_B),
           ("rb", DV_B), ("dlr", GATE_RANK), ("ga", D_MODEL), ("gb", D_MODEL))
_OFF = {}
_o = 0
for _n, _w in _PIECES:
    _OFF[_n] = (_o, _o + _w)
    _o += _w
D_IN = _o
assert all(lo % BF16_ROWS == 0 for lo, _ in _OFF.values())

TM = 512
ATT_QB = 256
ATT_WIN = ATT_QB + BAND_PAST
ATT_SUB = 4
ATT_MASKED = BAND_PAST // ATT_QB
assert ATT_MASKED <= ATT_SUB
BIAS_ORIGIN = ATT_QB
BIAS_F_LEN = 1024
GLA_TB = 1024
GLA_CHUNK = 128
GLA_UNROLL = 8
GLA_SEQS_PER_STEP = 8
ATTS_SEQS_PER_STEP = 4
FF_CHUNK = 256
N_FF_CHUNKS = D_FF // FF_CHUNK
MERGE_ROW_GROUPS = 2


def _const_spec(shape):
    nd = len(shape)
    return pl.BlockSpec(shape, lambda i: (0,) * nd, pipeline_mode=pl.Buffered(1))


def _params():
    return pltpu.CompilerParams(dimension_semantics=("arbitrary",), vmem_limit_bytes=VMEM_LIMIT_BYTES)


def _rms(x):
    return x * lax.rsqrt(jnp.mean(x * x, axis=-1, keepdims=True) + EPS)


def _dot(a, b):
    return jnp.dot(a, b, preferred_element_type=F32)


def _dot_nt(a, b):
    return lax.dot_general(a, b, (((1,), (1,)), ((), ())), preferred_element_type=F32)


def _dot_tn(a, b):
    return lax.dot_general(a, b, (((0,), (0,)), ((), ())), preferred_element_type=F32)


def _in_proj_kernel(x_ref, g_ref, wt_ref, wup_ref, bdec_ref,
                    qa_ref, ka_ref, va_ref, qb_ref, kb_ref, vb_ref, rs_ref, la_ref, ga_ref, gb_ref,
                    kf_ref, vf_ref):
    h = (_rms(x_ref[...]) * g_ref[...]).astype(BF16)

    def proj(name):
        lo, hi = _OFF[name]
        return _dot_nt(h, wt_ref[lo:hi, :])

    dlr = proj("dlr").astype(BF16)
    qa_ref[...] = (proj("qa") * (HEAD_DIM_A ** -0.5 * LOG2E)).astype(BF16)
    ka = proj("ka")
    va = proj("va")
    ka_ref[...] = ka.astype(BF16)
    va_ref[...] = va.astype(BF16)
    kf_ref[...] = ka
    vf_ref[...] = va
    z = (_dot(dlr, wup_ref[...]) + bdec_ref[...]) * LOG2E
    la_ref[...] = (jnp.minimum(z, 0.0) - jnp.log2(1.0 + jnp.exp2(-jnp.abs(z)))) * (1.0 / GATE_TEMP)
    qb_ref[...] = (proj("qb") * (DK_HEAD_B ** -0.5)).astype(BF16)
    kb_ref[...] = proj("kb").astype(BF16)
    vb_ref[...] = proj("vb").astype(BF16)
    r = proj("rb")
    rs_ref[...] = (r * jax.nn.sigmoid(r)).astype(BF16)
    ga_ref[...] = proj("ga").astype(BF16)
    gb_ref[...] = proj("gb").astype(BF16)


def _in_proj(x, g, wt_bf, wup, bdec, *, kv_rows_every_step):
    t = x.shape[0]
    n = t // TM
    row = lambda w: pl.BlockSpec((TM, w), lambda i: (i, 0))
    if kv_rows_every_step:
        kv_rows, kv_spec = t, row(D_A)
    else:
        kv_rows, kv_spec = TM, pl.BlockSpec((TM, D_A), lambda i: (0, 0))
    widths = (D_A, D_A, D_A, DK_B, DK_B, DV_B, DV_B, DK_B, D_MODEL, D_MODEL)
    dtypes = (BF16,) * 7 + (F32, BF16, BF16)
    out_shape = tuple(jax.ShapeDtypeStruct((t, w), d) for w, d in zip(widths, dtypes))
    out_shape += (jax.ShapeDtypeStruct((kv_rows, D_A), F32),) * 2
    out_specs = tuple(row(w) for w in widths) + (kv_spec, kv_spec)
    return pl.pallas_call(
        _in_proj_kernel,
        grid=(n,),
        in_specs=[row(D_MODEL), _const_spec((1, D_MODEL)), _const_spec((D_IN, D_MODEL)),
                  _const_spec((GATE_RANK, DK_B)), _const_spec((1, DK_B))],
        out_specs=out_specs,
        out_shape=out_shape,
        compiler_params=_params(),
        name="in_proj",
    )(x, g, wt_bf, wup, bdec)


def _softmax_pv(s, v):
    m = jnp.max(s, axis=-1, keepdims=True)
    e = jnp.exp2(s - m).astype(BF16)
    pv = _dot(e, jnp.concatenate([v, jnp.ones_like(v)], axis=1))
    return pv[:, :LANES] / pv[:, LANES:LANES + 1]


def _head_pair_attention(q_pair, k_pair, v_pair, bias0, bias1):
    nq = q_pair.shape[0]
    lane = lax.broadcasted_iota(jnp.int32, (1, LANES), 1)
    first = lane < HEAD_DIM_A
    zero = jnp.zeros_like(q_pair)
    qs = jnp.concatenate([jnp.where(first, q_pair, zero), jnp.where(first, zero, q_pair)], axis=0)
    s = _dot_nt(qs, k_pair) + jnp.concatenate([bias0, bias1], axis=0)
    pv = _softmax_pv(s, v_pair)
    return jnp.where(first, pv[:nq], pv[nq:])


def _expand_rel_bias(f_ref, bias_ref, n_q, n_k, first_valid, n_valid, band):
    t = lax.broadcasted_iota(jnp.int32, (n_q, n_k), 0)
    k = lax.broadcasted_iota(jnp.int32, (n_q, n_k), 1)
    keep = (k < n_valid) & (k >= first_valid)
    if band:
        jq, jk = t // CHUNK, k // CHUNK
        keep = keep & (jk >= jq) & (jk <= jq + BAND_CHUNKS)
    for h in range(N_HEADS_A):
        rows = jnp.broadcast_to(f_ref[h:h + 1, :], (n_q, BIAS_F_LEN))
        rolled = pltpu.roll(rows, BIAS_F_LEN - BIAS_ORIGIN, 1, stride=1, stride_axis=0)
        bias_ref[h] = jnp.where(keep, rolled[:, :n_k] * LOG2E, NEG_BIG)


def _attn_prompt_kernel(q_ref, *refs):
    n_kb = ATT_SUB + ATT_WIN // ATT_QB - 1
    k_refs, v_refs = refs[:n_kb], refs[n_kb:2 * n_kb]
    f_ref, o_ref, bias_ref = refs[2 * n_kb:]
    i = pl.program_id(0)

    @pl.when(i == 0)
    def _():
        _expand_rel_bias(f_ref, bias_ref.at[0], ATT_QB, ATT_WIN, 0, ATT_WIN, band=True)
        for g in range(ATT_MASKED):
            _expand_rel_bias(f_ref, bias_ref.at[1 + g], ATT_QB, ATT_WIN, BAND_PAST - ATT_QB * g, ATT_WIN,
                             band=True)

    for sub in range(ATT_SUB):
        kwin = jnp.concatenate([r[...] for r in k_refs[sub:sub + ATT_WIN // ATT_QB]], axis=0)
        vwin = jnp.concatenate([r[...] for r in v_refs[sub:sub + ATT_WIN // ATT_QB]], axis=0)
        rows = slice(sub * ATT_QB, (sub + 1) * ATT_QB)
        which = jnp.where(i == 0, 1 + sub, 0) if sub < ATT_MASKED else 0
        for p in range(N_HEADS_A // 2):
            sl = slice(p * LANES, (p + 1) * LANES)
            o = _head_pair_attention(q_ref[rows, sl], kwin[:, sl], vwin[:, sl],
                                     bias_ref[which, 2 * p], bias_ref[which, 2 * p + 1])
            o_ref[rows, sl] = o.astype(BF16)


def _attn_prompt(qa, ka, va, f_tab):
    t = qa.shape[0]
    n = t // (ATT_SUB * ATT_QB)
    n_kb = ATT_SUB + ATT_WIN // ATT_QB - 1
    kblk = [pl.BlockSpec((ATT_QB, D_A), functools.partial(
        lambda j, i: (jnp.maximum(ATT_SUB * i + j - (n_kb - ATT_SUB), 0), 0), j)) for j in range(n_kb)]
    qblk = pl.BlockSpec((ATT_SUB * ATT_QB, D_A), lambda i: (i, 0))
    return pl.pallas_call(
        _attn_prompt_kernel,
        grid=(n,),
        in_specs=[qblk] + kblk + kblk + [_const_spec((N_HEADS_A, BIAS_F_LEN))],
        out_specs=qblk,
        out_shape=jax.ShapeDtypeStruct((t, D_A), BF16),
        scratch_shapes=[pltpu.VMEM((1 + ATT_MASKED, N_HEADS_A, ATT_QB, ATT_WIN), F32)],
        compiler_params=_params(),
        name="attn_prompt",
    )(qa, *([ka] * n_kb), *([va] * n_kb), f_tab)


def _attn_sample_kernel(q_ref, kt_ref, vt_ref, kn_ref, vn_ref, f_ref, o_ref, bias_ref, *, n_seqs):
    n_q, w = q_ref.shape[0] // n_seqs, kt_ref.shape[-1]

    @pl.when(pl.program_id(0) == 0)
    def _():
        _expand_rel_bias(f_ref, bias_ref, n_q, bias_ref.shape[-1], 0, w + n_q, band=False)

    lane = lax.broadcasted_iota(jnp.int32, (1, LANES), 1)
    first = lane < HEAD_DIM_A
    for j, p in [(j, p) for j in range(n_seqs) for p in range(N_HEADS_A // 2)]:
        rows = slice(j * n_q, (j + 1) * n_q)
        sl = slice(p * LANES, (p + 1) * LANES)
        q_pair = q_ref[rows, sl]
        zero = jnp.zeros_like(q_pair)
        qs = jnp.concatenate([jnp.where(first, q_pair, zero), jnp.where(first, zero, q_pair)], axis=0)
        kt = jnp.concatenate([kt_ref[j, 2 * p], kt_ref[j, 2 * p + 1]], axis=0).astype(BF16)
        vt = jnp.concatenate([vt_ref[j, 2 * p], vt_ref[j, 2 * p + 1]], axis=0).astype(BF16)
        kn = kn_ref[rows, sl].astype(BF16)
        vn = vn_ref[rows, sl].astype(BF16)
        bias = jnp.concatenate([bias_ref[2 * p], bias_ref[2 * p + 1]], axis=0)
        s_c = _dot(qs, kt) + bias[:, :w]
        s_n = _dot_nt(qs, kn) + bias[:, w:w + n_q]
        m = jnp.maximum(jnp.max(s_c, axis=-1, keepdims=True), jnp.max(s_n, axis=-1, keepdims=True))
        e_c = jnp.exp2(s_c - m).astype(BF16)
        e_n = jnp.exp2(s_n - m).astype(BF16)
        pv = (_dot_nt(e_c, jnp.concatenate([vt, jnp.ones_like(vt)], axis=0))
              + _dot(e_n, jnp.concatenate([vn, jnp.ones_like(vn)], axis=1)))
        pv = pv[:, :LANES] / pv[:, LANES:LANES + 1]
        o_ref[rows, sl] = jnp.where(first, pv[:n_q], pv[n_q:]).astype(BF16)


def _attn_sample(qa, kn, vn, cache_kt, cache_vt, f_tab, *, seq):
    t = qa.shape[0]
    nb, _, _, w = cache_kt.shape
    n_k = w + seq
    n_seqs = ATTS_SEQS_PER_STEP
    assert nb % n_seqs == 0
    row = pl.BlockSpec((n_seqs * seq, D_A), lambda i: (i, 0))
    cache = pl.BlockSpec((n_seqs, N_HEADS_A, HEAD_DIM_A, w), lambda i: (i, 0, 0, 0))
    return pl.pallas_call(
        functools.partial(_attn_sample_kernel, n_seqs=n_seqs),
        grid=(nb // n_seqs,),
        in_specs=[row, cache, cache, row, row, _const_spec((N_HEADS_A, BIAS_F_LEN))],
        out_specs=row,
        out_shape=jax.ShapeDtypeStruct((t, D_A), BF16),
        scratch_shapes=[pltpu.VMEM((N_HEADS_A, seq, n_k + (-n_k) % LANES), F32)],
        compiler_params=_params(),
        name="attn_sample",
    )(qa, cache_kt, cache_vt, kn, vn, f_tab)


def _gla_levels(c):
    out, hs = [], 1
    while hs < c:
        out.append(hs)
        hs *= 2
    return out


def _gla_masks(c):
    t = np.arange(c)
    masks = []
    for hs in _gla_levels(c):
        blk = t // (2 * hs)
        second = (t % (2 * hs)) >= hs
        masks.append((blk[:, None] == blk[None, :]) & second[:, None] & ~second[None, :])
    masks.append(np.eye(c, dtype=bool))
    return np.stack(masks).astype(np.float32)


def _split_row_bcast(x, hs, row):
    c, w = x.shape
    blk = 2 * hs
    if blk >= SUBLANES:
        xr = x.reshape(c // blk, blk, w)
        return jnp.broadcast_to(xr[:, hs - 1:hs, :], (c // blk, blk, w)).reshape(c, w)
    tiles = (c // SUBLANES, SUBLANES, w)
    x3, pos = x.reshape(tiles), (row & (blk - 1)).reshape(tiles)
    if hs == 1:
        out = jnp.where(pos == 1, pltpu.roll(x3, 1, 1), x3)
    else:
        assert hs == 2
        nxt = jnp.where((pos & 1) == 1, x3, pltpu.roll(x3, SUBLANES - 1, 1))
        out = jnp.where(pos >= 2, pltpu.roll(nxt, 2, 1), nxt)
    return out.reshape(c, w)


def _level_step(p, qf, kf, hs, row):
    c, w = p.shape
    if hs < SUBLANES:
        tot = _split_row_bcast(p, hs, row)
        second = (row & hs) != 0
        return jnp.where(second, p, tot - p), jnp.where(second, p + tot, p), jnp.where(second, qf, kf)
    xs, ps, qks = [], [], []
    for lo in range(0, c, 2 * hs):
        first, second = p[lo:lo + hs], p[lo + hs:lo + 2 * hs]
        tot = jnp.broadcast_to(first[hs - 1:hs], (hs, w))
        xs += [tot - first, second]
        ps += [first, second + tot]
        qks += [kf[lo:lo + hs], qf[lo + hs:lo + 2 * hs]]
    if hs % BF16_ROWS == 0:
        qk = jnp.concatenate(qks, axis=0)
    else:
        qk = jnp.where((row & hs) != 0, qf, kf)
    return jnp.concatenate(xs, axis=0), jnp.concatenate(ps, axis=0), qk


def _gla_chunk(rows, c, q_ref, k_ref, v_ref, la_ref, mask_ref, st_in, st_out, o_ref, *, state_t):
    levels = _gla_levels(c)
    n_lvl = len(levels)
    row = lax.broadcasted_iota(jnp.int32, (c, DK_HEAD_B), 0)
    for h in range(N_HEADS_B):
        hk = slice(h * DK_HEAD_B, (h + 1) * DK_HEAD_B)
        hv = slice(h * DV_HEAD_B, (h + 1) * DV_HEAD_B)
        qf = q_ref[rows, hk]
        kf = k_ref[rows, hk]
        vh = v_ref[rows, hv]
        p = la_ref[rows, hk]
        a = mask_ref[n_lvl] * _dot_nt(qf, kf)
        for l, hs in enumerate(levels):
            x, p, qk = _level_step(p, qf, kf, hs, row)
            z = qk * jnp.exp2(x).astype(BF16)
            a = a + mask_ref[l] * _dot_nt(z, z)
        b = p
        b_last = b[c - 1:c, :]
        o = _dot(a.astype(BF16), vh)
        st = st_in[h]
        qd = qf * jnp.exp2(b).astype(BF16)
        kd = kf * jnp.exp2(b_last - b).astype(BF16)
        d_last = jnp.exp2(b_last)
        if state_t:
            o = o + _dot_nt(qd, st.astype(BF16))
            st_out[h] = st * d_last + _dot_tn(vh, kd)
        else:
            o = o + _dot(qd, st.astype(BF16))
            d_col = jnp.broadcast_to(d_last, (SUBLANES, DK_HEAD_B)).T[:, 0:1]
            st_out[h] = st * d_col + _dot_tn(kd, vh)
        o_ref[rows, hv] = o.astype(BF16)


def _gla_stream_kernel(q_ref, k_ref, v_ref, la_ref, mask_ref, s0_ref,
                       o_ref, sout_ref, st_ref, *, c, n_chunks, unroll):
    @pl.when(pl.program_id(0) == 0)
    def _():
        for h in range(N_HEADS_B):
            st_ref[h] = s0_ref[h].T

    def chunk(ci, carry):
        rows = pl.ds(pl.multiple_of(ci * c, c), c)
        _gla_chunk(rows, c, q_ref, k_ref, v_ref, la_ref, mask_ref, st_ref, st_ref, o_ref, state_t=True)
        return carry

    lax.fori_loop(0, n_chunks, chunk, 0, unroll=unroll)

    @pl.when(pl.program_id(0) == pl.num_programs(0) - 1)
    def _():
        for h in range(N_HEADS_B):
            sout_ref[h] = st_ref[h].T


def _gla_seqs_kernel(q_ref, k_ref, v_ref, la_ref, mask_ref, s0_ref, o_ref, sout_ref, *, c, n_seqs):
    for j in range(n_seqs):
        _gla_chunk(slice(j * c, (j + 1) * c), c, q_ref, k_ref, v_ref, la_ref, mask_ref,
                   s0_ref.at[j], sout_ref.at[j], o_ref, state_t=False)


def _gla_stream(qb, kb, vb, la, s0, *, c, tb):
    t = qb.shape[0]
    masks = jnp.asarray(_gla_masks(c), F32)
    row = lambda w: pl.BlockSpec((tb, w), lambda i: (i, 0))
    st_shape = (N_HEADS_B, DK_HEAD_B, DV_HEAD_B)
    s_spec = pl.BlockSpec(st_shape, lambda i: (0, 0, 0))
    n_chunks = tb // c
    return pl.pallas_call(
        functools.partial(_gla_stream_kernel, c=c, n_chunks=n_chunks, unroll=min(GLA_UNROLL, n_chunks)),
        grid=(t // tb,),
        in_specs=[row(DK_B), row(DK_B), row(DV_B), row(DK_B), _const_spec(masks.shape), s_spec],
        out_specs=(row(DV_B), s_spec),
        out_shape=(jax.ShapeDtypeStruct((t, DV_B), BF16), jax.ShapeDtypeStruct(st_shape, F32)),
        scratch_shapes=[pltpu.VMEM((N_HEADS_B, DV_HEAD_B, DK_HEAD_B), F32)],
        compiler_params=_params(),
        name="gla_stream",
    )(qb, kb, vb, la, masks, s0)


def _gla_seqs(qb, kb, vb, la, s0, *, c):
    t = qb.shape[0]
    n = s0.shape[0]
    assert t == n * c and n % GLA_SEQS_PER_STEP == 0
    masks = jnp.asarray(_gla_masks(c), F32)
    tb = GLA_SEQS_PER_STEP * c
    row = lambda w: pl.BlockSpec((tb, w), lambda i: (i, 0))
    s_spec = pl.BlockSpec((GLA_SEQS_PER_STEP,) + s0.shape[1:], lambda i: (i, 0, 0, 0))
    return pl.pallas_call(
        functools.partial(_gla_seqs_kernel, c=c, n_seqs=GLA_SEQS_PER_STEP),
        grid=(n // GLA_SEQS_PER_STEP,),
        in_specs=[row(DK_B), row(DK_B), row(DV_B), row(DK_B), _const_spec(masks.shape), s_spec],
        out_specs=(row(DV_B), s_spec),
        out_shape=(jax.ShapeDtypeStruct((t, DV_B), BF16), jax.ShapeDtypeStruct(s0.shape, F32)),
        compiler_params=_params(),
        name="gla_seqs",
    )(qb, kb, vb, la, masks, s0)


def _merge_ffn_kernel(x_ref, oa_ref, ob_ref, rs_ref, ga_ref, gb_ref, gn_ref, gpost_ref, gpre_ref, gfpost_ref,
                      wpa_ref, wpb_ref, wout_ref, wg_ref, wu_ref, wd_ref, y_ref, x1_ref, h_ref, act_ref):
    group_rows = TM // MERGE_ROW_GROUPS
    groups = [slice(r * group_rows, (r + 1) * group_rows) for r in range(MERGE_ROW_GROUPS)]
    for rows in groups:
        pa = _dot(oa_ref[rows, :], wpa_ref[...])
        ob = jnp.concatenate(
            [_rms(ob_ref[rows, h * DV_HEAD_B:(h + 1) * DV_HEAD_B].astype(F32)) * gn_ref[...]
             for h in range(N_HEADS_B)], axis=1) * rs_ref[rows, :].astype(F32)
        pb = _dot(ob.astype(BF16), wpb_ref[...])
        mix = jax.nn.sigmoid(ga_ref[rows, :].astype(F32)) * pa + jax.nn.sigmoid(gb_ref[rows, :].astype(F32)) * pb
        x1 = x_ref[rows, :] + _rms(_dot(mix.astype(BF16), wout_ref[...])) * gpost_ref[...]
        x1_ref[rows, :] = x1
        h_ref[rows, :] = (_rms(x1) * gpre_ref[...]).astype(BF16)

    for j in range(N_FF_CHUNKS):
        cols = slice(j * FF_CHUNK, (j + 1) * FF_CHUNK)
        for rows in (groups if j == 0 else [slice(0, TM)]):
            h = h_ref[rows, :]
            g = _dot(h, wg_ref[:, cols])
            u = _dot(h, wu_ref[:, cols])
            act_ref[rows, cols] = (g * jax.nn.sigmoid(g) * u).astype(BF16)
    y_ref[...] = x1_ref[...] + _rms(_dot(act_ref[...], wd_ref[...])) * gfpost_ref[...]


def _merge_ffn(x, oa, ob, rs, ga, gb, gnorm, gpost, gpre, gfpost, wpa, wpb, wout, wg, wu, wd):
    t = x.shape[0]
    n = t // TM
    row = lambda w: pl.BlockSpec((TM, w), lambda i: (i, 0))
    vec = _const_spec((1, D_MODEL))
    return pl.pallas_call(
        _merge_ffn_kernel,
        grid=(n,),
        in_specs=[row(D_MODEL), row(D_A), row(DV_B), row(DV_B), row(D_MODEL), row(D_MODEL),
                  _const_spec((1, DV_HEAD_B)), vec, vec, vec,
                  _const_spec(wpa.shape), _const_spec(wpb.shape), _const_spec(wout.shape),
                  _const_spec(wg.shape), _const_spec(wu.shape), _const_spec(wd.shape)],
        out_specs=row(D_MODEL),
        out_shape=jax.ShapeDtypeStruct((t, D_MODEL), F32),
        scratch_shapes=[pltpu.VMEM((TM, D_MODEL), F32), pltpu.VMEM((TM, D_MODEL), BF16),
                        pltpu.VMEM((TM, D_FF), BF16)],
        compiler_params=_params(),
        name="merge_ffn",
    )(x, oa, ob, rs, ga, gb, gnorm, gpost, gpre, gfpost, wpa, wpb, wout, wg, wu, wd)


def _rel_bias_row(table):
    n_hi = BAND_PAST + BIAS_ORIGIN - REL_CLIP
    n_lo = BIAS_F_LEN - n_hi - (2 * REL_CLIP + 1)
    h = table.shape[0]
    return jnp.concatenate([jnp.broadcast_to(table[:, -1:], (h, n_hi)), table[:, ::-1],
                            jnp.broadcast_to(table[:, :1], (h, n_lo))], axis=1)


def kernel(x_prompt, x_sample, cache_attn_k, cache_attn_v, state_gla, norm_mix_pre, norm_mix_post, norm_ffn_pre, norm_ffn_post, w_in, w_decay_up, b_decay, rel_bias, gla_norm, w_proj_a, w_proj_b, w_out, w_ffn_gate, w_ffn_up, w_ffn_down):
    depth = w_in.shape[0]
    assert depth == 1, "single-layer step"
    batch, seq, _ = x_prompt.shape
    dec_batch, dec_seq, _ = x_sample.shape
    assert batch == 1 and seq % TM == 0 and (dec_batch * dec_seq) % TM == 0
    assert seq % (ATT_SUB * ATT_QB) == 0
    past = cache_attn_k.shape[2]

    wt_bf = jnp.transpose(w_in[0]).astype(BF16)
    wup = w_decay_up[0].astype(BF16)
    bdec = b_decay[0][None, :]
    vec = lambda a: a[0][None, :]
    wpa, wpb, wout = w_proj_a[0].astype(BF16), w_proj_b[0].astype(BF16), w_out[0].astype(BF16)
    wg, wu, wd = w_ffn_gate[0].astype(BF16), w_ffn_up[0].astype(BF16), w_ffn_down[0].astype(BF16)
    gnorm = gla_norm[0][None, :]

    def layer_tail(x, oa, ob, rs, ga, gb):
        return _merge_ffn(x, oa, ob, rs, ga, gb, gnorm, vec(norm_mix_post), vec(norm_ffn_pre), vec(norm_ffn_post),
                          wpa, wpb, wout, wg, wu, wd)

    xp = x_prompt[0]
    qa, ka, va, qb, kb, vb, rs, la, ga, gb, kf, vf = _in_proj(
        xp, vec(norm_mix_pre), wt_bf, wup, bdec, kv_rows_every_step=False)
    f_tab = _rel_bias_row(rel_bias[0])
    oa = _attn_prompt(qa, ka, va, f_tab)
    s0 = jnp.zeros((N_HEADS_B, DK_HEAD_B, DV_HEAD_B), F32)
    ob, sp = _gla_stream(qb, kb, vb, la, s0, c=GLA_CHUNK, tb=GLA_TB)
    yp = layer_tail(xp, oa, ob, rs, ga, gb)
    keep = min(BAND_PAST, seq)
    assert keep == TM

    xs = x_sample.reshape(dec_batch * dec_seq, D_MODEL)
    qa, ka, va, qb, kb, vb, rs, la, ga, gb, kfs, vfs = _in_proj(
        xs, vec(norm_mix_pre), wt_bf, wup, bdec, kv_rows_every_step=True)
    assert past == BAND_PAST
    oa = _attn_sample(qa, kfs, vfs, jnp.transpose(cache_attn_k[0], (0, 2, 3, 1)),
                      jnp.transpose(cache_attn_v[0], (0, 2, 3, 1)), f_tab, seq=dec_seq)
    gla_chunk = CHUNK if dec_seq % CHUNK == 0 else dec_seq
    assert gla_chunk == dec_seq
    ob, ss = _gla_seqs(qb, kb, vb, la, state_gla[0], c=gla_chunk)
    ys = layer_tail(xs, oa, ob, rs, ga, gb)

    hd = (N_HEADS_A, HEAD_DIM_A)
    return (yp[None], ys.reshape(dec_batch, dec_seq, D_MODEL),
            kf.reshape((1, 1, keep) + hd), vf.reshape((1, 1, keep) + hd), sp[None, None],
            kfs.reshape((1, dec_batch, dec_seq) + hd), vfs.reshape((1, dec_batch, dec_seq) + hd), ss[None])
```

```python
import functools

import numpy as np
import jax
import jax.numpy as jnp
from jax import lax
from jax.experimental import pallas as pl
from jax.experimental.pallas import tpu as pltpu

F32 = jnp.float32
BF16 = jnp.bfloat16

D_MODEL = 1024
CHUNK = 64
BAND_CHUNKS = 8
BAND_PAST = BAND_CHUNKS * CHUNK
N_HEADS_A = 8
HEAD_DIM_A = 64
D_A = N_HEADS_A * HEAD_DIM_A
REL_CLIP = 128
N_HEADS_B = 4
DK_HEAD_B = 128
DV_HEAD_B = 256
DK_B = N_HEADS_B * DK_HEAD_B
DV_B = N_HEADS_B * DV_HEAD_B
GATE_RANK = 16
GATE_TEMP = 16.0
D_FF = 2816
EPS = 1e-6

LANES = 128
SUBLANES = 8
BF16_ROWS = 16
LOG2E = 1.4426950408889634
VMEM_LIMIT_BYTES = 56 * 1024 * 1024
NEG_BIG = -1e30

_PIECES = (("qa", D_A), ("ka", D_A), ("va", D_A), ("qb", DK_B), ("kb", DK_B), ("vb", DV_B),
           ("rb", DV_B), ("dlr", GATE_RANK), ("ga", D_MODEL), ("gb", D_MODEL))
_OFF = {}
_o = 0
for _n, _w in _PIECES:
    _OFF[_n] = (_o, _o + _w)
    _o += _w
D_IN = _o
assert all(lo % BF16_ROWS == 0 for lo, _ in _OFF.values())
assert DK_B == 4 * LANES

TM = 512
ATT_QB = 256
ATT_WIN = ATT_QB + BAND_PAST
ATT_SUB = 4
ATT_MASKED = BAND_PAST // ATT_QB
assert ATT_MASKED <= ATT_SUB
BIAS_ORIGIN = ATT_QB
BIAS_F_LEN = 1024
GLA_TB = 1024
GLA_CHUNK = 128
GLA_UNROLL = 8
GLA_SEQS_PER_STEP = 8
ATTS_SEQS_PER_STEP = 4
FF_CHUNK = 256
N_FF_CHUNKS = D_FF // FF_CHUNK
MERGE_ROW_GROUPS = 2


def _const_spec(shape):
    nd = len(shape)
    return pl.BlockSpec(shape, lambda i: (0,) * nd, pipeline_mode=pl.Buffered(1))


def _params():
    return pltpu.CompilerParams(dimension_semantics=("arbitrary",), vmem_limit_bytes=VMEM_LIMIT_BYTES)


def _rms(x):
    return x * lax.rsqrt(jnp.mean(x * x, axis=-1, keepdims=True) + EPS)


def _dot(a, b):
    return jnp.dot(a, b, preferred_element_type=F32)


def _dot_nt(a, b):
    return lax.dot_general(a, b, (((1,), (1,)), ((), ())), preferred_element_type=F32)


def _dot_tn(a, b):
    return lax.dot_general(a, b, (((0,), (0,)), ((), ())), preferred_element_type=F32)


def _in_proj_kernel(x_ref, g_ref, wt_ref, wup_ref, bdec_ref,
                    qa_ref, ka_ref, va_ref, qb_ref, kb_ref, vb_ref, rs_ref, la_ref, ga_ref, gb_ref,
                    kf_ref, vf_ref):
    h = (_rms(x_ref[...]) * g_ref[...]).astype(BF16)

    def proj(name):
        lo, hi = _OFF[name]
        return _dot_nt(h, wt_ref[lo:hi, :])

    dlr = proj("dlr").astype(BF16)
    qa_ref[...] = (proj("qa") * (HEAD_DIM_A ** -0.5 * LOG2E)).astype(BF16)
    ka = proj("ka")
    va = proj("va")
    ka_ref[...] = ka.astype(BF16)
    va_ref[...] = va.astype(BF16)
    kf_ref[...] = ka
    vf_ref[...] = va

    def log_decay(part):
        cols = slice(part * LANES, (part + 1) * LANES)
        z = (_dot(dlr, wup_ref[:, cols]) + bdec_ref[:, cols]) * LOG2E
        la_ref[:, cols] = (jnp.minimum(z, 0.0) - jnp.log2(1.0 + jnp.exp2(-jnp.abs(z)))) * (1.0 / GATE_TEMP)

    log_decay(0)
    qb_ref[...] = (proj("qb") * (DK_HEAD_B ** -0.5)).astype(BF16)
    log_decay(1)
    kb_ref[...] = proj("kb").astype(BF16)
    log_decay(2)
    vb_ref[...] = proj("vb").astype(BF16)
    log_decay(3)
    half_r = 0.5 * proj("rb")
    rs_ref[...] = (half_r + half_r * jnp.tanh(half_r)).astype(BF16)
    ga_ref[...] = proj("ga").astype(BF16)
    gb_ref[...] = proj("gb").astype(BF16)


def _in_proj(x, g, wt_bf, wup, bdec, *, kv_rows_every_step):
    t = x.shape[0]
    n = t // TM
    row = lambda w: pl.BlockSpec((TM, w), lambda i: (i, 0))
    if kv_rows_every_step:
        kv_rows, kv_spec = t, row(D_A)
    else:
        kv_rows, kv_spec = TM, pl.BlockSpec((TM, D_A), lambda i: (0, 0))
    widths = (D_A, D_A, D_A, DK_B, DK_B, DV_B, DV_B, DK_B, D_MODEL, D_MODEL)
    dtypes = (BF16,) * 7 + (F32, BF16, BF16)
    out_shape = tuple(jax.ShapeDtypeStruct((t, w), d) for w, d in zip(widths, dtypes))
    out_shape += (jax.ShapeDtypeStruct((kv_rows, D_A), F32),) * 2
    out_specs = tuple(row(w) for w in widths) + (kv_spec, kv_spec)
    return pl.pallas_call(
        _in_proj_kernel,
        grid=(n,),
        in_specs=[row(D_MODEL), _const_spec((1, D_MODEL)), _const_spec((D_IN, D_MODEL)),
                  _const_spec((GATE_RANK, DK_B)), _const_spec((1, DK_B))],
        out_specs=out_specs,
        out_shape=out_shape,
        compiler_params=_params(),
        name="in_proj",
    )(x, g, wt_bf, wup, bdec)


def _softmax_pv(s, v):
    m = jnp.max(s, axis=-1, keepdims=True)
    e = jnp.exp2(s - m).astype(BF16)
    pv = _dot(e, jnp.concatenate([v, jnp.ones_like(v)], axis=1))
    return pv[:, :LANES] / pv[:, LANES:LANES + 1]


def _head_pair_attention(q_pair, k_pair, v_pair, bias0, bias1):
    nq = q_pair.shape[0]
    lane = lax.broadcasted_iota(jnp.int32, (1, LANES), 1)
    first = lane < HEAD_DIM_A
    zero = jnp.zeros_like(q_pair)
    qs = jnp.concatenate([jnp.where(first, q_pair, zero), jnp.where(first, zero, q_pair)], axis=0)
    s = _dot_nt(qs, k_pair) + jnp.concatenate([bias0, bias1], axis=0)
    pv = _softmax_pv(s, v_pair)
    return jnp.where(first, pv[:nq], pv[nq:])


def _expand_rel_bias(f_ref, bias_ref, n_q, n_k, first_valid, n_valid, band):
    t = lax.broadcasted_iota(jnp.int32, (n_q, n_k), 0)
    k = lax.broadcasted_iota(jnp.int32, (n_q, n_k), 1)
    keep = (k < n_valid) & (k >= first_valid)
    if band:
        jq, jk = t // CHUNK, k // CHUNK
        keep = keep & (jk >= jq) & (jk <= jq + BAND_CHUNKS)
    for h in range(N_HEADS_A):
        rows = jnp.broadcast_to(f_ref[h:h + 1, :], (n_q, BIAS_F_LEN))
        rolled = pltpu.roll(rows, BIAS_F_LEN - BIAS_ORIGIN, 1, stride=1, stride_axis=0)
        bias_ref[h] = jnp.where(keep, rolled[:, :n_k] * LOG2E, NEG_BIG)


def _attn_prompt_kernel(q_ref, *refs):
    n_kb = ATT_SUB + ATT_WIN // ATT_QB - 1
    k_refs, v_refs = refs[:n_kb], refs[n_kb:2 * n_kb]
    f_ref, o_ref, bias_ref = refs[2 * n_kb:]
    i = pl.program_id(0)

    @pl.when(i == 0)
    def _():
        _expand_rel_bias(f_ref, bias_ref.at[0], ATT_QB, ATT_WIN, 0, ATT_WIN, band=True)
        for g in range(ATT_MASKED):
            _expand_rel_bias(f_ref, bias_ref.at[1 + g], ATT_QB, ATT_WIN, BAND_PAST - ATT_QB * g, ATT_WIN,
                             band=True)

    for sub in range(ATT_SUB):
        kwin = jnp.concatenate([r[...] for r in k_refs[sub:sub + ATT_WIN // ATT_QB]], axis=0)
        vwin = jnp.concatenate([r[...] for r in v_refs[sub:sub + ATT_WIN // ATT_QB]], axis=0)
        rows = slice(sub * ATT_QB, (sub + 1) * ATT_QB)
        which = jnp.where(i == 0, 1 + sub, 0) if sub < ATT_MASKED else 0
        for p in range(N_HEADS_A // 2):
            sl = slice(p * LANES, (p + 1) * LANES)
            o = _head_pair_attention(q_ref[rows, sl], kwin[:, sl], vwin[:, sl],
                                     bias_ref[which, 2 * p], bias_ref[which, 2 * p + 1])
            o_ref[rows, sl] = o.astype(BF16)


def _attn_prompt(qa, ka, va, f_tab):
    t = qa.shape[0]
    n = t // (ATT_SUB * ATT_QB)
    n_kb = ATT_SUB + ATT_WIN // ATT_QB - 1
    kblk = [pl.BlockSpec((ATT_QB, D_A), functools.partial(
        lambda j, i: (jnp.maximum(ATT_SUB * i + j - (n_kb - ATT_SUB), 0), 0), j)) for j in range(n_kb)]
    qblk = pl.BlockSpec((ATT_SUB * ATT_QB, D_A), lambda i: (i, 0))
    return pl.pallas_call(
        _attn_prompt_kernel,
        grid=(n,),
        in_specs=[qblk] + kblk + kblk + [_const_spec((N_HEADS_A, BIAS_F_LEN))],
        out_specs=qblk,
        out_shape=jax.ShapeDtypeStruct((t, D_A), BF16),
        scratch_shapes=[pltpu.VMEM((1 + ATT_MASKED, N_HEADS_A, ATT_QB, ATT_WIN), F32)],
        compiler_params=_params(),
        name="attn_prompt",
    )(qa, *([ka] * n_kb), *([va] * n_kb), f_tab)


def _attn_sample_kernel(q_ref, kt_ref, vt_ref, kn_ref, vn_ref, f_ref, o_ref, bias_ref, *, n_seqs):
    n_q, w = q_ref.shape[0] // n_seqs, kt_ref.shape[-1]

    @pl.when(pl.program_id(0) == 0)
    def _():
        _expand_rel_bias(f_ref, bias_ref, n_q, bias_ref.shape[-1], 0, w + n_q, band=False)

    lane = lax.broadcasted_iota(jnp.int32, (1, LANES), 1)
    first = lane < HEAD_DIM_A
    for j, p in [(j, p) for j in range(n_seqs) for p in range(N_HEADS_A // 2)]:
        rows = slice(j * n_q, (j + 1) * n_q)
        sl = slice(p * LANES, (p + 1) * LANES)
        q_pair = q_ref[rows, sl]
        zero = jnp.zeros_like(q_pair)
        qs = jnp.concatenate([jnp.where(first, q_pair, zero), jnp.where(first, zero, q_pair)], axis=0)
        kt = jnp.concatenate([kt_ref[j, 2 * p], kt_ref[j, 2 * p + 1]], axis=0).astype(BF16)
        vt = jnp.concatenate([vt_ref[j, 2 * p], vt_ref[j, 2 * p + 1]], axis=0).astype(BF16)
        kn = kn_ref[rows, sl].astype(BF16)
        vn = vn_ref[rows, sl].astype(BF16)
        bias = jnp.concatenate([bias_ref[2 * p], bias_ref[2 * p + 1]], axis=0)
        s_c = _dot(qs, kt) + bias[:, :w]
        s_n = _dot_nt(qs, kn) + bias[:, w:w + n_q]
        m = jnp.maximum(jnp.max(s_c, axis=-1, keepdims=True), jnp.max(s_n, axis=-1, keepdims=True))
        e_c = jnp.exp2(s_c - m).astype(BF16)
        e_n = jnp.exp2(s_n - m).astype(BF16)
        pv = (_dot_nt(e_c, jnp.concatenate([vt, jnp.ones_like(vt)], axis=0))
              + _dot(e_n, jnp.concatenate([vn, jnp.ones_like(vn)], axis=1)))
        pv = pv[:, :LANES] / pv[:, LANES:LANES + 1]
        o_ref[rows, sl] = jnp.where(first, pv[:n_q], pv[n_q:]).astype(BF16)


def _attn_sample(qa, kn, vn, cache_kt, cache_vt, f_tab, *, seq):
    t = qa.shape[0]
    nb, _, _, w = cache_kt.shape
    n_k = w + seq
    n_seqs = ATTS_SEQS_PER_STEP
    assert nb % n_seqs == 0
    row = pl.BlockSpec((n_seqs * seq, D_A), lambda i: (i, 0))
    cache = pl.BlockSpec((n_seqs, N_HEADS_A, HEAD_DIM_A, w), lambda i: (i, 0, 0, 0))
    return pl.pallas_call(
        functools.partial(_attn_sample_kernel, n_seqs=n_seqs),
        grid=(nb // n_seqs,),
        in_specs=[row, cache, cache, row, row, _const_spec((N_HEADS_A, BIAS_F_LEN))],
        out_specs=row,
        out_shape=jax.ShapeDtypeStruct((t, D_A), BF16),
        scratch_shapes=[pltpu.VMEM((N_HEADS_A, seq, n_k + (-n_k) % LANES), F32)],
        compiler_params=_params(),
        name="attn_sample",
    )(qa, cache_kt, cache_vt, kn, vn, f_tab)


def _gla_levels(c):
    out, hs = [], 1
    while hs < c:
        out.append(hs)
        hs *= 2
    return out


def _gla_masks(c):
    t = np.arange(c)
    masks = []
    for hs in _gla_levels(c):
        blk = t // (2 * hs)
        second = (t % (2 * hs)) >= hs
        masks.append((blk[:, None] == blk[None, :]) & second[:, None] & ~second[None, :])
    masks.append(np.eye(c, dtype=bool))
    return np.stack(masks).astype(np.float32)


def _split_row_bcast(x, hs, row):
    c, w = x.shape
    blk = 2 * hs
    if blk >= SUBLANES:
        xr = x.reshape(c // blk, blk, w)
        return jnp.broadcast_to(xr[:, hs - 1:hs, :], (c // blk, blk, w)).reshape(c, w)
    tiles = (c // SUBLANES, SUBLANES, w)
    x3, pos = x.reshape(tiles), (row & (blk - 1)).reshape(tiles)
    if hs == 1:
        out = jnp.where(pos == 1, pltpu.roll(x3, 1, 1), x3)
    else:
        assert hs == 2
        nxt = jnp.where((pos & 1) == 1, x3, pltpu.roll(x3, SUBLANES - 1, 1))
        out = jnp.where(pos >= 2, pltpu.roll(nxt, 2, 1), nxt)
    return out.reshape(c, w)


def _level_step(p, qf, kf, hs, row):
    c, w = p.shape
    if hs < SUBLANES:
        tot = _split_row_bcast(p, hs, row)
        second = (row & hs) != 0
        return jnp.where(second, p, tot - p), jnp.where(second, p + tot, p), jnp.where(second, qf, kf)
    xs, ps, qks = [], [], []
    for lo in range(0, c, 2 * hs):
        first, second = p[lo:lo + hs], p[lo + hs:lo + 2 * hs]
        tot = jnp.broadcast_to(first[hs - 1:hs], (hs, w))
        xs += [tot - first, second]
        ps += [first, second + tot]
        qks += [kf[lo:lo + hs], qf[lo + hs:lo + 2 * hs]]
    if hs % BF16_ROWS == 0:
        qk = jnp.concatenate(qks, axis=0)
    else:
        qk = jnp.where((row & hs) != 0, qf, kf)
    return jnp.concatenate(xs, axis=0), jnp.concatenate(ps, axis=0), qk


def _gla_chunk(rows, c, q_ref, k_ref, v_ref, la_ref, mask_ref, st_in, st_out, o_ref, *, state_t):
    levels = _gla_levels(c)
    n_lvl = len(levels)
    row = lax.broadcasted_iota(jnp.int32, (c, DK_HEAD_B), 0)
    for h in range(N_HEADS_B):
        hk = slice(h * DK_HEAD_B, (h + 1) * DK_HEAD_B)
        hv = slice(h * DV_HEAD_B, (h + 1) * DV_HEAD_B)
        qf = q_ref[rows, hk]
        kf = k_ref[rows, hk]
        vh = v_ref[rows, hv]
        p = la_ref[rows, hk]
        a = mask_ref[n_lvl] * _dot_nt(qf, kf)
        for l, hs in enumerate(levels):
            x, p, qk = _level_step(p, qf, kf, hs, row)
            z = qk * jnp.exp2(x).astype(BF16)
            a = a + mask_ref[l] * _dot_nt(z, z)
        b = p
        b_last = b[c - 1:c, :]
        o = _dot(a.astype(BF16), vh)
        st = st_in[h]
        qd = qf * jnp.exp2(b).astype(BF16)
        kd = kf * jnp.exp2(b_last - b).astype(BF16)
        d_last = jnp.exp2(b_last)
        if state_t:
            o = o + _dot_nt(qd, st.astype(BF16))
            st_out[h] = st * d_last + _dot_tn(vh, kd)
        else:
            o = o + _dot(qd, st.astype(BF16))
            d_col = jnp.broadcast_to(d_last, (SUBLANES, DK_HEAD_B)).T[:, 0:1]
            st_out[h] = st * d_col + _dot_tn(kd, vh)
        o_ref[rows, hv] = o.astype(BF16)


def _gla_stream_kernel(q_ref, k_ref, v_ref, la_ref, mask_ref, s0_ref,
                       o_ref, sout_ref, st_ref, *, c, n_chunks, unroll):
    @pl.when(pl.program_id(0) == 0)
    def _():
        for h in range(N_HEADS_B):
            st_ref[h] = s0_ref[h].T

    def chunk(ci, carry):
        rows = pl.ds(pl.multiple_of(ci * c, c), c)
        _gla_chunk(rows, c, q_ref, k_ref, v_ref, la_ref, mask_ref, st_ref, st_ref, o_ref, state_t=True)
        return carry

    lax.fori_loop(0, n_chunks, chunk, 0, unroll=unroll)

    @pl.when(pl.program_id(0) == pl.num_programs(0) - 1)
    def _():
        for h in range(N_HEADS_B):
            sout_ref[h] = st_ref[h].T


def _gla_seqs_kernel(q_ref, k_ref, v_ref, la_ref, mask_ref, s0_ref, o_ref, sout_ref, *, c, n_seqs):
    for j in range(n_seqs):
        _gla_chunk(slice(j * c, (j + 1) * c), c, q_ref, k_ref, v_ref, la_ref, mask_ref,
                   s0_ref.at[j], sout_ref.at[j], o_ref, state_t=False)


def _gla_stream(qb, kb, vb, la, s0, *, c, tb):
    t = qb.shape[0]
    masks = jnp.asarray(_gla_masks(c), F32)
    row = lambda w: pl.BlockSpec((tb, w), lambda i: (i, 0))
    st_shape = (N_HEADS_B, DK_HEAD_B, DV_HEAD_B)
    s_spec = pl.BlockSpec(st_shape, lambda i: (0, 0, 0))
    n_chunks = tb // c
    return pl.pallas_call(
        functools.partial(_gla_stream_kernel, c=c, n_chunks=n_chunks, unroll=min(GLA_UNROLL, n_chunks)),
        grid=(t // tb,),
        in_specs=[row(DK_B), row(DK_B), row(DV_B), row(DK_B), _const_spec(masks.shape), s_spec],
        out_specs=(row(DV_B), s_spec),
        out_shape=(jax.ShapeDtypeStruct((t, DV_B), BF16), jax.ShapeDtypeStruct(st_shape, F32)),
        scratch_shapes=[pltpu.VMEM((N_HEADS_B, DV_HEAD_B, DK_HEAD_B), F32)],
        compiler_params=_params(),
        name="gla_stream",
    )(qb, kb, vb, la, masks, s0)


def _gla_seqs(qb, kb, vb, la, s0, *, c):
    t = qb.shape[0]
    n = s0.shape[0]
    assert t == n * c and n % GLA_SEQS_PER_STEP == 0
    masks = jnp.asarray(_gla_masks(c), F32)
    tb = GLA_SEQS_PER_STEP * c
    row = lambda w: pl.BlockSpec((tb, w), lambda i: (i, 0))
    s_spec = pl.BlockSpec((GLA_SEQS_PER_STEP,) + s0.shape[1:], lambda i: (i, 0, 0, 0))
    return pl.pallas_call(
        functools.partial(_gla_seqs_kernel, c=c, n_seqs=GLA_SEQS_PER_STEP),
        grid=(n // GLA_SEQS_PER_STEP,),
        in_specs=[row(DK_B), row(DK_B), row(DV_B), row(DK_B), _const_spec(masks.shape), s_spec],
        out_specs=(row(DV_B), s_spec),
        out_shape=(jax.ShapeDtypeStruct((t, DV_B), BF16), jax.ShapeDtypeStruct(s0.shape, F32)),
        compiler_params=_params(),
        name="gla_seqs",
    )(qb, kb, vb, la, masks, s0)


def _merge_ffn_kernel(x_ref, oa_ref, ob_ref, rs_ref, ga_ref, gb_ref, gn_ref, gpost_ref, gpre_ref, gfpost_ref,
                      wpa_ref, wpb_ref, wout_ref, wg_ref, wu_ref, wd_ref, y_ref, x1_ref, h_ref, act_ref):
    group_rows = TM // MERGE_ROW_GROUPS
    groups = [slice(r * group_rows, (r + 1) * group_rows) for r in range(MERGE_ROW_GROUPS)]
    for rows in groups:
        pa = _dot(oa_ref[rows, :], wpa_ref[...])
        ob = jnp.concatenate(
            [_rms(ob_ref[rows, h * DV_HEAD_B:(h + 1) * DV_HEAD_B].astype(F32)) * gn_ref[...]
             for h in range(N_HEADS_B)], axis=1) * rs_ref[rows, :].astype(F32)
        pb = _dot(ob.astype(BF16), wpb_ref[...])
        mix = jax.nn.sigmoid(ga_ref[rows, :].astype(F32)) * pa + jax.nn.sigmoid(gb_ref[rows, :].astype(F32)) * pb
        x1 = x_ref[rows, :] + _rms(_dot(mix.astype(BF16), wout_ref[...])) * gpost_ref[...]
        x1_ref[rows, :] = x1
        h_ref[rows, :] = (_rms(x1) * gpre_ref[...]).astype(BF16)

    for j in range(N_FF_CHUNKS):
        cols = slice(j * FF_CHUNK, (j + 1) * FF_CHUNK)
        for rows in (groups if j == 0 else [slice(0, TM)]):
            h = h_ref[rows, :]
            g = _dot(h, wg_ref[:, cols])
            u = _dot(h, wu_ref[:, cols])
            act_ref[rows, cols] = (g * jax.nn.sigmoid(g) * u).astype(BF16)
    y_ref[...] = x1_ref[...] + _rms(_dot(act_ref[...], wd_ref[...])) * gfpost_ref[...]


def _merge_ffn(x, oa, ob, rs, ga, gb, gnorm, gpost, gpre, gfpost, wpa, wpb, wout, wg, wu, wd):
    t = x.shape[0]
    n = t // TM
    row = lambda w: pl.BlockSpec((TM, w), lambda i: (i, 0))
    vec = _const_spec((1, D_MODEL))
    return pl.pallas_call(
        _merge_ffn_kernel,
        grid=(n,),
        in_specs=[row(D_MODEL), row(D_A), row(DV_B), row(DV_B), row(D_MODEL), row(D_MODEL),
                  _const_spec((1, DV_HEAD_B)), vec, vec, vec,
                  _const_spec(wpa.shape), _const_spec(wpb.shape), _const_spec(wout.shape),
                  _const_spec(wg.shape), _const_spec(wu.shape), _const_spec(wd.shape)],
        out_specs=row(D_MODEL),
        out_shape=jax.ShapeDtypeStruct((t, D_MODEL), F32),
        scratch_shapes=[pltpu.VMEM((TM, D_MODEL), F32), pltpu.VMEM((TM, D_MODEL), BF16),
                        pltpu.VMEM((TM, D_FF), BF16)],
        compiler_params=_params(),
        name="merge_ffn",
    )(x, oa, ob, rs, ga, gb, gnorm, gpost, gpre, gfpost, wpa, wpb, wout, wg, wu, wd)


def _rel_bias_row(table):
    n_hi = BAND_PAST + BIAS_ORIGIN - REL_CLIP
    n_lo = BIAS_F_LEN - n_hi - (2 * REL_CLIP + 1)
    h = table.shape[0]
    return jnp.concatenate([jnp.broadcast_to(table[:, -1:], (h, n_hi)), table[:, ::-1],
                            jnp.broadcast_to(table[:, :1], (h, n_lo))], axis=1)


def kernel(x_prompt, x_sample, cache_attn_k, cache_attn_v, state_gla, norm_mix_pre, norm_mix_post, norm_ffn_pre, norm_ffn_post, w_in, w_decay_up, b_decay, rel_bias, gla_norm, w_proj_a, w_proj_b, w_out, w_ffn_gate, w_ffn_up, w_ffn_down):
    depth = w_in.shape[0]
    assert depth == 1, "single-layer step"
    batch, seq, _ = x_prompt.shape
    dec_batch, dec_seq, _ = x_sample.shape
    assert batch == 1 and seq % TM == 0 and (dec_batch * dec_seq) % TM == 0
    assert seq % (ATT_SUB * ATT_QB) == 0
    past = cache_attn_k.shape[2]

    wt_bf = jnp.transpose(w_in[0]).astype(BF16)
    wup = w_decay_up[0].astype(BF16)
    bdec = b_decay[0][None, :]
    vec = lambda a: a[0][None, :]
    wpa, wpb, wout = w_proj_a[0].astype(BF16), w_proj_b[0].astype(BF16), w_out[0].astype(BF16)
    wg, wu, wd = w_ffn_gate[0].astype(BF16), w_ffn_up[0].astype(BF16), w_ffn_down[0].astype(BF16)
    gnorm = gla_norm[0][None, :]

    def layer_tail(x, oa, ob, rs, ga, gb):
        return _merge_ffn(x, oa, ob, rs, ga, gb, gnorm, vec(norm_mix_post), vec(norm_ffn_pre), vec(norm_ffn_post),
                          wpa, wpb, wout, wg, wu, wd)

    xp = x_prompt[0]
    qa, ka, va, qb, kb, vb, rs, la, ga, gb, kf, vf = _in_proj(
        xp, vec(norm_mix_pre), wt_bf, wup, bdec, kv_rows_every_step=False)
    f_tab = _rel_bias_row(rel_bias[0])
    oa = _attn_prompt(qa, ka, va, f_tab)
    s0 = jnp.zeros((N_HEADS_B, DK_HEAD_B, DV_HEAD_B), F32)
    ob, sp = _gla_stream(qb, kb, vb, la, s0, c=GLA_CHUNK, tb=GLA_TB)
    yp = layer_tail(xp, oa, ob, rs, ga, gb)
    keep = min(BAND_PAST, seq)
    assert keep == TM

    xs = x_sample.reshape(dec_batch * dec_seq, D_MODEL)
    qa, ka, va, qb, kb, vb, rs, la, ga, gb, kfs, vfs = _in_proj(
        xs, vec(norm_mix_pre), wt_bf, wup, bdec, kv_rows_every_step=True)
    assert past == BAND_PAST
    oa = _attn_sample(qa, kfs, vfs, jnp.transpose(cache_attn_k[0], (0, 2, 3, 1)),
                      jnp.transpose(cache_attn_v[0], (0, 2, 3, 1)), f_tab, seq=dec_seq)
    gla_chunk = CHUNK if dec_seq % CHUNK == 0 else dec_seq
    assert gla_chunk == dec_seq
    ob, ss = _gla_seqs(qb, kb, vb, la, state_gla[0], c=gla_chunk)
    ys = layer_tail(xs, oa, ob, rs, ga, gb)

    hd = (N_HEADS_A, HEAD_DIM_A)
    return (yp[None], ys.reshape(dec_batch, dec_seq, D_MODEL),
            kf.reshape((1, 1, keep) + hd), vf.reshape((1, 1, keep) + hd), sp[None, None],
            kfs.reshape((1, dec_batch, dec_seq) + hd), vfs.reshape((1, dec_batch, dec_seq) + hd), ss[None])
```

```python
import functools

import numpy as np
import jax
import jax.numpy as jnp
from jax import lax
from jax.experimental import pallas as pl
from jax.experimental.pallas import tpu as pltpu

F32 = jnp.float32
BF16 = jnp.bfloat16

D_MODEL = 1024
CHUNK = 64
BAND_CHUNKS = 8
BAND_PAST = BAND_CHUNKS * CHUNK
N_HEADS_A = 8
HEAD_DIM_A = 64
D_A = N_HEADS_A * HEAD_DIM_A
REL_CLIP = 128
N_HEADS_B = 4
DK_HEAD_B = 128
DV_HEAD_B = 256
DK_B = N_HEADS_B * DK_HEAD_B
DV_B = N_HEADS_B * DV_HEAD_B
GATE_RANK = 16
GATE_TEMP = 16.0
D_FF = 2816
EPS = 1e-6

LANES = 128
SUBLANES = 8
BF16_ROWS = 16
LOG2E = 1.4426950408889634
VMEM_LIMIT_BYTES = 56 * 1024 * 1024
NEG_BIG = -1e30

_PIECES = (("qa", D_A), ("ka", D_A), ("va", D_A), ("qb", DK_B), ("kb", DK_B), ("vb", DV_B),
           ("rb", DV_B), ("dlr", GATE_RANK), ("ga", D_MODEL), ("gb", D_MODEL))
_OFF = {}
_o = 0
for _n, _w in _PIECES:
    _OFF[_n] = (_o, _o + _w)
    _o += _w
D_IN = _o
assert all(lo % BF16_ROWS == 0 for lo, _ in _OFF.values())
assert DK_B == 4 * LANES

TM = 512
ATT_QB = 256
ATT_WIN = ATT_QB + BAND_PAST
ATT_SUB = 4
ATT_MASKED = BAND_PAST // ATT_QB
assert ATT_MASKED <= ATT_SUB
BIAS_ORIGIN = ATT_QB
BIAS_F_LEN = 1024
GLA_TB = 1024
GLA_CHUNK = 128
PIPELINE_DEPTH = 2
GLA_SEQS_PER_STEP = 8
ATTS_SEQS_PER_STEP = 4
FF_CHUNK = 256
N_FF_CHUNKS = D_FF // FF_CHUNK
MERGE_ROW_GROUPS = 2


def _const_spec(shape):
    nd = len(shape)
    return pl.BlockSpec(shape, lambda i: (0,) * nd, pipeline_mode=pl.Buffered(1))


def _params():
    return pltpu.CompilerParams(dimension_semantics=("arbitrary",), vmem_limit_bytes=VMEM_LIMIT_BYTES)


def _two_phase_pipeline(units, first, second):
    pending = []
    for u in units:
        pending.append((u, first(*u)))
        if len(pending) > PIPELINE_DEPTH:
            u_done, carried = pending.pop(0)
            second(*u_done, *carried)
    for u_done, carried in pending:
        second(*u_done, *carried)


def _rms(x):
    return x * lax.rsqrt(jnp.mean(x * x, axis=-1, keepdims=True) + EPS)


def _dot(a, b):
    return jnp.dot(a, b, preferred_element_type=F32)


def _dot_nt(a, b):
    return lax.dot_general(a, b, (((1,), (1,)), ((), ())), preferred_element_type=F32)


def _dot_tn(a, b):
    return lax.dot_general(a, b, (((0,), (0,)), ((), ())), preferred_element_type=F32)


def _in_proj_kernel(x_ref, g_ref, wt_ref, wup_ref, bdec_ref,
                    qa_ref, ka_ref, va_ref, qb_ref, kb_ref, vb_ref, rs_ref, la_ref, ga_ref, gb_ref,
                    kf_ref, vf_ref):
    h = (_rms(x_ref[...]) * g_ref[...]).astype(BF16)

    def proj(name):
        lo, hi = _OFF[name]
        return _dot_nt(h, wt_ref[lo:hi, :])

    dlr = proj("dlr").astype(BF16)
    qa_ref[...] = (proj("qa") * (HEAD_DIM_A ** -0.5 * LOG2E)).astype(BF16)
    ka = proj("ka")
    va = proj("va")
    ka_ref[...] = ka.astype(BF16)
    va_ref[...] = va.astype(BF16)
    kf_ref[...] = ka
    vf_ref[...] = va

    def log_decay(part):
        cols = slice(part * LANES, (part + 1) * LANES)
        z = (_dot(dlr, wup_ref[:, cols]) + bdec_ref[:, cols]) * LOG2E
        la_ref[:, cols] = (jnp.minimum(z, 0.0) - jnp.log2(1.0 + jnp.exp2(-jnp.abs(z)))) * (1.0 / GATE_TEMP)

    log_decay(0)
    qb_ref[...] = (proj("qb") * (DK_HEAD_B ** -0.5)).astype(BF16)
    log_decay(1)
    kb_ref[...] = proj("kb").astype(BF16)
    log_decay(2)
    vb_ref[...] = proj("vb").astype(BF16)
    log_decay(3)
    half_r = 0.5 * proj("rb")
    rs_ref[...] = (half_r + half_r * jnp.tanh(half_r)).astype(BF16)
    ga_ref[...] = proj("ga").astype(BF16)
    gb_ref[...] = proj("gb").astype(BF16)


def _in_proj(x, g, wt_bf, wup, bdec, *, kv_rows_every_step):
    t = x.shape[0]
    n = t // TM
    row = lambda w: pl.BlockSpec((TM, w), lambda i: (i, 0))
    if kv_rows_every_step:
        kv_rows, kv_spec = t, row(D_A)
    else:
        kv_rows, kv_spec = TM, pl.BlockSpec((TM, D_A), lambda i: (0, 0))
    widths = (D_A, D_A, D_A, DK_B, DK_B, DV_B, DV_B, DK_B, D_MODEL, D_MODEL)
    dtypes = (BF16,) * 7 + (F32, BF16, BF16)
    out_shape = tuple(jax.ShapeDtypeStruct((t, w), d) for w, d in zip(widths, dtypes))
    out_shape += (jax.ShapeDtypeStruct((kv_rows, D_A), F32),) * 2
    out_specs = tuple(row(w) for w in widths) + (kv_spec, kv_spec)
    return pl.pallas_call(
        _in_proj_kernel,
        grid=(n,),
        in_specs=[row(D_MODEL), _const_spec((1, D_MODEL)), _const_spec((D_IN, D_MODEL)),
                  _const_spec((GATE_RANK, DK_B)), _const_spec((1, DK_B))],
        out_specs=out_specs,
        out_shape=out_shape,
        compiler_params=_params(),
        name="in_proj",
    )(x, g, wt_bf, wup, bdec)


def _softmax_pv(s, v):
    m = jnp.max(s, axis=-1, keepdims=True)
    e = jnp.exp2(s - m).astype(BF16)
    pv = _dot(e, jnp.concatenate([v, jnp.ones_like(v)], axis=1))
    return pv[:, :LANES] / pv[:, LANES:LANES + 1]


def _head_pair_attention(q_pair, k_pair, v_pair, bias0, bias1):
    nq = q_pair.shape[0]
    lane = lax.broadcasted_iota(jnp.int32, (1, LANES), 1)
    first = lane < HEAD_DIM_A
    zero = jnp.zeros_like(q_pair)
    qs = jnp.concatenate([jnp.where(first, q_pair, zero), jnp.where(first, zero, q_pair)], axis=0)
    s = _dot_nt(qs, k_pair) + jnp.concatenate([bias0, bias1], axis=0)
    pv = _softmax_pv(s, v_pair)
    return jnp.where(first, pv[:nq], pv[nq:])


def _expand_rel_bias(f_ref, bias_ref, n_q, n_k, first_valid, n_valid, band):
    t = lax.broadcasted_iota(jnp.int32, (n_q, n_k), 0)
    k = lax.broadcasted_iota(jnp.int32, (n_q, n_k), 1)
    keep = (k < n_valid) & (k >= first_valid)
    if band:
        jq, jk = t // CHUNK, k // CHUNK
        keep = keep & (jk >= jq) & (jk <= jq + BAND_CHUNKS)
    for h in range(N_HEADS_A):
        rows = jnp.broadcast_to(f_ref[h:h + 1, :], (n_q, BIAS_F_LEN))
        rolled = pltpu.roll(rows, BIAS_F_LEN - BIAS_ORIGIN, 1, stride=1, stride_axis=0)
        bias_ref[h] = jnp.where(keep, rolled[:, :n_k] * LOG2E, NEG_BIG)


def _attn_prompt_kernel(q_ref, *refs):
    n_kb = ATT_SUB + ATT_WIN // ATT_QB - 1
    k_refs, v_refs = refs[:n_kb], refs[n_kb:2 * n_kb]
    f_ref, o_ref, bias_ref = refs[2 * n_kb:]
    i = pl.program_id(0)

    @pl.when(i == 0)
    def _():
        _expand_rel_bias(f_ref, bias_ref.at[0], ATT_QB, ATT_WIN, 0, ATT_WIN, band=True)
        for g in range(ATT_MASKED):
            _expand_rel_bias(f_ref, bias_ref.at[1 + g], ATT_QB, ATT_WIN, BAND_PAST - ATT_QB * g, ATT_WIN,
                             band=True)

    for sub in range(ATT_SUB):
        kwin = jnp.concatenate([r[...] for r in k_refs[sub:sub + ATT_WIN // ATT_QB]], axis=0)
        vwin = jnp.concatenate([r[...] for r in v_refs[sub:sub + ATT_WIN // ATT_QB]], axis=0)
        rows = slice(sub * ATT_QB, (sub + 1) * ATT_QB)
        which = jnp.where(i == 0, 1 + sub, 0) if sub < ATT_MASKED else 0
        for p in range(N_HEADS_A // 2):
            sl = slice(p * LANES, (p + 1) * LANES)
            o = _head_pair_attention(q_ref[rows, sl], kwin[:, sl], vwin[:, sl],
                                     bias_ref[which, 2 * p], bias_ref[which, 2 * p + 1])
            o_ref[rows, sl] = o.astype(BF16)


def _attn_prompt(qa, ka, va, f_tab):
    t = qa.shape[0]
    n = t // (ATT_SUB * ATT_QB)
    n_kb = ATT_SUB + ATT_WIN // ATT_QB - 1
    kblk = [pl.BlockSpec((ATT_QB, D_A), functools.partial(
        lambda j, i: (jnp.maximum(ATT_SUB * i + j - (n_kb - ATT_SUB), 0), 0), j)) for j in range(n_kb)]
    qblk = pl.BlockSpec((ATT_SUB * ATT_QB, D_A), lambda i: (i, 0))
    return pl.pallas_call(
        _attn_prompt_kernel,
        grid=(n,),
        in_specs=[qblk] + kblk + kblk + [_const_spec((N_HEADS_A, BIAS_F_LEN))],
        out_specs=qblk,
        out_shape=jax.ShapeDtypeStruct((t, D_A), BF16),
        scratch_shapes=[pltpu.VMEM((1 + ATT_MASKED, N_HEADS_A, ATT_QB, ATT_WIN), F32)],
        compiler_params=_params(),
        name="attn_prompt",
    )(qa, *([ka] * n_kb), *([va] * n_kb), f_tab)


def _attn_sample_kernel(q_ref, kt_ref, vt_ref, kn_ref, vn_ref, f_ref, o_ref, bias_ref, *, n_seqs):
    n_q, w = q_ref.shape[0] // n_seqs, kt_ref.shape[-1]

    @pl.when(pl.program_id(0) == 0)
    def _():
        _expand_rel_bias(f_ref, bias_ref, n_q, bias_ref.shape[-1], 0, w + n_q, band=False)

    lane = lax.broadcasted_iota(jnp.int32, (1, LANES), 1)
    first = lane < HEAD_DIM_A
    rows = lambda j: slice(j * n_q, (j + 1) * n_q)
    lanes = lambda p: slice(p * LANES, (p + 1) * LANES)

    def scores(j, p):
        q_pair = q_ref[rows(j), lanes(p)]
        zero = jnp.zeros_like(q_pair)
        qs = jnp.concatenate([jnp.where(first, q_pair, zero), jnp.where(first, zero, q_pair)], axis=0)
        kt = jnp.concatenate([kt_ref[j, 2 * p], kt_ref[j, 2 * p + 1]], axis=0).astype(BF16)
        kn = kn_ref[rows(j), lanes(p)].astype(BF16)
        bias = jnp.concatenate([bias_ref[2 * p], bias_ref[2 * p + 1]], axis=0)
        return _dot(qs, kt) + bias[:, :w], _dot_nt(qs, kn) + bias[:, w:w + n_q]

    def softmax_pv(j, p, s_c, s_n):
        vt = jnp.concatenate([vt_ref[j, 2 * p], vt_ref[j, 2 * p + 1]], axis=0).astype(BF16)
        vn = vn_ref[rows(j), lanes(p)].astype(BF16)
        m = jnp.maximum(jnp.max(s_c, axis=-1, keepdims=True), jnp.max(s_n, axis=-1, keepdims=True))
        e_c = jnp.exp2(s_c - m).astype(BF16)
        e_n = jnp.exp2(s_n - m).astype(BF16)
        pv = (_dot_nt(e_c, jnp.concatenate([vt, jnp.ones_like(vt)], axis=0))
              + _dot(e_n, jnp.concatenate([vn, jnp.ones_like(vn)], axis=1)))
        pv = pv[:, :LANES] / pv[:, LANES:LANES + 1]
        o_ref[rows(j), lanes(p)] = jnp.where(first, pv[:n_q], pv[n_q:]).astype(BF16)

    _two_phase_pipeline([(j, p) for j in range(n_seqs) for p in range(N_HEADS_A // 2)], scores, softmax_pv)


def _attn_sample(qa, kn, vn, cache_kt, cache_vt, f_tab, *, seq):
    t = qa.shape[0]
    nb, _, _, w = cache_kt.shape
    n_k = w + seq
    n_seqs = ATTS_SEQS_PER_STEP
    assert nb % n_seqs == 0
    row = pl.BlockSpec((n_seqs * seq, D_A), lambda i: (i, 0))
    cache = pl.BlockSpec((n_seqs, N_HEADS_A, HEAD_DIM_A, w), lambda i: (i, 0, 0, 0))
    return pl.pallas_call(
        functools.partial(_attn_sample_kernel, n_seqs=n_seqs),
        grid=(nb // n_seqs,),
        in_specs=[row, cache, cache, row, row, _const_spec((N_HEADS_A, BIAS_F_LEN))],
        out_specs=row,
        out_shape=jax.ShapeDtypeStruct((t, D_A), BF16),
        scratch_shapes=[pltpu.VMEM((N_HEADS_A, seq, n_k + (-n_k) % LANES), F32)],
        compiler_params=_params(),
        name="attn_sample",
    )(qa, cache_kt, cache_vt, kn, vn, f_tab)


def _gla_levels(c):
    out, hs = [], 1
    while hs < c:
        out.append(hs)
        hs *= 2
    return out


def _gla_masks(c):
    t = np.arange(c)
    masks = []
    for hs in _gla_levels(c):
        blk = t // (2 * hs)
        second = (t % (2 * hs)) >= hs
        masks.append((blk[:, None] == blk[None, :]) & second[:, None] & ~second[None, :])
    masks.append(np.eye(c, dtype=bool))
    return np.stack(masks).astype(np.float32)


def _split_row_bcast(x, hs, row):
    c, w = x.shape
    blk = 2 * hs
    if blk >= SUBLANES:
        xr = x.reshape(c // blk, blk, w)
        return jnp.broadcast_to(xr[:, hs - 1:hs, :], (c // blk, blk, w)).reshape(c, w)
    tiles = (c // SUBLANES, SUBLANES, w)
    x3, pos = x.reshape(tiles), (row & (blk - 1)).reshape(tiles)
    if hs == 1:
        out = jnp.where(pos == 1, pltpu.roll(x3, 1, 1), x3)
    else:
        assert hs == 2
        nxt = jnp.where((pos & 1) == 1, x3, pltpu.roll(x3, SUBLANES - 1, 1))
        out = jnp.where(pos >= 2, pltpu.roll(nxt, 2, 1), nxt)
    return out.reshape(c, w)


def _level_step(p, qf, kf, hs, row):
    c, w = p.shape
    if hs < SUBLANES:
        tot = _split_row_bcast(p, hs, row)
        second = (row & hs) != 0
        return jnp.where(second, p, tot - p), jnp.where(second, p + tot, p), jnp.where(second, qf, kf)
    xs, ps, qks = [], [], []
    for lo in range(0, c, 2 * hs):
        first, second = p[lo:lo + hs], p[lo + hs:lo + 2 * hs]
        tot = jnp.broadcast_to(first[hs - 1:hs], (hs, w))
        xs += [tot - first, second]
        ps += [first, second + tot]
        qks += [kf[lo:lo + hs], qf[lo + hs:lo + 2 * hs]]
    if hs % BF16_ROWS == 0:
        qk = jnp.concatenate(qks, axis=0)
    else:
        qk = jnp.where((row & hs) != 0, qf, kf)
    return jnp.concatenate(xs, axis=0), jnp.concatenate(ps, axis=0), qk


def _gla_intra(rows, c, h, q_ref, k_ref, la_ref, mask_ref):
    levels = _gla_levels(c)
    n_lvl = len(levels)
    row = lax.broadcasted_iota(jnp.int32, (c, DK_HEAD_B), 0)
    hk = slice(h * DK_HEAD_B, (h + 1) * DK_HEAD_B)
    qf = q_ref[rows, hk]
    kf = k_ref[rows, hk]
    p = la_ref[rows, hk]
    a = mask_ref[n_lvl] * _dot_nt(qf, kf)
    for l, hs in enumerate(levels):
        x, p, qk = _level_step(p, qf, kf, hs, row)
        z = qk * jnp.exp2(x).astype(BF16)
        a = a + mask_ref[l] * _dot_nt(z, z)
    b = p
    b_last = b[c - 1:c, :]
    qd = qf * jnp.exp2(b).astype(BF16)
    kd = kf * jnp.exp2(b_last - b).astype(BF16)
    d_last = jnp.exp2(b_last)
    return a.astype(BF16), qd, kd, d_last


def _gla_inter(rows, h, v_ref, st_in, st_out, o_ref, a, qd, kd, d_last, *, state_t):
    hv = slice(h * DV_HEAD_B, (h + 1) * DV_HEAD_B)
    vh = v_ref[rows, hv]
    o = _dot(a, vh)
    st = st_in[h]
    if state_t:
        o = o + _dot_nt(qd, st.astype(BF16))
        st_out[h] = st * d_last + _dot_tn(vh, kd)
    else:
        o = o + _dot(qd, st.astype(BF16))
        d_col = jnp.broadcast_to(d_last, (SUBLANES, DK_HEAD_B)).T[:, 0:1]
        st_out[h] = st * d_col + _dot_tn(kd, vh)
    o_ref[rows, hv] = o.astype(BF16)


def _gla_stream_kernel(q_ref, k_ref, v_ref, la_ref, mask_ref, s0_ref, o_ref, sout_ref, st_ref, *, c, n_chunks):
    @pl.when(pl.program_id(0) == 0)
    def _():
        for h in range(N_HEADS_B):
            st_ref[h] = s0_ref[h].T

    rows = lambda ci: slice(ci * c, (ci + 1) * c)
    _two_phase_pipeline(
        [(ci, h) for ci in range(n_chunks) for h in range(N_HEADS_B)],
        lambda ci, h: _gla_intra(rows(ci), c, h, q_ref, k_ref, la_ref, mask_ref),
        lambda ci, h, *intra: _gla_inter(rows(ci), h, v_ref, st_ref, st_ref, o_ref, *intra, state_t=True))

    @pl.when(pl.program_id(0) == pl.num_programs(0) - 1)
    def _():
        for h in range(N_HEADS_B):
            sout_ref[h] = st_ref[h].T


def _gla_seqs_kernel(q_ref, k_ref, v_ref, la_ref, mask_ref, s0_ref, o_ref, sout_ref, *, c, n_seqs):
    rows = lambda j: slice(j * c, (j + 1) * c)
    _two_phase_pipeline(
        [(j, h) for j in range(n_seqs) for h in range(N_HEADS_B)],
        lambda j, h: _gla_intra(rows(j), c, h, q_ref, k_ref, la_ref, mask_ref),
        lambda j, h, *intra: _gla_inter(rows(j), h, v_ref, s0_ref.at[j], sout_ref.at[j], o_ref, *intra,
                                        state_t=False))


def _gla_stream(qb, kb, vb, la, s0, *, c, tb):
    t = qb.shape[0]
    masks = jnp.asarray(_gla_masks(c), F32)
    row = lambda w: pl.BlockSpec((tb, w), lambda i: (i, 0))
    st_shape = (N_HEADS_B, DK_HEAD_B, DV_HEAD_B)
    s_spec = pl.BlockSpec(st_shape, lambda i: (0, 0, 0))
    n_chunks = tb // c
    return pl.pallas_call(
        functools.partial(_gla_stream_kernel, c=c, n_chunks=n_chunks),
        grid=(t // tb,),
        in_specs=[row(DK_B), row(DK_B), row(DV_B), row(DK_B), _const_spec(masks.shape), s_spec],
        out_specs=(row(DV_B), s_spec),
        out_shape=(jax.ShapeDtypeStruct((t, DV_B), BF16), jax.ShapeDtypeStruct(st_shape, F32)),
        scratch_shapes=[pltpu.VMEM((N_HEADS_B, DV_HEAD_B, DK_HEAD_B), F32)],
        compiler_params=_params(),
        name="gla_stream",
    )(qb, kb, vb, la, masks, s0)


def _gla_seqs(qb, kb, vb, la, s0, *, c):
    t = qb.shape[0]
    n = s0.shape[0]
    assert t == n * c and n % GLA_SEQS_PER_STEP == 0
    masks = jnp.asarray(_gla_masks(c), F32)
    tb = GLA_SEQS_PER_STEP * c
    row = lambda w: pl.BlockSpec((tb, w), lambda i: (i, 0))
    s_spec = pl.BlockSpec((GLA_SEQS_PER_STEP,) + s0.shape[1:], lambda i: (i, 0, 0, 0))
    return pl.pallas_call(
        functools.partial(_gla_seqs_kernel, c=c, n_seqs=GLA_SEQS_PER_STEP),
        grid=(n // GLA_SEQS_PER_STEP,),
        in_specs=[row(DK_B), row(DK_B), row(DV_B), row(DK_B), _const_spec(masks.shape), s_spec],
        out_specs=(row(DV_B), s_spec),
        out_shape=(jax.ShapeDtypeStruct((t, DV_B), BF16), jax.ShapeDtypeStruct(s0.shape, F32)),
        compiler_params=_params(),
        name="gla_seqs",
    )(qb, kb, vb, la, masks, s0)


def _merge_ffn_kernel(x_ref, oa_ref, ob_ref, rs_ref, ga_ref, gb_ref, gn_ref, gpost_ref, gpre_ref, gfpost_ref,
                      wpa_ref, wpb_ref, wout_ref, wg_ref, wu_ref, wd_ref, y_ref, x1_ref, h_ref, act_ref):
    group_rows = TM // MERGE_ROW_GROUPS
    groups = [slice(r * group_rows, (r + 1) * group_rows) for r in range(MERGE_ROW_GROUPS)]
    for rows in groups:
        pa = _dot(oa_ref[rows, :], wpa_ref[...])
        ob = jnp.concatenate(
            [_rms(ob_ref[rows, h * DV_HEAD_B:(h + 1) * DV_HEAD_B].astype(F32)) * gn_ref[...]
             for h in range(N_HEADS_B)], axis=1) * rs_ref[rows, :].astype(F32)
        pb = _dot(ob.astype(BF16), wpb_ref[...])
        mix = jax.nn.sigmoid(ga_ref[rows, :].astype(F32)) * pa + jax.nn.sigmoid(gb_ref[rows, :].astype(F32)) * pb
        x1 = x_ref[rows, :] + _rms(_dot(mix.astype(BF16), wout_ref[...])) * gpost_ref[...]
        x1_ref[rows, :] = x1
        h_ref[rows, :] = (_rms(x1) * gpre_ref[...]).astype(BF16)

    for j in range(N_FF_CHUNKS):
        cols = slice(j * FF_CHUNK, (j + 1) * FF_CHUNK)
        for rows in (groups if j == 0 else [slice(0, TM)]):
            h = h_ref[rows, :]
            g = _dot(h, wg_ref[:, cols])
            u = _dot(h, wu_ref[:, cols])
            act_ref[rows, cols] = (g * jax.nn.sigmoid(g) * u).astype(BF16)
    y_ref[...] = x1_ref[...] + _rms(_dot(act_ref[...], wd_ref[...])) * gfpost_ref[...]


def _merge_ffn(x, oa, ob, rs, ga, gb, gnorm, gpost, gpre, gfpost, wpa, wpb, wout, wg, wu, wd):
    t = x.shape[0]
    n = t // TM
    row = lambda w: pl.BlockSpec((TM, w), lambda i: (i, 0))
    vec = _const_spec((1, D_MODEL))
    return pl.pallas_call(
        _merge_ffn_kernel,
        grid=(n,),
        in_specs=[row(D_MODEL), row(D_A), row(DV_B), row(DV_B), row(D_MODEL), row(D_MODEL),
                  _const_spec((1, DV_HEAD_B)), vec, vec, vec,
                  _const_spec(wpa.shape), _const_spec(wpb.shape), _const_spec(wout.shape),
                  _const_spec(wg.shape), _const_spec(wu.shape), _const_spec(wd.shape)],
        out_specs=row(D_MODEL),
        out_shape=jax.ShapeDtypeStruct((t, D_MODEL), F32),
        scratch_shapes=[pltpu.VMEM((TM, D_MODEL), F32), pltpu.VMEM((TM, D_MODEL), BF16),
                        pltpu.VMEM((TM, D_FF), BF16)],
        compiler_params=_params(),
        name="merge_ffn",
    )(x, oa, ob, rs, ga, gb, gnorm, gpost, gpre, gfpost, wpa, wpb, wout, wg, wu, wd)


def _rel_bias_row(table):
    n_hi = BAND_PAST + BIAS_ORIGIN - REL_CLIP
    n_lo = BIAS_F_LEN - n_hi - (2 * REL_CLIP + 1)
    h = table.shape[0]
    return jnp.concatenate([jnp.broadcast_to(table[:, -1:], (h, n_hi)), table[:, ::-1],
                            jnp.broadcast_to(table[:, :1], (h, n_lo))], axis=1)


def kernel(x_prompt, x_sample, cache_attn_k, cache_attn_v, state_gla, norm_mix_pre, norm_mix_post, norm_ffn_pre, norm_ffn_post, w_in, w_decay_up, b_decay, rel_bias, gla_norm, w_proj_a, w_proj_b, w_out, w_ffn_gate, w_ffn_up, w_ffn_down):
    depth = w_in.shape[0]
    assert depth == 1, "single-layer step"
    batch, seq, _ = x_prompt.shape
    dec_batch, dec_seq, _ = x_sample.shape
    assert batch == 1 and seq % TM == 0 and (dec_batch * dec_seq) % TM == 0
    assert seq % (ATT_SUB * ATT_QB) == 0
    past = cache_attn_k.shape[2]

    wt_bf = jnp.transpose(w_in[0]).astype(BF16)
    wup = w_decay_up[0].astype(BF16)
    bdec = b_decay[0][None, :]
    vec = lambda a: a[0][None, :]
    wpa, wpb, wout = w_proj_a[0].astype(BF16), w_proj_b[0].astype(BF16), w_out[0].astype(BF16)
    wg, wu, wd = w_ffn_gate[0].astype(BF16), w_ffn_up[0].astype(BF16), w_ffn_down[0].astype(BF16)
    gnorm = gla_norm[0][None, :]

    def layer_tail(x, oa, ob, rs, ga, gb):
        return _merge_ffn(x, oa, ob, rs, ga, gb, gnorm, vec(norm_mix_post), vec(norm_ffn_pre), vec(norm_ffn_post),
                          wpa, wpb, wout, wg, wu, wd)

    xp = x_prompt[0]
    qa, ka, va, qb, kb, vb, rs, la, ga, gb, kf, vf = _in_proj(
        xp, vec(norm_mix_pre), wt_bf, wup, bdec, kv_rows_every_step=False)
    f_tab = _rel_bias_row(rel_bias[0])
    oa = _attn_prompt(qa, ka, va, f_tab)
    s0 = jnp.zeros((N_HEADS_B, DK_HEAD_B, DV_HEAD_B), F32)
    ob, sp = _gla_stream(qb, kb, vb, la, s0, c=GLA_CHUNK, tb=GLA_TB)
    yp = layer_tail(xp, oa, ob, rs, ga, gb)
    keep = min(BAND_PAST, seq)
    assert keep == TM

    xs = x_sample.reshape(dec_batch * dec_seq, D_MODEL)
    qa, ka, va, qb, kb, vb, rs, la, ga, gb, kfs, vfs = _in_proj(
        xs, vec(norm_mix_pre), wt_bf, wup, bdec, kv_rows_every_step=True)
    assert past == BAND_PAST
    oa = _attn_sample(qa, kfs, vfs, jnp.transpose(cache_attn_k[0], (0, 2, 3, 1)),
                      jnp.transpose(cache_attn_v[0], (0, 2, 3, 1)), f_tab, seq=dec_seq)
    gla_chunk = CHUNK if dec_seq % CHUNK == 0 else dec_seq
    assert gla_chunk == dec_seq
    ob, ss = _gla_seqs(qb, kb, vb, la, state_gla[0], c=gla_chunk)
    ys = layer_tail(xs, oa, ob, rs, ga, gb)

    hd = (N_HEADS_A, HEAD_DIM_A)
    return (yp[None], ys.reshape(dec_batch, dec_seq, D_MODEL),
            kf.reshape((1, 1, keep) + hd), vf.reshape((1, 1, keep) + hd), sp[None, None],
            kfs.reshape((1, dec_batch, dec_seq) + hd), vfs.reshape((1, dec_batch, dec_seq) + hd), ss[None])
```

```python
import functools

import numpy as np
import jax
import jax.numpy as jnp
from jax import lax
from jax.experimental import pallas as pl
from jax.experimental.pallas import tpu as pltpu

F32 = jnp.float32
BF16 = jnp.bfloat16

D_MODEL = 1024
CHUNK = 64
BAND_CHUNKS = 8
BAND_PAST = BAND_CHUNKS * CHUNK
N_HEADS_A = 8
HEAD_DIM_A = 64
D_A = N_HEADS_A * HEAD_DIM_A
REL_CLIP = 128
N_HEADS_B = 4
DK_HEAD_B = 128
DV_HEAD_B = 256
DK_B = N_HEADS_B * DK_HEAD_B
DV_B = N_HEADS_B * DV_HEAD_B
GATE_RANK = 16
GATE_TEMP = 16.0
D_FF = 2816
EPS = 1e-6

LANES = 128
SUBLANES = 8
BF16_ROWS = 16
LOG2E = 1.4426950408889634
VMEM_LIMIT_BYTES = 56 * 1024 * 1024
NEG_BIG = -1e30

_PIECES = (("qa", D_A), ("ka", D_A), ("va", D_A), ("qb", DK_B), ("kb", DK_B), ("vb", DV_B),
           ("rb", DV_B), ("dlr", GATE_RANK), ("ga", D_MODEL), ("gb", D_MODEL))
_OFF = {}
_o = 0
for _n, _w in _PIECES:
    _OFF[_n] = (_o, _o + _w)
    _o += _w
D_IN = _o
assert all(lo % BF16_ROWS == 0 for lo, _ in _OFF.values())
assert DK_B == 4 * LANES

TM = 512
ATT_QB = 256
ATT_WIN = ATT_QB + BAND_PAST
ATT_SUB = 4
ATT_MASKED = BAND_PAST // ATT_QB
assert ATT_MASKED <= ATT_SUB
BIAS_ORIGIN = ATT_QB
BIAS_F_LEN = 1024
GLA_TB = 2048
GLA_CHUNK = 128
PIPELINE_DEPTH = 2
GLA_SEQS_PER_STEP = 8
ATTS_SEQS_PER_STEP = 4
FF_CHUNK = 256
N_FF_CHUNKS = D_FF // FF_CHUNK
MERGE_ROW_GROUPS = 2


def _const_spec(shape):
    nd = len(shape)
    return pl.BlockSpec(shape, lambda i: (0,) * nd, pipeline_mode=pl.Buffered(1))


def _params():
    return pltpu.CompilerParams(dimension_semantics=("arbitrary",), vmem_limit_bytes=VMEM_LIMIT_BYTES)


def _two_phase_pipeline(units, first, second):
    pending = []
    for u in units:
        pending.append((u, first(*u)))
        if len(pending) > PIPELINE_DEPTH:
            u_done, carried = pending.pop(0)
            second(*u_done, *carried)
    for u_done, carried in pending:
        second(*u_done, *carried)


def _rms(x):
    return x * lax.rsqrt(jnp.mean(x * x, axis=-1, keepdims=True) + EPS)


def _dot(a, b):
    return jnp.dot(a, b, preferred_element_type=F32)


def _dot_nt(a, b):
    return lax.dot_general(a, b, (((1,), (1,)), ((), ())), preferred_element_type=F32)


def _dot_tn(a, b):
    return lax.dot_general(a, b, (((0,), (0,)), ((), ())), preferred_element_type=F32)


def _in_proj_kernel(x_ref, g_ref, wt_ref, wup_ref, bdec_ref,
                    qa_ref, ka_ref, va_ref, qb_ref, kb_ref, vb_ref, rs_ref, la_ref, ga_ref, gb_ref,
                    kf_ref, vf_ref):
    h = (_rms(x_ref[...]) * g_ref[...]).astype(BF16)

    def proj(name):
        lo, hi = _OFF[name]
        return _dot_nt(h, wt_ref[lo:hi, :])

    dlr = proj("dlr").astype(BF16)
    qa_ref[...] = (proj("qa") * (HEAD_DIM_A ** -0.5 * LOG2E)).astype(BF16)
    ka = proj("ka")
    va = proj("va")
    ka_ref[...] = ka.astype(BF16)
    va_ref[...] = va.astype(BF16)
    kf_ref[...] = ka
    vf_ref[...] = va

    def log_decay(part):
        cols = slice(part * LANES, (part + 1) * LANES)
        z = (_dot(dlr, wup_ref[:, cols]) + bdec_ref[:, cols]) * LOG2E
        la_ref[:, cols] = (jnp.minimum(z, 0.0) - jnp.log2(1.0 + jnp.exp2(-jnp.abs(z)))) * (1.0 / GATE_TEMP)

    log_decay(0)
    qb_ref[...] = (proj("qb") * (DK_HEAD_B ** -0.5)).astype(BF16)
    log_decay(1)
    kb_ref[...] = proj("kb").astype(BF16)
    log_decay(2)
    vb_ref[...] = proj("vb").astype(BF16)
    log_decay(3)
    half_r = 0.5 * proj("rb")
    rs_ref[...] = (half_r + half_r * jnp.tanh(half_r)).astype(BF16)
    ga_ref[...] = proj("ga").astype(BF16)
    gb_ref[...] = proj("gb").astype(BF16)


def _in_proj(x, g, wt_bf, wup, bdec, *, kv_rows_every_step):
    t = x.shape[0]
    n = t // TM
    row = lambda w: pl.BlockSpec((TM, w), lambda i: (i, 0))
    if kv_rows_every_step:
        kv_rows, kv_spec = t, row(D_A)
    else:
        kv_rows, kv_spec = TM, pl.BlockSpec((TM, D_A), lambda i: (0, 0))
    widths = (D_A, D_A, D_A, DK_B, DK_B, DV_B, DV_B, DK_B, D_MODEL, D_MODEL)
    dtypes = (BF16,) * 7 + (F32, BF16, BF16)
    out_shape = tuple(jax.ShapeDtypeStruct((t, w), d) for w, d in zip(widths, dtypes))
    out_shape += (jax.ShapeDtypeStruct((kv_rows, D_A), F32),) * 2
    out_specs = tuple(row(w) for w in widths) + (kv_spec, kv_spec)
    return pl.pallas_call(
        _in_proj_kernel,
        grid=(n,),
        in_specs=[row(D_MODEL), _const_spec((1, D_MODEL)), _const_spec((D_IN, D_MODEL)),
                  _const_spec((GATE_RANK, DK_B)), _const_spec((1, DK_B))],
        out_specs=out_specs,
        out_shape=out_shape,
        compiler_params=_params(),
        name="in_proj",
    )(x, g, wt_bf, wup, bdec)


def _softmax_pv(s, v):
    m = jnp.max(s, axis=-1, keepdims=True)
    e = jnp.exp2(s - m).astype(BF16)
    pv = _dot(e, jnp.concatenate([v, jnp.ones_like(v)], axis=1))
    return pv[:, :LANES] / pv[:, LANES:LANES + 1]


def _expand_rel_bias(f_ref, bias_ref, n_q, n_k, first_valid, n_valid, band):
    t = lax.broadcasted_iota(jnp.int32, (n_q, n_k), 0)
    k = lax.broadcasted_iota(jnp.int32, (n_q, n_k), 1)
    keep = (k < n_valid) & (k >= first_valid)
    if band:
        jq, jk = t // CHUNK, k // CHUNK
        keep = keep & (jk >= jq) & (jk <= jq + BAND_CHUNKS)
    for h in range(N_HEADS_A):
        rows = jnp.broadcast_to(f_ref[h:h + 1, :], (n_q, BIAS_F_LEN))
        rolled = pltpu.roll(rows, BIAS_F_LEN - BIAS_ORIGIN, 1, stride=1, stride_axis=0)
        bias_ref[h] = jnp.where(keep, rolled[:, :n_k] * LOG2E, NEG_BIG)


def _attn_prompt_kernel(q_ref, *refs):
    n_kb = ATT_SUB + ATT_WIN // ATT_QB - 1
    k_refs, v_refs = refs[:n_kb], refs[n_kb:2 * n_kb]
    f_ref, o_ref, bias_ref = refs[2 * n_kb:]
    i = pl.program_id(0)

    @pl.when(i == 0)
    def _():
        _expand_rel_bias(f_ref, bias_ref.at[0], ATT_QB, ATT_WIN, 0, ATT_WIN, band=True)
        for g in range(ATT_MASKED):
            _expand_rel_bias(f_ref, bias_ref.at[1 + g], ATT_QB, ATT_WIN, BAND_PAST - ATT_QB * g, ATT_WIN,
                             band=True)

    rows = lambda sub: slice(sub * ATT_QB, (sub + 1) * ATT_QB)
    lanes = lambda p: slice(p * LANES, (p + 1) * LANES)
    window = lambda refs, sub, p: jnp.concatenate(
        [r[:, lanes(p)] for r in refs[sub:sub + ATT_WIN // ATT_QB]], axis=0)
    first = lax.broadcasted_iota(jnp.int32, (1, LANES), 1) < HEAD_DIM_A

    def scores(sub, p):
        which = jnp.where(i == 0, 1 + sub, 0) if sub < ATT_MASKED else 0
        q_pair = q_ref[rows(sub), lanes(p)]
        zero = jnp.zeros_like(q_pair)
        qs = jnp.concatenate([jnp.where(first, q_pair, zero), jnp.where(first, zero, q_pair)], axis=0)
        bias = jnp.concatenate([bias_ref[which, 2 * p], bias_ref[which, 2 * p + 1]], axis=0)
        return (_dot_nt(qs, window(k_refs, sub, p)) + bias,)

    def softmax_pv(sub, p, s):
        pv = _softmax_pv(s, window(v_refs, sub, p))
        o_ref[rows(sub), lanes(p)] = jnp.where(first, pv[:ATT_QB], pv[ATT_QB:]).astype(BF16)

    _two_phase_pipeline([(sub, p) for sub in range(ATT_SUB) for p in range(N_HEADS_A // 2)], scores, softmax_pv)


def _attn_prompt(qa, ka, va, f_tab):
    t = qa.shape[0]
    n = t // (ATT_SUB * ATT_QB)
    n_kb = ATT_SUB + ATT_WIN // ATT_QB - 1
    kblk = [pl.BlockSpec((ATT_QB, D_A), functools.partial(
        lambda j, i: (jnp.maximum(ATT_SUB * i + j - (n_kb - ATT_SUB), 0), 0), j)) for j in range(n_kb)]
    qblk = pl.BlockSpec((ATT_SUB * ATT_QB, D_A), lambda i: (i, 0))
    return pl.pallas_call(
        _attn_prompt_kernel,
        grid=(n,),
        in_specs=[qblk] + kblk + kblk + [_const_spec((N_HEADS_A, BIAS_F_LEN))],
        out_specs=qblk,
        out_shape=jax.ShapeDtypeStruct((t, D_A), BF16),
        scratch_shapes=[pltpu.VMEM((1 + ATT_MASKED, N_HEADS_A, ATT_QB, ATT_WIN), F32)],
        compiler_params=_params(),
        name="attn_prompt",
    )(qa, *([ka] * n_kb), *([va] * n_kb), f_tab)


def _attn_sample_kernel(q_ref, kt_ref, vt_ref, kn_ref, vn_ref, f_ref, o_ref, bias_ref, *, n_seqs):
    n_q, w = q_ref.shape[0] // n_seqs, kt_ref.shape[-1]

    @pl.when(pl.program_id(0) == 0)
    def _():
        _expand_rel_bias(f_ref, bias_ref, n_q, bias_ref.shape[-1], 0, w + n_q, band=False)

    lane = lax.broadcasted_iota(jnp.int32, (1, LANES), 1)
    first = lane < HEAD_DIM_A
    rows = lambda j: slice(j * n_q, (j + 1) * n_q)
    lanes = lambda p: slice(p * LANES, (p + 1) * LANES)

    def scores(j, p):
        q_pair = q_ref[rows(j), lanes(p)]
        zero = jnp.zeros_like(q_pair)
        qs = jnp.concatenate([jnp.where(first, q_pair, zero), jnp.where(first, zero, q_pair)], axis=0)
        kt = jnp.concatenate([kt_ref[j, 2 * p], kt_ref[j, 2 * p + 1]], axis=0).astype(BF16)
        kn = kn_ref[rows(j), lanes(p)].astype(BF16)
        bias = jnp.concatenate([bias_ref[2 * p], bias_ref[2 * p + 1]], axis=0)
        return _dot(qs, kt) + bias[:, :w], _dot_nt(qs, kn) + bias[:, w:w + n_q]

    def softmax_pv(j, p, s_c, s_n):
        vt = jnp.concatenate([vt_ref[j, 2 * p], vt_ref[j, 2 * p + 1]], axis=0).astype(BF16)
        vn = vn_ref[rows(j), lanes(p)].astype(BF16)
        m = jnp.maximum(jnp.max(s_c, axis=-1, keepdims=True), jnp.max(s_n, axis=-1, keepdims=True))
        e_c = jnp.exp2(s_c - m).astype(BF16)
        e_n = jnp.exp2(s_n - m).astype(BF16)
        pv = (_dot_nt(e_c, jnp.concatenate([vt, jnp.ones_like(vt)], axis=0))
              + _dot(e_n, jnp.concatenate([vn, jnp.ones_like(vn)], axis=1)))
        pv = pv[:, :LANES] / pv[:, LANES:LANES + 1]
        o_ref[rows(j), lanes(p)] = jnp.where(first, pv[:n_q], pv[n_q:]).astype(BF16)

    _two_phase_pipeline([(j, p) for j in range(n_seqs) for p in range(N_HEADS_A // 2)], scores, softmax_pv)


def _attn_sample(qa, kn, vn, cache_kt, cache_vt, f_tab, *, seq):
    t = qa.shape[0]
    nb, _, _, w = cache_kt.shape
    n_k = w + seq
    n_seqs = ATTS_SEQS_PER_STEP
    assert nb % n_seqs == 0
    row = pl.BlockSpec((n_seqs * seq, D_A), lambda i: (i, 0))
    cache = pl.BlockSpec((n_seqs, N_HEADS_A, HEAD_DIM_A, w), lambda i: (i, 0, 0, 0))
    return pl.pallas_call(
        functools.partial(_attn_sample_kernel, n_seqs=n_seqs),
        grid=(nb // n_seqs,),
        in_specs=[row, cache, cache, row, row, _const_spec((N_HEADS_A, BIAS_F_LEN))],
        out_specs=row,
        out_shape=jax.ShapeDtypeStruct((t, D_A), BF16),
        scratch_shapes=[pltpu.VMEM((N_HEADS_A, seq, n_k + (-n_k) % LANES), F32)],
        compiler_params=_params(),
        name="attn_sample",
    )(qa, cache_kt, cache_vt, kn, vn, f_tab)


def _gla_levels(c):
    out, hs = [], 1
    while hs < c:
        out.append(hs)
        hs *= 2
    return out


def _gla_masks(c):
    t = np.arange(c)
    masks = []
    for hs in _gla_levels(c):
        blk = t // (2 * hs)
        second = (t % (2 * hs)) >= hs
        masks.append((blk[:, None] == blk[None, :]) & second[:, None] & ~second[None, :])
    masks.append(np.eye(c, dtype=bool))
    return np.stack(masks).astype(np.float32)


def _split_row_bcast(x, hs, row):
    c, w = x.shape
    blk = 2 * hs
    if blk >= SUBLANES:
        xr = x.reshape(c // blk, blk, w)
        return jnp.broadcast_to(xr[:, hs - 1:hs, :], (c // blk, blk, w)).reshape(c, w)
    tiles = (c // SUBLANES, SUBLANES, w)
    x3, pos = x.reshape(tiles), (row & (blk - 1)).reshape(tiles)
    if hs == 1:
        out = jnp.where(pos == 1, pltpu.roll(x3, 1, 1), x3)
    else:
        assert hs == 2
        nxt = jnp.where((pos & 1) == 1, x3, pltpu.roll(x3, SUBLANES - 1, 1))
        out = jnp.where(pos >= 2, pltpu.roll(nxt, 2, 1), nxt)
    return out.reshape(c, w)


def _level_step(p, qf, kf, hs, row):
    c, w = p.shape
    if hs < SUBLANES:
        tot = _split_row_bcast(p, hs, row)
        second = (row & hs) != 0
        return jnp.where(second, p, tot - p), jnp.where(second, p + tot, p), jnp.where(second, qf, kf)
    xs, ps, qks = [], [], []
    for lo in range(0, c, 2 * hs):
        first, second = p[lo:lo + hs], p[lo + hs:lo + 2 * hs]
        tot = jnp.broadcast_to(first[hs - 1:hs], (hs, w))
        xs += [tot - first, second]
        ps += [first, second + tot]
        qks += [kf[lo:lo + hs], qf[lo + hs:lo + 2 * hs]]
    if hs % BF16_ROWS == 0:
        qk = jnp.concatenate(qks, axis=0)
    else:
        qk = jnp.where((row & hs) != 0, qf, kf)
    return jnp.concatenate(xs, axis=0), jnp.concatenate(ps, axis=0), qk


def _gla_intra(rows, c, h, q_ref, k_ref, la_ref, mask_ref):
    levels = _gla_levels(c)
    n_lvl = len(levels)
    row = lax.broadcasted_iota(jnp.int32, (c, DK_HEAD_B), 0)
    hk = slice(h * DK_HEAD_B, (h + 1) * DK_HEAD_B)
    qf = q_ref[rows, hk]
    kf = k_ref[rows, hk]
    p = la_ref[rows, hk]
    a = mask_ref[n_lvl] * _dot_nt(qf, kf)
    for l, hs in enumerate(levels):
        x, p, qk = _level_step(p, qf, kf, hs, row)
        z = qk * jnp.exp2(x).astype(BF16)
        a = a + mask_ref[l] * _dot_nt(z, z)
    b = p
    b_last = b[c - 1:c, :]
    qd = qf * jnp.exp2(b).astype(BF16)
    kd = kf * jnp.exp2(b_last - b).astype(BF16)
    d_last = jnp.exp2(b_last)
    return a.astype(BF16), qd, kd, d_last


def _gla_inter(rows, h, v_ref, st_in, st_out, o_ref, a, qd, kd, d_last, *, state_t):
    hv = slice(h * DV_HEAD_B, (h + 1) * DV_HEAD_B)
    vh = v_ref[rows, hv]
    o = _dot(a, vh)
    st = st_in[h]
    if state_t:
        o = o + _dot_nt(qd, st.astype(BF16))
        st_out[h] = st * d_last + _dot_tn(vh, kd)
    else:
        o = o + _dot(qd, st.astype(BF16))
        d_col = jnp.broadcast_to(d_last, (SUBLANES, DK_HEAD_B)).T[:, 0:1]
        st_out[h] = st * d_col + _dot_tn(kd, vh)
    o_ref[rows, hv] = o.astype(BF16)


def _gla_stream_kernel(q_ref, k_ref, v_ref, la_ref, mask_ref, s0_ref, o_ref, sout_ref, st_ref, *, c, n_chunks):
    @pl.when(pl.program_id(0) == 0)
    def _():
        for h in range(N_HEADS_B):
            st_ref[h] = s0_ref[h].T

    rows = lambda ci: slice(ci * c, (ci + 1) * c)
    _two_phase_pipeline(
        [(ci, h) for ci in range(n_chunks) for h in range(N_HEADS_B)],
        lambda ci, h: _gla_intra(rows(ci), c, h, q_ref, k_ref, la_ref, mask_ref),
        lambda ci, h, *intra: _gla_inter(rows(ci), h, v_ref, st_ref, st_ref, o_ref, *intra, state_t=True))

    @pl.when(pl.program_id(0) == pl.num_programs(0) - 1)
    def _():
        for h in range(N_HEADS_B):
            sout_ref[h] = st_ref[h].T


def _gla_seqs_kernel(q_ref, k_ref, v_ref, la_ref, mask_ref, s0_ref, o_ref, sout_ref, *, c, n_seqs):
    rows = lambda j: slice(j * c, (j + 1) * c)
    _two_phase_pipeline(
        [(j, h) for j in range(n_seqs) for h in range(N_HEADS_B)],
        lambda j, h: _gla_intra(rows(j), c, h, q_ref, k_ref, la_ref, mask_ref),
        lambda j, h, *intra: _gla_inter(rows(j), h, v_ref, s0_ref.at[j], sout_ref.at[j], o_ref, *intra,
                                        state_t=False))


def _gla_stream(qb, kb, vb, la, s0, *, c, tb):
    t = qb.shape[0]
    assert t % tb == 0 and tb % c == 0
    masks = jnp.asarray(_gla_masks(c), F32)
    row = lambda w: pl.BlockSpec((tb, w), lambda i: (i, 0))
    st_shape = (N_HEADS_B, DK_HEAD_B, DV_HEAD_B)
    s_spec = pl.BlockSpec(st_shape, lambda i: (0, 0, 0))
    n_chunks = tb // c
    return pl.pallas_call(
        functools.partial(_gla_stream_kernel, c=c, n_chunks=n_chunks),
        grid=(t // tb,),
        in_specs=[row(DK_B), row(DK_B), row(DV_B), row(DK_B), _const_spec(masks.shape), s_spec],
        out_specs=(row(DV_B), s_spec),
        out_shape=(jax.ShapeDtypeStruct((t, DV_B), BF16), jax.ShapeDtypeStruct(st_shape, F32)),
        scratch_shapes=[pltpu.VMEM((N_HEADS_B, DV_HEAD_B, DK_HEAD_B), F32)],
        compiler_params=_params(),
        name="gla_stream",
    )(qb, kb, vb, la, masks, s0)


def _gla_seqs(qb, kb, vb, la, s0, *, c):
    t = qb.shape[0]
    n = s0.shape[0]
    assert t == n * c and n % GLA_SEQS_PER_STEP == 0
    masks = jnp.asarray(_gla_masks(c), F32)
    tb = GLA_SEQS_PER_STEP * c
    row = lambda w: pl.BlockSpec((tb, w), lambda i: (i, 0))
    s_spec = pl.BlockSpec((GLA_SEQS_PER_STEP,) + s0.shape[1:], lambda i: (i, 0, 0, 0))
    return pl.pallas_call(
        functools.partial(_gla_seqs_kernel, c=c, n_seqs=GLA_SEQS_PER_STEP),
        grid=(n // GLA_SEQS_PER_STEP,),
        in_specs=[row(DK_B), row(DK_B), row(DV_B), row(DK_B), _const_spec(masks.shape), s_spec],
        out_specs=(row(DV_B), s_spec),
        out_shape=(jax.ShapeDtypeStruct((t, DV_B), BF16), jax.ShapeDtypeStruct(s0.shape, F32)),
        compiler_params=_params(),
        name="gla_seqs",
    )(qb, kb, vb, la, masks, s0)


def _merge_ffn_kernel(x_ref, oa_ref, ob_ref, rs_ref, ga_ref, gb_ref, gn_ref, gpost_ref, gpre_ref, gfpost_ref,
                      wpa_ref, wpb_ref, wout_ref, wg_ref, wu_ref, wd_ref, y_ref, x1_ref, h_ref, act_ref):
    group_rows = TM // MERGE_ROW_GROUPS
    groups = [slice(r * group_rows, (r + 1) * group_rows) for r in range(MERGE_ROW_GROUPS)]

    def gated_mix(rows):
        pa = _dot(oa_ref[rows, :], wpa_ref[...])
        ob = jnp.concatenate(
            [_rms(ob_ref[rows, h * DV_HEAD_B:(h + 1) * DV_HEAD_B].astype(F32)) * gn_ref[...]
             for h in range(N_HEADS_B)], axis=1) * rs_ref[rows, :].astype(F32)
        pb = _dot(ob.astype(BF16), wpb_ref[...])
        mix = jax.nn.sigmoid(ga_ref[rows, :].astype(F32)) * pa + jax.nn.sigmoid(gb_ref[rows, :].astype(F32)) * pb
        return (mix.astype(BF16),)

    def out_proj(rows, mix):
        x1 = x_ref[rows, :] + _rms(_dot(mix, wout_ref[...])) * gpost_ref[...]
        x1_ref[rows, :] = x1
        h_ref[rows, :] = (_rms(x1) * gpre_ref[...]).astype(BF16)

    _two_phase_pipeline([(rows,) for rows in groups], gated_mix, out_proj)

    for j in range(N_FF_CHUNKS):
        cols = slice(j * FF_CHUNK, (j + 1) * FF_CHUNK)
        for rows in (groups if j == 0 else [slice(0, TM)]):
            h = h_ref[rows, :]
            g = _dot(h, wg_ref[:, cols])
            u = _dot(h, wu_ref[:, cols])
            act_ref[rows, cols] = (g * jax.nn.sigmoid(g) * u).astype(BF16)
    y_ref[...] = x1_ref[...] + _rms(_dot(act_ref[...], wd_ref[...])) * gfpost_ref[...]


def _merge_ffn(x, oa, ob, rs, ga, gb, gnorm, gpost, gpre, gfpost, wpa, wpb, wout, wg, wu, wd):
    t = x.shape[0]
    n = t // TM
    row = lambda w: pl.BlockSpec((TM, w), lambda i: (i, 0))
    vec = _const_spec((1, D_MODEL))
    return pl.pallas_call(
        _merge_ffn_kernel,
        grid=(n,),
        in_specs=[row(D_MODEL), row(D_A), row(DV_B), row(DV_B), row(D_MODEL), row(D_MODEL),
                  _const_spec((1, DV_HEAD_B)), vec, vec, vec,
                  _const_spec(wpa.shape), _const_spec(wpb.shape), _const_spec(wout.shape),
                  _const_spec(wg.shape), _const_spec(wu.shape), _const_spec(wd.shape)],
        out_specs=row(D_MODEL),
        out_shape=jax.ShapeDtypeStruct((t, D_MODEL), F32),
        scratch_shapes=[pltpu.VMEM((TM, D_MODEL), F32), pltpu.VMEM((TM, D_MODEL), BF16),
                        pltpu.VMEM((TM, D_FF), BF16)],
        compiler_params=_params(),
        name="merge_ffn",
    )(x, oa, ob, rs, ga, gb, gnorm, gpost, gpre, gfpost, wpa, wpb, wout, wg, wu, wd)


def _rel_bias_row(table):
    n_hi = BAND_PAST + BIAS_ORIGIN - REL_CLIP
    n_lo = BIAS_F_LEN - n_hi - (2 * REL_CLIP + 1)
    h = table.shape[0]
    return jnp.concatenate([jnp.broadcast_to(table[:, -1:], (h, n_hi)), table[:, ::-1],
                            jnp.broadcast_to(table[:, :1], (h, n_lo))], axis=1)


def kernel(x_prompt, x_sample, cache_attn_k, cache_attn_v, state_gla, norm_mix_pre, norm_mix_post, norm_ffn_pre, norm_ffn_post, w_in, w_decay_up, b_decay, rel_bias, gla_norm, w_proj_a, w_proj_b, w_out, w_ffn_gate, w_ffn_up, w_ffn_down):
    depth = w_in.shape[0]
    assert depth == 1, "single-layer step"
    batch, seq, _ = x_prompt.shape
    dec_batch, dec_seq, _ = x_sample.shape
    assert batch == 1 and seq % TM == 0 and (dec_batch * dec_seq) % TM == 0
    assert seq % (ATT_SUB * ATT_QB) == 0
    past = cache_attn_k.shape[2]

    wt_bf = jnp.transpose(w_in[0]).astype(BF16)
    wup = w_decay_up[0].astype(BF16)
    bdec = b_decay[0][None, :]
    vec = lambda a: a[0][None, :]
    wpa, wpb, wout = w_proj_a[0].astype(BF16), w_proj_b[0].astype(BF16), w_out[0].astype(BF16)
    wg, wu, wd = w_ffn_gate[0].astype(BF16), w_ffn_up[0].astype(BF16), w_ffn_down[0].astype(BF16)
    gnorm = gla_norm[0][None, :]

    def layer_tail(x, oa, ob, rs, ga, gb):
        return _merge_ffn(x, oa, ob, rs, ga, gb, gnorm, vec(norm_mix_post), vec(norm_ffn_pre), vec(norm_ffn_post),
                          wpa, wpb, wout, wg, wu, wd)

    xp = x_prompt[0]
    qa, ka, va, qb, kb, vb, rs, la, ga, gb, kf, vf = _in_proj(
        xp, vec(norm_mix_pre), wt_bf, wup, bdec, kv_rows_every_step=False)
    f_tab = _rel_bias_row(rel_bias[0])
    oa = _attn_prompt(qa, ka, va, f_tab)
    s0 = jnp.zeros((N_HEADS_B, DK_HEAD_B, DV_HEAD_B), F32)
    ob, sp = _gla_stream(qb, kb, vb, la, s0, c=GLA_CHUNK, tb=GLA_TB)
    yp = layer_tail(xp, oa, ob, rs, ga, gb)
    keep = min(BAND_PAST, seq)
    assert keep == TM

    xs = x_sample.reshape(dec_batch * dec_seq, D_MODEL)
    qa, ka, va, qb, kb, vb, rs, la, ga, gb, kfs, vfs = _in_proj(
        xs, vec(norm_mix_pre), wt_bf, wup, bdec, kv_rows_every_step=True)
    assert past == BAND_PAST
    oa = _attn_sample(qa, kfs, vfs, jnp.transpose(cache_attn_k[0], (0, 2, 3, 1)),
                      jnp.transpose(cache_attn_v[0], (0, 2, 3, 1)), f_tab, seq=dec_seq)
    gla_chunk = CHUNK if dec_seq % CHUNK == 0 else dec_seq
    assert gla_chunk == dec_seq
    ob, ss = _gla_seqs(qb, kb, vb, la, state_gla[0], c=gla_chunk)
    ys = layer_tail(xs, oa, ob, rs, ga, gb)

    hd = (N_HEADS_A, HEAD_DIM_A)
    return (yp[None], ys.reshape(dec_batch, dec_seq, D_MODEL),
            kf.reshape((1, 1, keep) + hd), vf.reshape((1, 1, keep) + hd), sp[None, None],
            kfs.reshape((1, dec_batch, dec_seq) + hd), vfs.reshape((1, dec_batch, dec_seq) + hd), ss[None])
```

```python
import functools

import numpy as np
import jax
import jax.numpy as jnp
from jax import lax
from jax.experimental import pallas as pl
from jax.experimental.pallas import tpu as pltpu

F32 = jnp.float32
BF16 = jnp.bfloat16

D_MODEL = 1024
CHUNK = 64
BAND_CHUNKS = 8
BAND_PAST = BAND_CHUNKS * CHUNK
N_HEADS_A = 8
HEAD_DIM_A = 64
D_A = N_HEADS_A * HEAD_DIM_A
REL_CLIP = 128
N_HEADS_B = 4
DK_HEAD_B = 128
DV_HEAD_B = 256
DK_B = N_HEADS_B * DK_HEAD_B
DV_B = N_HEADS_B * DV_HEAD_B
GATE_RANK = 16
GATE_TEMP = 16.0
D_FF = 2816
EPS = 1e-6

LANES = 128
SUBLANES = 8
BF16_ROWS = 16
LOG2E = 1.4426950408889634
VMEM_LIMIT_BYTES = 56 * 1024 * 1024
NEG_BIG = -1e30

_PIECES = (("qa", D_A), ("ka", D_A), ("va", D_A), ("qb", DK_B), ("kb", DK_B), ("vb", DV_B),
           ("rb", DV_B), ("dlr", GATE_RANK), ("ga", D_MODEL), ("gb", D_MODEL))
_OFF = {}
_o = 0
for _n, _w in _PIECES:
    _OFF[_n] = (_o, _o + _w)
    _o += _w
D_IN = _o
assert all(lo % BF16_ROWS == 0 for lo, _ in _OFF.values())
assert DK_B == 4 * LANES

TM = 512
ATT_QB = 256
ATT_WIN = ATT_QB + BAND_PAST
ATT_SUB = 4
ATT_MASKED = BAND_PAST // ATT_QB
assert ATT_MASKED <= ATT_SUB
BIAS_ORIGIN = ATT_QB
BIAS_F_LEN = 1024
GLA_TB = 2048
GLA_CHUNK = 128
PIPELINE_DEPTH = 2
GLA_SEQS_PER_STEP = 8
ATTS_SEQS_PER_STEP = 4
CACHE_SLOTS = 3
FF_CHUNK = 256
N_FF_CHUNKS = D_FF // FF_CHUNK
MERGE_ROW_GROUPS = 2


def _const_spec(shape):
    nd = len(shape)
    return pl.BlockSpec(shape, lambda i: (0,) * nd, pipeline_mode=pl.Buffered(1))


def _params():
    return pltpu.CompilerParams(dimension_semantics=("arbitrary",), vmem_limit_bytes=VMEM_LIMIT_BYTES)


def _two_phase_pipeline(units, first, second):
    pending = []
    for u in units:
        pending.append((u, first(*u)))
        if len(pending) > PIPELINE_DEPTH:
            u_done, carried = pending.pop(0)
            second(*u_done, *carried)
    for u_done, carried in pending:
        second(*u_done, *carried)


def _rms(x):
    return x * lax.rsqrt(jnp.mean(x * x, axis=-1, keepdims=True) + EPS)


def _dot(a, b):
    return jnp.dot(a, b, preferred_element_type=F32)


def _dot_nt(a, b):
    return lax.dot_general(a, b, (((1,), (1,)), ((), ())), preferred_element_type=F32)


def _dot_tn(a, b):
    return lax.dot_general(a, b, (((0,), (0,)), ((), ())), preferred_element_type=F32)


def _in_proj_kernel(x_ref, g_ref, wt_ref, wup_ref, bdec_ref,
                    qa_ref, ka_ref, va_ref, qb_ref, kb_ref, vb_ref, rs_ref, la_ref, ga_ref, gb_ref,
                    kf_ref, vf_ref):
    h = (_rms(x_ref[...]) * g_ref[...]).astype(BF16)

    def proj(name):
        lo, hi = _OFF[name]
        return _dot_nt(h, wt_ref[lo:hi, :])

    dlr = proj("dlr").astype(BF16)
    qa_ref[...] = (proj("qa") * (HEAD_DIM_A ** -0.5 * LOG2E)).astype(BF16)
    ka = proj("ka")
    va = proj("va")
    ka_ref[...] = ka.astype(BF16)
    va_ref[...] = va.astype(BF16)
    kf_ref[...] = ka
    vf_ref[...] = va

    def log_decay(part):
        cols = slice(part * LANES, (part + 1) * LANES)
        z = (_dot(dlr, wup_ref[:, cols]) + bdec_ref[:, cols]) * LOG2E
        la_ref[:, cols] = (jnp.minimum(z, 0.0) - jnp.log2(1.0 + jnp.exp2(-jnp.abs(z)))) * (1.0 / GATE_TEMP)

    log_decay(0)
    qb_ref[...] = (proj("qb") * (DK_HEAD_B ** -0.5)).astype(BF16)
    log_decay(1)
    kb_ref[...] = proj("kb").astype(BF16)
    log_decay(2)
    vb_ref[...] = proj("vb").astype(BF16)
    log_decay(3)
    half_r = 0.5 * proj("rb")
    rs_ref[...] = (half_r + half_r * jnp.tanh(half_r)).astype(BF16)
    ga_ref[...] = proj("ga").astype(BF16)
    gb_ref[...] = proj("gb").astype(BF16)


def _in_proj(x, g, wt_bf, wup, bdec, *, kv_rows_every_step):
    t = x.shape[0]
    n = t // TM
    row = lambda w: pl.BlockSpec((TM, w), lambda i: (i, 0))
    if kv_rows_every_step:
        kv_rows, kv_spec = t, row(D_A)
    else:
        kv_rows, kv_spec = TM, pl.BlockSpec((TM, D_A), lambda i: (0, 0))
    widths = (D_A, D_A, D_A, DK_B, DK_B, DV_B, DV_B, DK_B, D_MODEL, D_MODEL)
    dtypes = (BF16,) * 7 + (F32, BF16, BF16)
    out_shape = tuple(jax.ShapeDtypeStruct((t, w), d) for w, d in zip(widths, dtypes))
    out_shape += (jax.ShapeDtypeStruct((kv_rows, D_A), F32),) * 2
    out_specs = tuple(row(w) for w in widths) + (kv_spec, kv_spec)
    return pl.pallas_call(
        _in_proj_kernel,
        grid=(n,),
        in_specs=[row(D_MODEL), _const_spec((1, D_MODEL)), _const_spec((D_IN, D_MODEL)),
                  _const_spec((GATE_RANK, DK_B)), _const_spec((1, DK_B))],
        out_specs=out_specs,
        out_shape=out_shape,
        compiler_params=_params(),
        name="in_proj",
    )(x, g, wt_bf, wup, bdec)


def _softmax_pv(s, v):
    m = jnp.max(s, axis=-1, keepdims=True)
    e = jnp.exp2(s - m).astype(BF16)
    pv = _dot(e, jnp.concatenate([v, jnp.ones_like(v)], axis=1))
    return pv[:, :LANES] / pv[:, LANES:LANES + 1]


def _expand_rel_bias(f_ref, bias_ref, n_q, n_k, first_valid, n_valid, band):
    t = lax.broadcasted_iota(jnp.int32, (n_q, n_k), 0)
    k = lax.broadcasted_iota(jnp.int32, (n_q, n_k), 1)
    keep = (k < n_valid) & (k >= first_valid)
    if band:
        jq, jk = t // CHUNK, k // CHUNK
        keep = keep & (jk >= jq) & (jk <= jq + BAND_CHUNKS)
    for h in range(N_HEADS_A):
        rows = jnp.broadcast_to(f_ref[h:h + 1, :], (n_q, BIAS_F_LEN))
        rolled = pltpu.roll(rows, BIAS_F_LEN - BIAS_ORIGIN, 1, stride=1, stride_axis=0)
        bias_ref[h] = jnp.where(keep, rolled[:, :n_k] * LOG2E, NEG_BIG)


def _attn_prompt_kernel(q_ref, *refs):
    n_kb = ATT_SUB + ATT_WIN // ATT_QB - 1
    k_refs, v_refs = refs[:n_kb], refs[n_kb:2 * n_kb]
    f_ref, o_ref, bias_ref = refs[2 * n_kb:]
    i = pl.program_id(0)

    @pl.when(i == 0)
    def _():
        _expand_rel_bias(f_ref, bias_ref.at[0], ATT_QB, ATT_WIN, 0, ATT_WIN, band=True)
        for g in range(ATT_MASKED):
            _expand_rel_bias(f_ref, bias_ref.at[1 + g], ATT_QB, ATT_WIN, BAND_PAST - ATT_QB * g, ATT_WIN,
                             band=True)

    rows = lambda sub: slice(sub * ATT_QB, (sub + 1) * ATT_QB)
    lanes = lambda p: slice(p * LANES, (p + 1) * LANES)
    window = lambda refs, sub, p: jnp.concatenate(
        [r[:, lanes(p)] for r in refs[sub:sub + ATT_WIN // ATT_QB]], axis=0)
    first = lax.broadcasted_iota(jnp.int32, (1, LANES), 1) < HEAD_DIM_A

    def scores(sub, p):
        which = jnp.where(i == 0, 1 + sub, 0) if sub < ATT_MASKED else 0
        q_pair = q_ref[rows(sub), lanes(p)]
        zero = jnp.zeros_like(q_pair)
        qs = jnp.concatenate([jnp.where(first, q_pair, zero), jnp.where(first, zero, q_pair)], axis=0)
        bias = jnp.concatenate([bias_ref[which, 2 * p], bias_ref[which, 2 * p + 1]], axis=0)
        return (_dot_nt(qs, window(k_refs, sub, p)) + bias,)

    def softmax_pv(sub, p, s):
        pv = _softmax_pv(s, window(v_refs, sub, p))
        o_ref[rows(sub), lanes(p)] = jnp.where(first, pv[:ATT_QB], pv[ATT_QB:]).astype(BF16)

    _two_phase_pipeline([(sub, p) for sub in range(ATT_SUB) for p in range(N_HEADS_A // 2)], scores, softmax_pv)


def _attn_prompt(qa, ka, va, f_tab):
    t = qa.shape[0]
    n = t // (ATT_SUB * ATT_QB)
    n_kb = ATT_SUB + ATT_WIN // ATT_QB - 1
    kblk = [pl.BlockSpec((ATT_QB, D_A), functools.partial(
        lambda j, i: (jnp.maximum(ATT_SUB * i + j - (n_kb - ATT_SUB), 0), 0), j)) for j in range(n_kb)]
    qblk = pl.BlockSpec((ATT_SUB * ATT_QB, D_A), lambda i: (i, 0))
    return pl.pallas_call(
        _attn_prompt_kernel,
        grid=(n,),
        in_specs=[qblk] + kblk + kblk + [_const_spec((N_HEADS_A, BIAS_F_LEN))],
        out_specs=qblk,
        out_shape=jax.ShapeDtypeStruct((t, D_A), BF16),
        scratch_shapes=[pltpu.VMEM((1 + ATT_MASKED, N_HEADS_A, ATT_QB, ATT_WIN), F32)],
        compiler_params=_params(),
        name="attn_prompt",
    )(qa, *([ka] * n_kb), *([va] * n_kb), f_tab)


def _attn_sample_kernel(q_ref, kt_hbm, vt_hbm, kn_ref, vn_ref, f_ref, o_ref, bias_ref, kt_buf, vt_buf, sem,
                        *, n_seqs, n_steps):
    n_q, w = q_ref.shape[0] // n_seqs, kt_hbm.shape[-1]
    i = pl.program_id(0)

    def cache_copies(step, slot):
        src = pl.ds(step * n_seqs, n_seqs)
        return (pltpu.make_async_copy(kt_hbm.at[src], kt_buf.at[slot], sem.at[0, slot]),
                pltpu.make_async_copy(vt_hbm.at[src], vt_buf.at[slot], sem.at[1, slot]))

    @pl.when(i == 0)
    def _():
        for step in range(min(CACHE_SLOTS - 1, n_steps)):
            for c in cache_copies(step, step):
                c.start()
        _expand_rel_bias(f_ref, bias_ref, n_q, bias_ref.shape[-1], 0, w + n_q, band=False)

    ahead = i + (CACHE_SLOTS - 1)

    @pl.when(ahead < n_steps)
    def _():
        for c in cache_copies(ahead, lax.rem(ahead, CACHE_SLOTS)):
            c.start()

    slot = lax.rem(i, CACHE_SLOTS)
    for c in cache_copies(i, slot):
        c.wait()
    kt_ref, vt_ref = kt_buf.at[slot], vt_buf.at[slot]

    lane = lax.broadcasted_iota(jnp.int32, (1, LANES), 1)
    first = lane < HEAD_DIM_A
    rows = lambda j: slice(j * n_q, (j + 1) * n_q)
    lanes = lambda p: slice(p * LANES, (p + 1) * LANES)

    def scores(j, p):
        q_pair = q_ref[rows(j), lanes(p)]
        zero = jnp.zeros_like(q_pair)
        qs = jnp.concatenate([jnp.where(first, q_pair, zero), jnp.where(first, zero, q_pair)], axis=0)
        kt = jnp.concatenate([kt_ref[j, 2 * p], kt_ref[j, 2 * p + 1]], axis=0).astype(BF16)
        kn = kn_ref[rows(j), lanes(p)].astype(BF16)
        bias = jnp.concatenate([bias_ref[2 * p], bias_ref[2 * p + 1]], axis=0)
        return _dot(qs, kt) + bias[:, :w], _dot_nt(qs, kn) + bias[:, w:w + n_q]

    def softmax_pv(j, p, s_c, s_n):
        vt = jnp.concatenate([vt_ref[j, 2 * p], vt_ref[j, 2 * p + 1]], axis=0).astype(BF16)
        vn = vn_ref[rows(j), lanes(p)].astype(BF16)
        m = jnp.maximum(jnp.max(s_c, axis=-1, keepdims=True), jnp.max(s_n, axis=-1, keepdims=True))
        e_c = jnp.exp2(s_c - m).astype(BF16)
        e_n = jnp.exp2(s_n - m).astype(BF16)
        pv = (_dot_nt(e_c, jnp.concatenate([vt, jnp.ones_like(vt)], axis=0))
              + _dot(e_n, jnp.concatenate([vn, jnp.ones_like(vn)], axis=1)))
        pv = pv[:, :LANES] / pv[:, LANES:LANES + 1]
        o_ref[rows(j), lanes(p)] = jnp.where(first, pv[:n_q], pv[n_q:]).astype(BF16)

    _two_phase_pipeline([(j, p) for j in range(n_seqs) for p in range(N_HEADS_A // 2)], scores, softmax_pv)


def _attn_sample(qa, kn, vn, cache_kt, cache_vt, f_tab, *, seq):
    t = qa.shape[0]
    nb, _, _, w = cache_kt.shape
    n_k = w + seq
    n_seqs = ATTS_SEQS_PER_STEP
    assert nb % n_seqs == 0
    row = pl.BlockSpec((n_seqs * seq, D_A), lambda i: (i, 0))
    cache = pl.BlockSpec(memory_space=pl.ANY)
    cache_ring = pltpu.VMEM((CACHE_SLOTS, n_seqs, N_HEADS_A, HEAD_DIM_A, w), F32)
    return pl.pallas_call(
        functools.partial(_attn_sample_kernel, n_seqs=n_seqs, n_steps=nb // n_seqs),
        grid=(nb // n_seqs,),
        in_specs=[row, cache, cache, row, row, _const_spec((N_HEADS_A, BIAS_F_LEN))],
        out_specs=row,
        out_shape=jax.ShapeDtypeStruct((t, D_A), BF16),
        scratch_shapes=[pltpu.VMEM((N_HEADS_A, seq, n_k + (-n_k) % LANES), F32), cache_ring, cache_ring,
                        pltpu.SemaphoreType.DMA((2, CACHE_SLOTS))],
        compiler_params=_params(),
        name="attn_sample",
    )(qa, cache_kt, cache_vt, kn, vn, f_tab)


def _gla_levels(c):
    out, hs = [], 1
    while hs < c:
        out.append(hs)
        hs *= 2
    return out


def _gla_masks(c):
    t = np.arange(c)
    masks = []
    for hs in _gla_levels(c):
        blk = t // (2 * hs)
        second = (t % (2 * hs)) >= hs
        masks.append((blk[:, None] == blk[None, :]) & second[:, None] & ~second[None, :])
    masks.append(np.eye(c, dtype=bool))
    return np.stack(masks).astype(np.float32)


def _split_row_bcast(x, hs, row):
    c, w = x.shape
    blk = 2 * hs
    if blk >= SUBLANES:
        xr = x.reshape(c // blk, blk, w)
        return jnp.broadcast_to(xr[:, hs - 1:hs, :], (c // blk, blk, w)).reshape(c, w)
    tiles = (c // SUBLANES, SUBLANES, w)
    x3, pos = x.reshape(tiles), (row & (blk - 1)).reshape(tiles)
    if hs == 1:
        out = jnp.where(pos == 1, pltpu.roll(x3, 1, 1), x3)
    else:
        assert hs == 2
        nxt = jnp.where((pos & 1) == 1, x3, pltpu.roll(x3, SUBLANES - 1, 1))
        out = jnp.where(pos >= 2, pltpu.roll(nxt, 2, 1), nxt)
    return out.reshape(c, w)


def _level_step(p, qf, kf, hs, row):
    c, w = p.shape
    if hs < SUBLANES:
        tot = _split_row_bcast(p, hs, row)
        second = (row & hs) != 0
        return jnp.where(second, p, tot - p), jnp.where(second, p + tot, p), jnp.where(second, qf, kf)
    xs, ps, qks = [], [], []
    for lo in range(0, c, 2 * hs):
        first, second = p[lo:lo + hs], p[lo + hs:lo + 2 * hs]
        tot = jnp.broadcast_to(first[hs - 1:hs], (hs, w))
        xs += [tot - first, second]
        ps += [first, second + tot]
        qks += [kf[lo:lo + hs], qf[lo + hs:lo + 2 * hs]]
    if hs % BF16_ROWS == 0:
        qk = jnp.concatenate(qks, axis=0)
    else:
        qk = jnp.where((row & hs) != 0, qf, kf)
    return jnp.concatenate(xs, axis=0), jnp.concatenate(ps, axis=0), qk


def _gla_intra(rows, c, h, q_ref, k_ref, la_ref, mask_ref):
    levels = _gla_levels(c)
    n_lvl = len(levels)
    row = lax.broadcasted_iota(jnp.int32, (c, DK_HEAD_B), 0)
    hk = slice(h * DK_HEAD_B, (h + 1) * DK_HEAD_B)
    qf = q_ref[rows, hk]
    kf = k_ref[rows, hk]
    p = la_ref[rows, hk]
    a = mask_ref[n_lvl] * _dot_nt(qf, kf)
    for l, hs in enumerate(levels):
        x, p, qk = _level_step(p, qf, kf, hs, row)
        z = qk * jnp.exp2(x).astype(BF16)
        a = a + mask_ref[l] * _dot_nt(z, z)
    b = p
    b_last = b[c - 1:c, :]
    qd = qf * jnp.exp2(b).astype(BF16)
    kd = kf * jnp.exp2(b_last - b).astype(BF16)
    d_last = jnp.exp2(b_last)
    return a.astype(BF16), qd, kd, d_last


def _gla_inter(rows, h, v_ref, st_in, st_out, o_ref, a, qd, kd, d_last, *, state_t):
    hv = slice(h * DV_HEAD_B, (h + 1) * DV_HEAD_B)
    vh = v_ref[rows, hv]
    o = _dot(a, vh)
    st = st_in[h]
    if state_t:
        o = o + _dot_nt(qd, st.astype(BF16))
        st_out[h] = st * d_last + _dot_tn(vh, kd)
    else:
        o = o + _dot(qd, st.astype(BF16))
        d_col = jnp.broadcast_to(d_last, (SUBLANES, DK_HEAD_B)).T[:, 0:1]
        st_out[h] = st * d_col + _dot_tn(kd, vh)
    o_ref[rows, hv] = o.astype(BF16)


def _gla_stream_kernel(q_ref, k_ref, v_ref, la_ref, mask_ref, s0_ref, o_ref, sout_ref, st_ref, *, c, n_chunks):
    @pl.when(pl.program_id(0) == 0)
    def _():
        for h in range(N_HEADS_B):
            st_ref[h] = s0_ref[h].T

    rows = lambda ci: slice(ci * c, (ci + 1) * c)
    _two_phase_pipeline(
        [(ci, h) for ci in range(n_chunks) for h in range(N_HEADS_B)],
        lambda ci, h: _gla_intra(rows(ci), c, h, q_ref, k_ref, la_ref, mask_ref),
        lambda ci, h, *intra: _gla_inter(rows(ci), h, v_ref, st_ref, st_ref, o_ref, *intra, state_t=True))

    @pl.when(pl.program_id(0) == pl.num_programs(0) - 1)
    def _():
        for h in range(N_HEADS_B):
            sout_ref[h] = st_ref[h].T


def _gla_seqs_kernel(q_ref, k_ref, v_ref, la_ref, mask_ref, s0_ref, o_ref, sout_ref, *, c, n_seqs):
    rows = lambda j: slice(j * c, (j + 1) * c)
    _two_phase_pipeline(
        [(j, h) for j in range(n_seqs) for h in range(N_HEADS_B)],
        lambda j, h: _gla_intra(rows(j), c, h, q_ref, k_ref, la_ref, mask_ref),
        lambda j, h, *intra: _gla_inter(rows(j), h, v_ref, s0_ref.at[j], sout_ref.at[j], o_ref, *intra,
                                        state_t=False))


def _gla_stream(qb, kb, vb, la, s0, *, c, tb):
    t = qb.shape[0]
    assert t % tb == 0 and tb % c == 0
    masks = jnp.asarray(_gla_masks(c), F32)
    row = lambda w: pl.BlockSpec((tb, w), lambda i: (i, 0))
    st_shape = (N_HEADS_B, DK_HEAD_B, DV_HEAD_B)
    s_spec = pl.BlockSpec(st_shape, lambda i: (0, 0, 0))
    n_chunks = tb // c
    return pl.pallas_call(
        functools.partial(_gla_stream_kernel, c=c, n_chunks=n_chunks),
        grid=(t // tb,),
        in_specs=[row(DK_B), row(DK_B), row(DV_B), row(DK_B), _const_spec(masks.shape), s_spec],
        out_specs=(row(DV_B), s_spec),
        out_shape=(jax.ShapeDtypeStruct((t, DV_B), BF16), jax.ShapeDtypeStruct(st_shape, F32)),
        scratch_shapes=[pltpu.VMEM((N_HEADS_B, DV_HEAD_B, DK_HEAD_B), F32)],
        compiler_params=_params(),
        name="gla_stream",
    )(qb, kb, vb, la, masks, s0)


def _gla_seqs(qb, kb, vb, la, s0, *, c):
    t = qb.shape[0]
    n = s0.shape[0]
    assert t == n * c and n % GLA_SEQS_PER_STEP == 0
    masks = jnp.asarray(_gla_masks(c), F32)
    tb = GLA_SEQS_PER_STEP * c
    row = lambda w: pl.BlockSpec((tb, w), lambda i: (i, 0))
    s_spec = pl.BlockSpec((GLA_SEQS_PER_STEP,) + s0.shape[1:], lambda i: (i, 0, 0, 0))
    return pl.pallas_call(
        functools.partial(_gla_seqs_kernel, c=c, n_seqs=GLA_SEQS_PER_STEP),
        grid=(n // GLA_SEQS_PER_STEP,),
        in_specs=[row(DK_B), row(DK_B), row(DV_B), row(DK_B), _const_spec(masks.shape), s_spec],
        out_specs=(row(DV_B), s_spec),
        out_shape=(jax.ShapeDtypeStruct((t, DV_B), BF16), jax.ShapeDtypeStruct(s0.shape, F32)),
        compiler_params=_params(),
        name="gla_seqs",
    )(qb, kb, vb, la, masks, s0)


def _merge_ffn_kernel(x_ref, oa_ref, ob_ref, rs_ref, ga_ref, gb_ref, gn_ref, gpost_ref, gpre_ref, gfpost_ref,
                      wpa_ref, wpb_ref, wout_ref, wg_ref, wu_ref, wd_ref, y_ref, x1_ref, h_ref, act_ref):
    group_rows = TM // MERGE_ROW_GROUPS
    groups = [slice(r * group_rows, (r + 1) * group_rows) for r in range(MERGE_ROW_GROUPS)]

    def gated_mix(rows):
        pa = _dot(oa_ref[rows, :], wpa_ref[...])
        ob = jnp.concatenate(
            [_rms(ob_ref[rows, h * DV_HEAD_B:(h + 1) * DV_HEAD_B].astype(F32)) * gn_ref[...]
             for h in range(N_HEADS_B)], axis=1) * rs_ref[rows, :].astype(F32)
        pb = _dot(ob.astype(BF16), wpb_ref[...])
        mix = jax.nn.sigmoid(ga_ref[rows, :].astype(F32)) * pa + jax.nn.sigmoid(gb_ref[rows, :].astype(F32)) * pb
        return (mix.astype(BF16),)

    def out_proj(rows, mix):
        x1 = x_ref[rows, :] + _rms(_dot(mix, wout_ref[...])) * gpost_ref[...]
        x1_ref[rows, :] = x1
        h_ref[rows, :] = (_rms(x1) * gpre_ref[...]).astype(BF16)

    _two_phase_pipeline([(rows,) for rows in groups], gated_mix, out_proj)

    for j in range(N_FF_CHUNKS):
        cols = slice(j * FF_CHUNK, (j + 1) * FF_CHUNK)
        for rows in (groups if j == 0 else [slice(0, TM)]):
            h = h_ref[rows, :]
            g = _dot(h, wg_ref[:, cols])
            u = _dot(h, wu_ref[:, cols])
            act_ref[rows, cols] = (g * jax.nn.sigmoid(g) * u).astype(BF16)
    y_ref[...] = x1_ref[...] + _rms(_dot(act_ref[...], wd_ref[...])) * gfpost_ref[...]


def _merge_ffn(x, oa, ob, rs, ga, gb, gnorm, gpost, gpre, gfpost, wpa, wpb, wout, wg, wu, wd):
    t = x.shape[0]
    n = t // TM
    row = lambda w: pl.BlockSpec((TM, w), lambda i: (i, 0))
    vec = _const_spec((1, D_MODEL))
    return pl.pallas_call(
        _merge_ffn_kernel,
        grid=(n,),
        in_specs=[row(D_MODEL), row(D_A), row(DV_B), row(DV_B), row(D_MODEL), row(D_MODEL),
                  _const_spec((1, DV_HEAD_B)), vec, vec, vec,
                  _const_spec(wpa.shape), _const_spec(wpb.shape), _const_spec(wout.shape),
                  _const_spec(wg.shape), _const_spec(wu.shape), _const_spec(wd.shape)],
        out_specs=row(D_MODEL),
        out_shape=jax.ShapeDtypeStruct((t, D_MODEL), F32),
        scratch_shapes=[pltpu.VMEM((TM, D_MODEL), F32), pltpu.VMEM((TM, D_MODEL), BF16),
                        pltpu.VMEM((TM, D_FF), BF16)],
        compiler_params=_params(),
        name="merge_ffn",
    )(x, oa, ob, rs, ga, gb, gnorm, gpost, gpre, gfpost, wpa, wpb, wout, wg, wu, wd)


def _rel_bias_row(table):
    n_hi = BAND_PAST + BIAS_ORIGIN - REL_CLIP
    n_lo = BIAS_F_LEN - n_hi - (2 * REL_CLIP + 1)
    h = table.shape[0]
    return jnp.concatenate([jnp.broadcast_to(table[:, -1:], (h, n_hi)), table[:, ::-1],
                            jnp.broadcast_to(table[:, :1], (h, n_lo))], axis=1)


def kernel(x_prompt, x_sample, cache_attn_k, cache_attn_v, state_gla, norm_mix_pre, norm_mix_post, norm_ffn_pre, norm_ffn_post, w_in, w_decay_up, b_decay, rel_bias, gla_norm, w_proj_a, w_proj_b, w_out, w_ffn_gate, w_ffn_up, w_ffn_down):
    depth = w_in.shape[0]
    assert depth == 1, "single-layer step"
    batch, seq, _ = x_prompt.shape
    dec_batch, dec_seq, _ = x_sample.shape
    assert batch == 1 and seq % TM == 0 and (dec_batch * dec_seq) % TM == 0
    assert seq % (ATT_SUB * ATT_QB) == 0
    past = cache_attn_k.shape[2]

    wt_bf = jnp.transpose(w_in[0]).astype(BF16)
    wup = w_decay_up[0].astype(BF16)
    bdec = b_decay[0][None, :]
    vec = lambda a: a[0][None, :]
    wpa, wpb, wout = w_proj_a[0].astype(BF16), w_proj_b[0].astype(BF16), w_out[0].astype(BF16)
    wg, wu, wd = w_ffn_gate[0].astype(BF16), w_ffn_up[0].astype(BF16), w_ffn_down[0].astype(BF16)
    gnorm = gla_norm[0][None, :]

    def layer_tail(x, oa, ob, rs, ga, gb):
        return _merge_ffn(x, oa, ob, rs, ga, gb, gnorm, vec(norm_mix_post), vec(norm_ffn_pre), vec(norm_ffn_post),
                          wpa, wpb, wout, wg, wu, wd)

    xp = x_prompt[0]
    qa, ka, va, qb, kb, vb, rs, la, ga, gb, kf, vf = _in_proj(
        xp, vec(norm_mix_pre), wt_bf, wup, bdec, kv_rows_every_step=False)
    f_tab = _rel_bias_row(rel_bias[0])
    oa = _attn_prompt(qa, ka, va, f_tab)
    s0 = jnp.zeros((N_HEADS_B, DK_HEAD_B, DV_HEAD_B), F32)
    ob, sp = _gla_stream(qb, kb, vb, la, s0, c=GLA_CHUNK, tb=GLA_TB)
    yp = layer_tail(xp, oa, ob, rs, ga, gb)
    keep = min(BAND_PAST, seq)
    assert keep == TM

    xs = x_sample.reshape(dec_batch * dec_seq, D_MODEL)
    qa, ka, va, qb, kb, vb, rs, la, ga, gb, kfs, vfs = _in_proj(
        xs, vec(norm_mix_pre), wt_bf, wup, bdec, kv_rows_every_step=True)
    assert past == BAND_PAST
    oa = _attn_sample(qa, kfs, vfs, jnp.transpose(cache_attn_k[0], (0, 2, 3, 1)),
                      jnp.transpose(cache_attn_v[0], (0, 2, 3, 1)), f_tab, seq=dec_seq)
    gla_chunk = CHUNK if dec_seq % CHUNK == 0 else dec_seq
    assert gla_chunk == dec_seq
    ob, ss = _gla_seqs(qb, kb, vb, la, state_gla[0], c=gla_chunk)
    ys = layer_tail(xs, oa, ob, rs, ga, gb)

    hd = (N_HEADS_A, HEAD_DIM_A)
    return (yp[None], ys.reshape(dec_batch, dec_seq, D_MODEL),
            kf.reshape((1, 1, keep) + hd), vf.reshape((1, 1, keep) + hd), sp[None, None],
            kfs.reshape((1, dec_batch, dec_seq) + hd), vfs.reshape((1, dec_batch, dec_seq) + hd), ss[None])
```
